```python
import math
import jax, jax.numpy as jnp
from jax import lax
import numpy as np

D_MODEL = 1024
BATCH = 2
SEQ = 8192
DEPTH = 1
DEC_BATCH = 8
DEC_SEQ = 16
PAST_LEN = 2048

CHUNK = 64
Q_BLOCK = 128
EPS = 1e-6
NEG_INF = -1e30
ROPE_THETA = 10000.0
MLA_HEADS = 8
Q_LORA = 768
KV_LORA = 512
QK_NOPE = 128
QK_ROPE = 64
V_HEAD = 128
MLA_SCALE = (QK_NOPE + QK_ROPE) ** -0.5
MLA_WIDTH = MLA_HEADS * V_HEAD
RET_HEADS = 4
RET_DK = 128
RET_DV = 256
RET_WIDTH = RET_HEADS * RET_DV
D_FF = 2816
CONV_W = 3
IN_SIZES = (Q_LORA, KV_LORA, QK_ROPE, RET_HEADS * RET_DK, RET_HEADS * RET_DK,
            RET_WIDTH, RET_WIDTH, D_MODEL, D_MODEL)
IN_WIDTH = Q_LORA + KV_LORA + QK_ROPE + 2 * RET_HEADS * RET_DK + 2 * RET_WIDTH + 2 * D_MODEL

kernel_name = "hybrid_mla_retention_convffn_stream_step"


def rms_norm(x, gain):
    xf = x.astype(jnp.float32)
    inv = lax.rsqrt(jnp.mean(xf * xf, axis=-1, keepdims=True) + EPS)
    return (xf * inv * gain.astype(jnp.float32)).astype(x.dtype)


def rope(x, pos):
    half = x.shape[-1] // 2
    freqs = ROPE_THETA ** (-jnp.arange(half, dtype=jnp.float32) / half)
    ang = pos.astype(jnp.float32)[:, None] * freqs[None, :]
    cos = jnp.cos(ang)[:, None, :]
    sin = jnp.sin(ang)[:, None, :]
    xf = x.astype(jnp.float32)
    x1, x2 = xf[..., :half], xf[..., half:]
    out = jnp.concatenate([x1 * cos - x2 * sin, x1 * sin + x2 * cos], axis=-1)
    return out.astype(x.dtype)


def retention_log_decay():
    return jnp.log1p(-(2.0 ** (-5.0 - jnp.arange(RET_HEADS, dtype=jnp.float32))))


def retention_chunk(q, k, v, S, lg):
    L = q.shape[1]
    idx = jnp.arange(L, dtype=jnp.float32)
    diff = idx[:, None] - idx[None, :]
    decay = jnp.where(diff[None] >= 0,
                      jnp.exp(jnp.maximum(diff, 0.0)[None] * lg[:, None, None]), 0.0)
    s = jnp.einsum('bihd,bjhd->bhij', q, k) * decay[None]
    inner = jnp.einsum('bhij,bjhe->bihe', s, v)
    q_dec = jnp.exp((idx + 1.0)[:, None] * lg[None, :])[None, :, :, None]
    cross = jnp.einsum('bihd,bhde->bihe', q, S) * q_dec
    k_dec = k * jnp.exp((L - 1.0 - idx)[:, None] * lg[None, :])[None, :, :, None]
    S_new = S * jnp.exp(L * lg)[None, :, None, None] + jnp.einsum('bjhd,bjhe->bhde', k_dec, v)
    return inner + cross, S_new


def head_group_norm(y, gain):
    mu = jnp.mean(y, axis=-1, keepdims=True)
    yc = y - mu
    var = jnp.mean(yc * yc, axis=-1, keepdims=True)
    return yc * lax.rsqrt(var + EPS) * gain.astype(jnp.float32)


def mla_expand(c_kv, w_kv_up, g_k_nope):
    B, T = c_kv.shape[:2]
    kv = (c_kv @ w_kv_up).reshape(B, T, MLA_HEADS, QK_NOPE + V_HEAD)
    return rms_norm(kv[..., :QK_NOPE], g_k_nope), kv[..., QK_NOPE:]


def mla_scores(qn, qp, k_nope, k_pe):
    s = jnp.einsum('bqhd,bkhd->bhqk', qn, k_nope) + jnp.einsum('bqhr,bkr->bhqk', qp, k_pe)
    return s.astype(jnp.float32) * MLA_SCALE


def mla_prompt_attention(q_nope, q_pe, k_nope, k_pe, v):
    B, T = q_nope.shape[:2]
    nb = T // Q_BLOCK
    qn = jnp.moveaxis(q_nope.reshape(B, nb, Q_BLOCK, MLA_HEADS, QK_NOPE), 1, 0)
    qp = jnp.moveaxis(q_pe.reshape(B, nb, Q_BLOCK, MLA_HEADS, QK_ROPE), 1, 0)
    k_chunk = jnp.arange(T) // CHUNK

    def block(args):
        qn_b, qp_b, i = args
        q_chunk = (i * Q_BLOCK + jnp.arange(Q_BLOCK)) // CHUNK
        s = mla_scores(qn_b, qp_b, k_nope, k_pe)
        s = jnp.where((k_chunk[None, :] <= q_chunk[:, None])[None, None], s, NEG_INF)
        p = jax.nn.softmax(s, axis=-1).astype(v.dtype)
        return jnp.einsum('bhqk,bkhd->bqhd', p, v)

    o = lax.map(block, (qn, qp, jnp.arange(nb)))
    return jnp.moveaxis(o, 0, 1).reshape(B, T, MLA_WIDTH)


def mla_sample_attention(q_nope, q_pe, k_nope, k_pe, v):
    B, T = q_nope.shape[:2]
    p = jax.nn.softmax(mla_scores(q_nope, q_pe, k_nope, k_pe), axis=-1).astype(v.dtype)
    return jnp.einsum('bhqk,bkhd->bqhd', p, v).reshape(B, T, MLA_WIDTH)


def hybrid_layer(x, pos, past, w):
    B, T, _ = x.shape
    h = rms_norm(x, w['g_norm_mix'])
    z = h @ w['w_in']
    offs = [int(o) for o in np.cumsum(IN_SIZES)[:-1]]
    q_lat, kv_lat, k_pe, rq, rk, rv, rg, ga, gb = jnp.split(z, offs, axis=-1)

    q = (rms_norm(q_lat, w['g_q_lat']) @ w['w_q_up']).reshape(B, T, MLA_HEADS, QK_NOPE + QK_ROPE)
    q_nope = rms_norm(q[..., :QK_NOPE], w['g_q_nope'])
    q_pe = rope(rms_norm(q[..., QK_NOPE:], w['g_q_rope']), pos)
    c_kv = rms_norm(kv_lat, w['g_kv_lat'])
    k_pe = rope(rms_norm(k_pe, w['g_k_rope'])[:, :, None, :], pos)[:, :, 0, :]
    if past is None:
        k_nope, v = mla_expand(c_kv, w['w_kv_up'], w['g_k_nope'])
        attn = mla_prompt_attention(q_nope, q_pe, k_nope, k_pe, v)
    else:
        lat_all = jnp.concatenate([past[0].astype(c_kv.dtype), c_kv], axis=1)
        pe_all = jnp.concatenate([past[1].astype(k_pe.dtype), k_pe], axis=1)
        k_nope, v = mla_expand(lat_all, w['w_kv_up'], w['g_k_nope'])
        attn = mla_sample_attention(q_nope, q_pe, k_nope, pe_all, v)

    lg = retention_log_decay()
    rq = rope(rq.reshape(B, T, RET_HEADS, RET_DK), pos).astype(jnp.float32)
    rk = rope(rk.reshape(B, T, RET_HEADS, RET_DK), pos).astype(jnp.float32) * (RET_DK ** -0.5)
    rv = rv.reshape(B, T, RET_HEADS, RET_DV).astype(jnp.float32)
    if past is None:
        nc = T // CHUNK

        def to_chunks(a):
            return jnp.swapaxes(a.reshape(B, nc, CHUNK, *a.shape[2:]), 0, 1)

        def step(S, xs):
            o, S = retention_chunk(xs[0], xs[1], xs[2], S, lg)
            return S, o

        S0 = jnp.zeros((B, RET_HEADS, RET_DK, RET_DV), jnp.float32)
        S_fin, o = lax.scan(step, S0, (to_chunks(rq), to_chunks(rk), to_chunks(rv)))
        ret = jnp.swapaxes(o, 0, 1).reshape(B, T, RET_HEADS, RET_DV)
    else:
        ret, S_fin = retention_chunk(rq, rk, rv, past[2].astype(jnp.float32), lg)
    ret = head_group_norm(ret, w['g_ret_out']).reshape(B, T, RET_WIDTH)
    ret = (jax.nn.silu(rg.astype(jnp.float32)) * ret).astype(x.dtype)

    a_d = attn @ w['w_o_branch'][:MLA_WIDTH]
    r_d = ret @ w['w_o_branch'][MLA_WIDTH:]
    merged = jax.nn.sigmoid(ga) * a_d + jax.nn.sigmoid(gb) * r_d
    x = x + merged @ w['w_out']

    u = rms_norm(x, w['g_norm_ffn']) @ w['w_ffn_up']
    if past is None:
        prev = jnp.zeros((B, CONV_W - 1, 2 * D_FF), u.dtype)
    else:
        prev = past[3].astype(u.dtype)
    up = jnp.concatenate([prev, u], axis=1)
    cw = w['ffn_conv_w']
    c = up[:, 0:T] * cw[0] + up[:, 1:T + 1] * cw[1] + up[:, 2:T + 2] * cw[2] + w['ffn_conv_b']
    gate, val = c[..., :D_FF], c[..., D_FF:]
    x = x + (jax.nn.silu(gate) * val) @ w['w_ffn_down']
    new_state = (c_kv, k_pe, S_fin.astype(x.dtype), up[:, -(CONV_W - 1):])
    return x, new_state


def setup_inputs(seed: int = 0) -> dict:
    key = jax.random.key(seed)
    ks = jax.random.split(key, 32)
    f32 = jnp.float32
    nrm = lambda k, shape, scale: jax.random.normal(k, shape, f32) * scale
    gain = lambda k, n: 1.0 + 0.1 * jax.random.normal(k, (DEPTH,) + n, f32)
    return {
        'x_prompt': nrm(ks[0], (BATCH, SEQ, D_MODEL), 1.0),
        'x_sample': nrm(ks[1], (DEC_BATCH, DEC_SEQ, D_MODEL), 1.0),
        'cache_mla_latent': nrm(ks[2], (DEPTH, DEC_BATCH, PAST_LEN, KV_LORA), 1.0),
        'cache_mla_rope_key': nrm(ks[3], (DEPTH, DEC_BATCH, PAST_LEN, QK_ROPE), 1.0),
        'state_retention': nrm(ks[4], (DEPTH, DEC_BATCH, RET_HEADS, RET_DK, RET_DV), 1.0),
        'state_ffn_conv': nrm(ks[5], (DEPTH, DEC_BATCH, CONV_W - 1, 2 * D_FF), 1.0),
        'g_norm_mix': gain(ks[6], (D_MODEL,)),
        'w_in': nrm(ks[7], (DEPTH, D_MODEL, IN_WIDTH), D_MODEL ** -0.5),
        'g_q_lat': gain(ks[8], (Q_LORA,)),
        'w_q_up': nrm(ks[9], (DEPTH, Q_LORA, MLA_HEADS * (QK_NOPE + QK_ROPE)), Q_LORA ** -0.5),
        'g_q_nope': gain(ks[10], (QK_NOPE,)),
        'g_q_rope': gain(ks[11], (QK_ROPE,)),
        'g_kv_lat': gain(ks[12], (KV_LORA,)),
        'w_kv_up': nrm(ks[13], (DEPTH, KV_LORA, MLA_HEADS * (QK_NOPE + V_HEAD)), KV_LORA ** -0.5),
        'g_k_nope': gain(ks[14], (QK_NOPE,)),
        'g_k_rope': gain(ks[15], (QK_ROPE,)),
        'g_ret_out': gain(ks[16], (RET_HEADS, RET_DV)),
        'w_o_branch': nrm(ks[17], (DEPTH, MLA_WIDTH + RET_WIDTH, D_MODEL), MLA_WIDTH ** -0.5),
        'w_out': nrm(ks[18], (DEPTH, D_MODEL, D_MODEL), D_MODEL ** -0.5),
        'g_norm_ffn': gain(ks[19], (D_MODEL,)),
        'w_ffn_up': nrm(ks[20], (DEPTH, D_MODEL, 2 * D_FF), D_MODEL ** -0.5),
        'ffn_conv_w': nrm(ks[21], (DEPTH, CONV_W, 2 * D_FF), CONV_W ** -0.5),
        'ffn_conv_b': nrm(ks[22], (DEPTH, 2 * D_FF), 0.01),
        'w_ffn_down': nrm(ks[23], (DEPTH, D_FF, D_MODEL), D_FF ** -0.5),
    }


def reference(x_prompt, x_sample, cache_mla_latent, cache_mla_rope_key, state_retention,
              state_ffn_conv, g_norm_mix, w_in, g_q_lat, w_q_up, g_q_nope, g_q_rope,
              g_kv_lat, w_kv_up, g_k_nope, g_k_rope, g_ret_out, w_o_branch, w_out,
              g_norm_ffn, w_ffn_up, ffn_conv_w, ffn_conv_b, w_ffn_down):
    pos_p = jnp.arange(x_prompt.shape[1])
    pos_s = PAST_LEN + jnp.arange(x_sample.shape[1])
    yp, ys = x_prompt, x_sample
    st_p, st_s = [], []
    for l in range(DEPTH):
        wl = dict(g_norm_mix=g_norm_mix[l], w_in=w_in[l], g_q_lat=g_q_lat[l], w_q_up=w_q_up[l],
                  g_q_nope=g_q_nope[l], g_q_rope=g_q_rope[l], g_kv_lat=g_kv_lat[l],
                  w_kv_up=w_kv_up[l], g_k_nope=g_k_nope[l], g_k_rope=g_k_rope[l],
                  g_ret_out=g_ret_out[l], w_o_branch=w_o_branch[l], w_out=w_out[l],
                  g_norm_ffn=g_norm_ffn[l], w_ffn_up=w_ffn_up[l], ffn_conv_w=ffn_conv_w[l],
                  ffn_conv_b=ffn_conv_b[l], w_ffn_down=w_ffn_down[l])
        yp, sp = hybrid_layer(yp, pos_p, None, wl)
        past = (cache_mla_latent[l], cache_mla_rope_key[l], state_retention[l], state_ffn_conv[l])
        ys, ss = hybrid_layer(ys, pos_s, past, wl)
        st_p.append(sp)
        st_s.append(ss)
    stk = lambda lst, i: jnp.stack([s[i] for s in lst], axis=0)
    return (yp, ys, stk(st_p, 0), stk(st_p, 1), stk(st_p, 2), stk(st_p, 3),
            stk(st_s, 0), stk(st_s, 1), stk(st_s, 2), stk(st_s, 3))
```

```python
import functools

import jax
import jax.numpy as jnp
import numpy as np
from jax import lax
from jax.experimental import pallas as pl
from jax.experimental.pallas import tpu as pltpu

F32 = jnp.float32
BF16 = jnp.bfloat16

EPS = 1e-6
NEG_INF = -1e30
ROPE_THETA = 10000.0
CHUNK = 64
LANES = 128
V7X_VMEM_BYTES = 64 * 1024 * 1024

MLA_HEADS = 8
QK_NOPE = 128
QK_ROPE = 64
QK_DIM = QK_NOPE + QK_ROPE
V_HEAD = 128
MLA_SCALE = QK_DIM ** -0.5
RET_HEADS = 4
RET_DK = 128
RET_DV = 256
CONV_W = 3


def _vmem_limit(nbytes):
    return int(min(V7X_VMEM_BYTES - (4 << 20), max(32 << 20, nbytes)))


def _const_spec(shape):
    nd = len(shape)
    return pl.BlockSpec(shape, lambda *_: (0,) * nd, pipeline_mode=pl.Buffered(1))


def _rms_scale(z, width):
    return lax.rsqrt(jnp.sum(z * z, axis=-1, keepdims=True) * (1.0 / width) + EPS)


def _swap_halves_64(z):
    lane = lax.broadcasted_iota(jnp.int32, z.shape, 1)
    return jnp.where((lane % 64) < 32, pltpu.roll(z, 96, 1), pltpu.roll(z, 32, 1))


def _rope_tables(pos, half):
    freqs = ROPE_THETA ** (-jnp.arange(half, dtype=F32) / half)
    ang = pos.astype(F32)[:, None] * freqs[None, :]
    cos, sin = jnp.cos(ang), jnp.sin(ang)
    reps = LANES // (2 * half)
    return (jnp.tile(jnp.concatenate([cos, cos], -1), (1, reps)),
            jnp.tile(jnp.concatenate([-sin, sin], -1), (1, reps)))


_IN_GROUPS = (('qlat', 768), ('kvl', 512), ('kpe', 128), ('rq', 512), ('rk', 512),
              ('rv', 1024), ('rg', 1024), ('ga', 1024), ('gb', 1024))
_IN_OFFS = {}
_o = 0
for _n, _w in _IN_GROUPS:
    _IN_OFFS[_n] = (_o, _o + _w)
    _o += _w
IN_PACKED = _o


def _inproj_kernel(x_ref, gmix_ref, w_ref, gq_ref, gkv_ref, gkpe_ref,
                   c64_ref, s64_ref, c32_ref, s32_ref,
                   qlat_ref, ckv_ref, kpe_ref, rq_ref, rk_ref, rv_ref, rg_ref, ga_ref, gb_ref):
    xf = x_ref[...]
    h = (xf * _rms_scale(xf, xf.shape[-1]) * gmix_ref[...]).astype(BF16)

    def proj(name):
        a, b = _IN_OFFS[name]
        return jnp.dot(h, w_ref[:, a:b], preferred_element_type=F32)

    z = proj('qlat')
    qlat_ref[...] = (z * _rms_scale(z, z.shape[-1]) * gq_ref[...]).astype(BF16)
    z = proj('kvl')
    ckv_ref[...] = z * _rms_scale(z, z.shape[-1]) * gkv_ref[...]
    z = proj('kpe')
    zn = z * _rms_scale(z, QK_ROPE) * gkpe_ref[...]
    kpe = zn * c32_ref[...] + _swap_halves_64(zn) * s32_ref[...]
    kpe_ref[...] = kpe[:, :QK_ROPE]
    c64, s64 = c64_ref[...], s64_ref[...]
    for name, ref, scale in (('rq', rq_ref, 1.0), ('rk', rk_ref, RET_DK ** -0.5)):
        z = proj(name)
        for hd in range(RET_HEADS):
            zh = z[:, hd * RET_DK:(hd + 1) * RET_DK]
            r = zh * c64 + pltpu.roll(zh, RET_DK // 2, 1) * s64
            if scale != 1.0:
                r = r * scale
            ref[:, hd * RET_DK:(hd + 1) * RET_DK] = r.astype(BF16)
    rv_ref[...] = proj('rv').astype(BF16)
    rg_ref[...] = proj('rg')
    ga_ref[...] = proj('ga')
    gb_ref[...] = proj('gb')


def _in_proj(x, w_in_packed, gmix, gq, gkv, gkpe, tabs64, tabs32, *, seq_tiles, tm):
    n, d = x.shape
    row = lambda w: pl.BlockSpec((tm, w), lambda i: (i, 0))
    tab = pl.BlockSpec((tm, LANES), lambda i: (i % seq_tiles, 0))
    out_shapes = (
        jax.ShapeDtypeStruct((n, 768), BF16), jax.ShapeDtypeStruct((n, 512), F32),
        jax.ShapeDtypeStruct((n, QK_ROPE), F32), jax.ShapeDtypeStruct((n, 512), BF16),
        jax.ShapeDtypeStruct((n, 512), BF16), jax.ShapeDtypeStruct((n, 1024), BF16),
        jax.ShapeDtypeStruct((n, 1024), F32), jax.ShapeDtypeStruct((n, 1024), F32),
        jax.ShapeDtypeStruct((n, 1024), F32))
    return pl.pallas_call(
        _inproj_kernel,
        grid=(n // tm,),
        in_specs=[row(d), _const_spec(gmix.shape), _const_spec(w_in_packed.shape),
                  _const_spec(gq.shape), _const_spec(gkv.shape), _const_spec(gkpe.shape),
                  tab, tab, tab, tab],
        out_specs=[row(s.shape[1]) for s in out_shapes],
        out_shape=out_shapes,
        compiler_params=pltpu.CompilerParams(
            dimension_semantics=("arbitrary",), vmem_limit_bytes=_vmem_limit(56 << 20)),
        name="in_proj",
    )(x, gmix, w_in_packed, gq, gkv, gkpe, *tabs64, *tabs32)


def _qup_kernel(ql_ref, w_ref, gn_ref, gr_ref, c32_ref, s32_ref, q_ref):
    ql = ql_ref[...]
    zn = jnp.dot(ql, w_ref[:, :MLA_HEADS * QK_NOPE], preferred_element_type=F32)
    zr = jnp.dot(ql, w_ref[:, MLA_HEADS * QK_NOPE:], preferred_element_type=F32)
    gn = gn_ref[...]
    for hd in range(MLA_HEADS):
        zh = zn[:, hd * QK_NOPE:(hd + 1) * QK_NOPE]
        q_ref[hd, :, :QK_NOPE] = (zh * (_rms_scale(zh, QK_NOPE) * MLA_SCALE) * gn).astype(BF16)
    c32, s32, gr = c32_ref[...], s32_ref[...], gr_ref[...]
    lane = lax.broadcasted_iota(jnp.int32, c32.shape, 1)
    for pair in range(MLA_HEADS // 2):
        zp = zr[:, pair * LANES:(pair + 1) * LANES]
        sq = zp * zp
        lo = jnp.sum(jnp.where(lane < QK_ROPE, sq, 0.0), axis=-1, keepdims=True)
        hi = jnp.sum(sq, axis=-1, keepdims=True) - lo
        inv = jnp.where(lane < QK_ROPE, lax.rsqrt(lo * (1.0 / QK_ROPE) + EPS),
                        lax.rsqrt(hi * (1.0 / QK_ROPE) + EPS))
        z = zp * (inv * MLA_SCALE) * gr
        r = (z * c32 + _swap_halves_64(z) * s32).astype(BF16)
        q_ref[2 * pair, :, QK_NOPE:] = r[:, :QK_ROPE]
        q_ref[2 * pair + 1, :, QK_NOPE:] = r[:, QK_ROPE:]


def _q_up(qlat, w_q, gn, gr, tabs32, *, seq_tiles, tm):
    n = qlat.shape[0]
    tab = pl.BlockSpec((tm, LANES), lambda i: (i % seq_tiles, 0))
    return pl.pallas_call(
        _qup_kernel,
        grid=(n // tm,),
        in_specs=[pl.BlockSpec((tm, qlat.shape[1]), lambda i: (i, 0)), _const_spec(w_q.shape),
                  _const_spec(gn.shape), _const_spec(gr.shape), tab, tab],
        out_specs=pl.BlockSpec((MLA_HEADS, tm, QK_DIM), lambda i: (0, i, 0)),
        out_shape=jax.ShapeDtypeStruct((MLA_HEADS, n, QK_DIM), BF16),
        compiler_params=pltpu.CompilerParams(
            dimension_semantics=("arbitrary",), vmem_limit_bytes=_vmem_limit(40 << 20)),
        name="q_up",
    )(qlat, w_q, gn, gr, *tabs32)


def _kvup_kernel(c_ref, kpe_ref, w_ref, gk_ref, k_ref, v_ref):
    c = c_ref[...].astype(BF16)
    kpe = kpe_ref[...].astype(BF16)
    gk = gk_ref[...]
    kw = MLA_HEADS * QK_NOPE
    for pair in range(MLA_HEADS // 2):
        zk = jnp.dot(c, w_ref[:, pair * 256:(pair + 1) * 256], preferred_element_type=F32)
        zv = jnp.dot(c, w_ref[:, kw + pair * 256:kw + (pair + 1) * 256],
                     preferred_element_type=F32)
        for sub in range(2):
            hd = 2 * pair + sub
            zh = zk[:, sub * QK_NOPE:(sub + 1) * QK_NOPE]
            k_ref[hd, :, :QK_NOPE] = (zh * _rms_scale(zh, QK_NOPE) * gk).astype(BF16)
            k_ref[hd, :, QK_NOPE:] = kpe
            v_ref[hd] = zv[:, sub * V_HEAD:(sub + 1) * V_HEAD].astype(BF16)


def _kv_up(ckv, kpe, w_kv, gk, *, tm):
    n = ckv.shape[0]
    return pl.pallas_call(
        _kvup_kernel,
        grid=(n // tm,),
        in_specs=[pl.BlockSpec((tm, ckv.shape[1]), lambda i: (i, 0)),
                  pl.BlockSpec((tm, QK_ROPE), lambda i: (i, 0)),
                  _const_spec(w_kv.shape), _const_spec(gk.shape)],
        out_specs=[pl.BlockSpec((MLA_HEADS, tm, QK_DIM), lambda i: (0, i, 0)),
                   pl.BlockSpec((MLA_HEADS, tm, V_HEAD), lambda i: (0, i, 0))],
        out_shape=(jax.ShapeDtypeStruct((MLA_HEADS, n, QK_DIM), BF16),
                   jax.ShapeDtypeStruct((MLA_HEADS, n, V_HEAD), BF16)),
        compiler_params=pltpu.CompilerParams(
            dimension_semantics=("arbitrary",), vmem_limit_bytes=_vmem_limit(40 << 20)),
        name="kv_up",
    )(ckv, kpe, w_kv, gk)


def _softmax_step(s, v, m_ref, l_ref, acc_ref):
    m_prev = m_ref[...]
    m_new = jnp.maximum(m_prev, jnp.max(s, axis=-1, keepdims=True))
    alpha = jnp.exp(m_prev - m_new)
    p = jnp.exp(s - m_new)
    l_ref[...] = alpha * l_ref[...] + jnp.sum(p, axis=-1, keepdims=True)
    acc_ref[...] = acc_ref[...] * alpha + jnp.dot(p.astype(BF16), v, preferred_element_type=F32)
    m_ref[...] = m_new


def _qk(q, k):
    return lax.dot_general(q, k, (((1,), (1,)), ((), ())), preferred_element_type=F32)


def _prompt_attn_kernel(q_ref, k_ref, v_ref, o_ref, m_ref, l_ref, acc_ref, *, tq):
    i = pl.program_id(2)
    q = q_ref[0]
    m_ref[...] = jnp.full(m_ref.shape, NEG_INF, F32)
    l_ref[...] = jnp.zeros(l_ref.shape, F32)
    acc_ref[...] = jnp.zeros(acc_ref.shape, F32)

    def kv_block(j):
        start = pl.multiple_of(j * tq, tq)
        return k_ref[0, pl.ds(start, tq), :], v_ref[0, pl.ds(start, tq), :]

    def full_block(j, carry):
        k, v = kv_block(j)
        _softmax_step(_qk(q, k), v, m_ref, l_ref, acc_ref)
        return carry

    lax.fori_loop(0, i, full_block, 0)
    k, v = kv_block(i)
    row = lax.broadcasted_iota(jnp.int32, (tq, tq), 0) // CHUNK
    col = lax.broadcasted_iota(jnp.int32, (tq, tq), 1) // CHUNK
    s = jnp.where(col <= row, _qk(q, k), NEG_INF)
    _softmax_step(s, v, m_ref, l_ref, acc_ref)
    o_ref[...] = (acc_ref[...] / l_ref[...]).astype(o_ref.dtype)


def _prompt_attention(q, k, v, *, batch, seq, tq):
    heads, n, _ = q.shape
    nq = seq // tq
    return pl.pallas_call(
        functools.partial(_prompt_attn_kernel, tq=tq),
        grid=(batch, heads, nq),
        in_specs=[pl.BlockSpec((1, tq, QK_DIM), lambda b, h, i: (h, b * nq + i, 0)),
                  pl.BlockSpec((1, seq, QK_DIM), lambda b, h, i: (h, b, 0)),
                  pl.BlockSpec((1, seq, V_HEAD), lambda b, h, i: (h, b, 0))],
        out_specs=pl.BlockSpec((tq, V_HEAD), lambda b, h, i: (b * nq + i, h)),
        out_shape=jax.ShapeDtypeStruct((n, heads * V_HEAD), BF16),
        scratch_shapes=[pltpu.VMEM((tq, 1), F32), pltpu.VMEM((tq, 1), F32),
                        pltpu.VMEM((tq, V_HEAD), F32)],
        compiler_params=pltpu.CompilerParams(
            dimension_semantics=("arbitrary", "arbitrary", "arbitrary"),
            vmem_limit_bytes=_vmem_limit(40 << 20)),
        name="prompt_attention",
    )(q, k, v)


def _sample_attn_kernel(q_ref, kc_ref, vc_ref, kn_ref, vn_ref, o_ref):
    q = q_ref[0]
    s_c = _qk(q, kc_ref[0])
    s_n = _qk(q, kn_ref[0])
    m = jnp.maximum(jnp.max(s_c, axis=-1, keepdims=True), jnp.max(s_n, axis=-1, keepdims=True))
    p_c = jnp.exp(s_c - m)
    p_n = jnp.exp(s_n - m)
    l = jnp.sum(p_c, axis=-1, keepdims=True) + jnp.sum(p_n, axis=-1, keepdims=True)
    acc = (jnp.dot(p_c.astype(BF16), vc_ref[0], preferred_element_type=F32)
           + jnp.dot(p_n.astype(BF16), vn_ref[0], preferred_element_type=F32))
    o_ref[...] = (acc / l).astype(o_ref.dtype)


def _sample_attention(q, kc, vc, kn, vn, *, batch, past, new):
    heads = q.shape[0]
    return pl.pallas_call(
        _sample_attn_kernel,
        grid=(batch, heads),
        in_specs=[pl.BlockSpec((1, new, QK_DIM), lambda b, h: (h, b, 0)),
                  pl.BlockSpec((1, past, QK_DIM), lambda b, h: (h, b, 0)),
                  pl.BlockSpec((1, past, V_HEAD), lambda b, h: (h, b, 0)),
                  pl.BlockSpec((1, new, QK_DIM), lambda b, h: (h, b, 0)),
                  pl.BlockSpec((1, new, V_HEAD), lambda b, h: (h, b, 0))],
        out_specs=pl.BlockSpec((new, V_HEAD), lambda b, h: (b, h)),
        out_shape=jax.ShapeDtypeStruct((batch * new, heads * V_HEAD), BF16),
        compiler_params=pltpu.CompilerParams(
            dimension_semantics=("arbitrary", "arbitrary")),
        name="sample_attention",
    )(q, kc, vc, kn, vn)


def _retention_kernel(q_ref, k_ref, v_ref, rg_ref, dec_ref, qd_ref, kd_ref, g_ref, s0_ref,
                      o_ref, sfin_ref, state_ref, *, chunk, n_chunks):
    c = pl.program_id(2)

    @pl.when(c == 0)
    def _():
        state_ref[...] = s0_ref[0, 0]

    q, k, v = q_ref[...], k_ref[...], v_ref[...]
    state = state_ref[...]
    qd = qd_ref[0]
    s = _qk(q, k) * dec_ref[0]
    inner = jnp.dot(s.astype(BF16), v, preferred_element_type=F32)
    cross = jnp.dot(q, state.astype(BF16), preferred_element_type=F32) * qd
    k_dec = (k.astype(F32) * kd_ref[0]).astype(BF16)
    kv = lax.dot_general(k_dec, v, (((0,), (0,)), ((), ())), preferred_element_type=F32)
    new_state = state * qd[chunk - 1:chunk, :] + kv
    state_ref[...] = new_state

    o = inner + cross
    mu = jnp.mean(o, axis=-1, keepdims=True)
    oc = o - mu
    var = jnp.mean(oc * oc, axis=-1, keepdims=True)
    y = oc * lax.rsqrt(var + EPS) * g_ref[0]
    rg = rg_ref[...]
    o_ref[...] = (rg * jax.nn.sigmoid(rg) * y).astype(o_ref.dtype)

    @pl.when(c == n_chunks - 1)
    def _():
        sfin_ref[0, 0] = new_state


def _retention_tables(chunk):
    lg = jnp.log1p(-(2.0 ** (-5.0 - jnp.arange(RET_HEADS, dtype=F32))))
    idx = jnp.arange(chunk, dtype=F32)
    diff = idx[:, None] - idx[None, :]
    decay = jnp.where(diff[None] >= 0,
                      jnp.exp(jnp.maximum(diff, 0.0)[None] * lg[:, None, None]), 0.0)
    q_dec = jnp.exp((idx + 1.0)[None, :] * lg[:, None])[..., None]
    k_dec = jnp.exp((chunk - 1.0 - idx)[None, :] * lg[:, None])[..., None]
    return decay, q_dec, k_dec


def _retention(rq, rk, rv, rg, g_ret, s0, *, batch, seq, chunk):
    n = rq.shape[0]
    nc = seq // chunk
    decay, q_dec, k_dec = _retention_tables(chunk)
    tok = lambda w: pl.BlockSpec((chunk, w), lambda b, h, c: (b * nc + c, h))
    per_head = lambda shape: pl.BlockSpec((1,) + shape, lambda b, h, c: (h, 0, 0))
    state_spec = pl.BlockSpec((1, 1, RET_DK, RET_DV), lambda b, h, c: (b, h, 0, 0))
    return pl.pallas_call(
        functools.partial(_retention_kernel, chunk=chunk, n_chunks=nc),
        grid=(batch, RET_HEADS, nc),
        in_specs=[tok(RET_DK), tok(RET_DK), tok(RET_DV), tok(RET_DV),
                  per_head((chunk, chunk)), per_head((chunk, 1)), per_head((chunk, 1)),
                  per_head((1, RET_DV)), state_spec],
        out_specs=[tok(RET_DV), state_spec],
        out_shape=(jax.ShapeDtypeStruct((n, RET_HEADS * RET_DV), BF16),
                   jax.ShapeDtypeStruct((batch, RET_HEADS, RET_DK, RET_DV), F32)),
        scratch_shapes=[pltpu.VMEM((RET_DK, RET_DV), F32)],
        compiler_params=pltpu.CompilerParams(
            dimension_semantics=("arbitrary", "arbitrary", "arbitrary")),
        name="retention",
    )(rq, rk, rv, rg, decay, q_dec, k_dec, g_ret.reshape(RET_HEADS, 1, RET_DV), s0)


def _merge_kernel(x_ref, a_ref, r_ref, ga_ref, gb_ref, wa_ref, wb_ref, wo_ref, o_ref):
    a_d = jnp.dot(a_ref[...], wa_ref[...], preferred_element_type=F32)
    r_d = jnp.dot(r_ref[...], wb_ref[...], preferred_element_type=F32)
    merged = jax.nn.sigmoid(ga_ref[...]) * a_d + jax.nn.sigmoid(gb_ref[...]) * r_d
    o_ref[...] = x_ref[...] + jnp.dot(merged.astype(BF16), wo_ref[...],
                                      preferred_element_type=F32)


def _merge(x, attn, ret, ga, gb, wa, wb, wo, *, tm):
    n, d = x.shape
    row = pl.BlockSpec((tm, d), lambda i: (i, 0))
    return pl.pallas_call(
        _merge_kernel,
        grid=(n // tm,),
        in_specs=[row, row, row, row, row,
                  _const_spec(wa.shape), _const_spec(wb.shape), _const_spec(wo.shape)],
        out_specs=row,
        out_shape=jax.ShapeDtypeStruct((n, d), F32),
        compiler_params=pltpu.CompilerParams(
            dimension_semantics=("arbitrary",), vmem_limit_bytes=_vmem_limit(40 << 20)),
        name="merge",
    )(x, attn, ret, ga, gb, wa, wb, wo)


_PAD_ROWS = 8


def _ffn_kernel(x_ref, g_ref, wup_ref, cw_ref, cb_ref, wdn_ref, prev_ref,
                y_ref, st_ref, ubuf_ref, *, tm, d_ff, n_tiles):
    j = pl.program_id(1)
    lo = _PAD_ROWS - (CONV_W - 1)

    @pl.when(j == 0)
    def _():
        ubuf_ref[0:_PAD_ROWS, :] = jnp.zeros((_PAD_ROWS, ubuf_ref.shape[1]), F32)
        ubuf_ref[lo:_PAD_ROWS, :] = prev_ref[0]

    @pl.when(j > 0)
    def _():
        ubuf_ref[0:_PAD_ROWS, :] = ubuf_ref[tm:tm + _PAD_ROWS, :]

    xf = x_ref[...]
    hn = (xf * _rms_scale(xf, xf.shape[-1]) * g_ref[...]).astype(BF16)
    ubuf_ref[_PAD_ROWS:_PAD_ROWS + tm, :] = jnp.dot(hn, wup_ref[...],
                                                    preferred_element_type=F32)
    c = cb_ref[...]
    for tap in range(CONV_W):
        c = c + ubuf_ref[lo + tap:lo + tap + tm, :] * cw_ref[tap:tap + 1, :]
    gate, val = c[:, :d_ff], c[:, d_ff:]
    act = (gate * jax.nn.sigmoid(gate) * val).astype(BF16)
    y_ref[...] = xf + jnp.dot(act, wdn_ref[...], preferred_element_type=F32)

    @pl.when(j == n_tiles - 1)
    def _():
        st_ref[0] = ubuf_ref[tm + lo:tm + _PAD_ROWS, :]


def _conv_ffn(x, g, w_up, cw, cb, w_dn, prev, *, batch, seq, tm):
    n, d = x.shape
    d_ff = w_dn.shape[0]
    nt = seq // tm
    row = pl.BlockSpec((tm, d), lambda b, j: (b * nt + j, 0))
    st = pl.BlockSpec((1, CONV_W - 1, 2 * d_ff), lambda b, j: (b, 0, 0))
    return pl.pallas_call(
        functools.partial(_ffn_kernel, tm=tm, d_ff=d_ff, n_tiles=nt),
        grid=(batch, nt),
        in_specs=[row, _const_spec(g.shape), _const_spec(w_up.shape), _const_spec(cw.shape),
                  _const_spec(cb.shape), _const_spec(w_dn.shape), st],
        out_specs=[row, st],
        out_shape=(jax.ShapeDtypeStruct((n, d), F32),
                   jax.ShapeDtypeStruct((batch, CONV_W - 1, 2 * d_ff), F32)),
        scratch_shapes=[pltpu.VMEM((tm + _PAD_ROWS, 2 * d_ff), F32)],
        compiler_params=pltpu.CompilerParams(
            dimension_semantics=("arbitrary", "arbitrary"),
            vmem_limit_bytes=_vmem_limit(56 << 20)),
        name="conv_ffn",
    )(x, g, w_up, cw, cb, w_dn, prev)


def _pack_weights(w_in, w_q_up, w_kv_up, w_o_branch, w_out, w_ffn_up, w_ffn_down):
    sizes = (768, 512, 64, 512, 512, 1024, 1024, 1024, 1024)
    offs = np.concatenate([[0], np.cumsum(sizes)])
    parts = []
    for (name, width), a, b in zip(_IN_GROUPS, offs[:-1], offs[1:]):
        part = w_in[:, a:b]
        if b - a < width:
            part = jnp.pad(part, ((0, 0), (0, width - (b - a))))
        parts.append(part)
    w_in_p = jnp.concatenate(parts, axis=1).astype(BF16)
    wq = w_q_up.reshape(w_q_up.shape[0], MLA_HEADS, QK_DIM)
    w_q_p = jnp.concatenate([wq[:, :, :QK_NOPE].reshape(wq.shape[0], -1),
                             wq[:, :, QK_NOPE:].reshape(wq.shape[0], -1)], axis=1).astype(BF16)
    wkv = w_kv_up.reshape(w_kv_up.shape[0], MLA_HEADS, QK_NOPE + V_HEAD)
    w_kv_p = jnp.concatenate([wkv[:, :, :QK_NOPE].reshape(wkv.shape[0], -1),
                              wkv[:, :, QK_NOPE:].reshape(wkv.shape[0], -1)], axis=1).astype(BF16)
    mla_width = MLA_HEADS * V_HEAD
    return (w_in_p, w_q_p, w_kv_p, w_o_branch[:mla_width].astype(BF16),
            w_o_branch[mla_width:].astype(BF16), w_out.astype(BF16),
            w_ffn_up.astype(BF16), w_ffn_down.astype(BF16))


def _layer(x, pos, past, w, *, batch, seq, tm, tq, ret_chunk, ffn_tm):
    n = x.shape[0]
    tabs64 = _rope_tables(pos, RET_DK // 2)
    tabs32 = _rope_tables(pos, QK_ROPE // 2)
    seq_tiles = max(seq // tm, 1)
    if seq < tm:
        tabs64 = tuple(jnp.tile(t, (tm // seq, 1)) for t in tabs64)
        tabs32 = tuple(jnp.tile(t, (tm // seq, 1)) for t in tabs32)
    qlat, ckv, kpe, rq, rk, rv, rg, ga, gb = _in_proj(
        x, w['w_in'], w['g_norm_mix'], w['g_q_lat'], w['g_kv_lat'], w['g_k_rope'],
        tabs64, tabs32, seq_tiles=seq_tiles, tm=tm)
    q = _q_up(qlat, w['w_q_up'], w['g_q_nope'], w['g_q_rope'], tabs32,
              seq_tiles=seq_tiles, tm=tm)
    k_new, v_new = _kv_up(ckv, kpe, w['w_kv_up'], w['g_k_nope'], tm=tm)
    if past is None:
        attn = _prompt_attention(q, k_new, v_new, batch=batch, seq=seq, tq=tq)
        s0 = jnp.zeros((batch, RET_HEADS, RET_DK, RET_DV), F32)
        prev = jnp.zeros((batch, CONV_W - 1, w['w_ffn_up'].shape[1]), F32)
    else:
        lat_c, pe_c, s0, prev = past
        past_len = lat_c.shape[1]
        k_c, v_c = _kv_up(lat_c.reshape(batch * past_len, -1), pe_c.reshape(batch * past_len, -1),
                          w['w_kv_up'], w['g_k_nope'], tm=512)
        attn = _sample_attention(q, k_c, v_c, k_new, v_new, batch=batch, past=past_len, new=seq)
    ret, s_fin = _retention(rq, rk, rv, rg, w['g_ret_out'], s0,
                            batch=batch, seq=seq, chunk=ret_chunk)
    x1 = _merge(x, attn, ret, ga, gb, w['w_o_a'], w['w_o_b'], w['w_out'], tm=tm)
    y, conv_state = _conv_ffn(x1, w['g_norm_ffn'], w['w_ffn_up'], w['ffn_conv_w'],
                              w['ffn_conv_b'], w['w_ffn_down'], prev,
                              batch=batch, seq=seq, tm=ffn_tm)
    return y, (ckv, kpe, s_fin, conv_state)


def kernel(x_prompt, x_sample, cache_mla_latent, cache_mla_rope_key, state_retention, state_ffn_conv, g_norm_mix, w_in, g_q_lat, w_q_up, g_q_nope, g_q_rope, g_kv_lat, w_kv_up, g_k_nope, g_k_rope, g_ret_out, w_o_branch, w_out, g_norm_ffn, w_ffn_up, ffn_conv_w, ffn_conv_b, w_ffn_down):
    depth = w_in.shape[0]
    assert depth == 1, "single-layer trunk"
    bp, tp, d = x_prompt.shape
    bs, ts, _ = x_sample.shape
    past_len = cache_mla_latent.shape[2]
    (w_in_p, w_q_p, w_kv_p, w_o_a, w_o_b, w_out_p, w_up_p, w_dn_p) = _pack_weights(
        w_in[0], w_q_up[0], w_kv_up[0], w_o_branch[0], w_out[0], w_ffn_up[0], w_ffn_down[0])
    row = lambda g: g.reshape(1, -1).astype(F32)
    w = dict(
        w_in=w_in_p, w_q_up=w_q_p, w_kv_up=w_kv_p, w_o_a=w_o_a, w_o_b=w_o_b, w_out=w_out_p,
        w_ffn_up=w_up_p, w_ffn_down=w_dn_p,
        g_norm_mix=row(g_norm_mix[0]), g_q_lat=row(g_q_lat[0]), g_kv_lat=row(g_kv_lat[0]),
        g_k_rope=jnp.pad(row(g_k_rope[0]), ((0, 0), (0, LANES - QK_ROPE))),
        g_q_nope=row(g_q_nope[0]), g_q_rope=jnp.tile(row(g_q_rope[0]), (1, LANES // QK_ROPE)),
        g_k_nope=row(g_k_nope[0]), g_ret_out=g_ret_out[0], g_norm_ffn=row(g_norm_ffn[0]),
        ffn_conv_w=ffn_conv_w[0], ffn_conv_b=row(ffn_conv_b[0]))

    yp, sp = _layer(x_prompt.reshape(bp * tp, d), jnp.arange(tp), None, w,
                    batch=bp, seq=tp, tm=512, tq=512, ret_chunk=256, ffn_tm=256)
    past = (cache_mla_latent[0], cache_mla_rope_key[0], state_retention[0], state_ffn_conv[0])
    ys, ss = _layer(x_sample.reshape(bs * ts, d), past_len + jnp.arange(ts), past, w,
                    batch=bs, seq=ts, tm=bs * ts, tq=None, ret_chunk=ts, ffn_tm=ts)

    def states(st, b, t):
        ckv, kpe, s_fin, conv = st
        return (ckv.reshape(1, b, t, -1), kpe.reshape(1, b, t, -1), s_fin[None], conv[None])

    return (yp.reshape(bp, tp, d), ys.reshape(bs, ts, d)) + states(sp, bp, tp) + states(ss, bs, ts)
```

```python
import functools
import math

import jax
import jax.numpy as jnp
import numpy as np
from jax import lax
from jax.experimental import pallas as pl
from jax.experimental.pallas import tpu as pltpu

F32 = jnp.float32
BF16 = jnp.bfloat16

EPS = 1e-6
NEG_INF = -1e30
ROPE_THETA = 10000.0
CHUNK = 64
LANES = 128
V7X_VMEM_BYTES = 64 * 1024 * 1024

MLA_HEADS = 8
QK_NOPE = 128
QK_ROPE = 64
QK_DIM = QK_NOPE + QK_ROPE
V_HEAD = 128
MLA_SCALE = QK_DIM ** -0.5
Q_PRESCALE = MLA_SCALE * math.log2(math.e)
RET_HEADS = 4
RET_DK = 128
RET_DV = 256
CONV_W = 3


def _vmem_limit(nbytes):
    return int(min(V7X_VMEM_BYTES - (4 << 20), max(32 << 20, nbytes)))


def _const_spec(shape):
    nd = len(shape)
    return pl.BlockSpec(shape, lambda *_: (0,) * nd, pipeline_mode=pl.Buffered(1))


def _rms_scale(z, width):
    return lax.rsqrt(jnp.sum(z * z, axis=-1, keepdims=True) * (1.0 / width) + EPS)


def _swap_halves_64(z):
    lane = lax.broadcasted_iota(jnp.int32, z.shape, 1)
    return jnp.where((lane % 64) < 32, pltpu.roll(z, 96, 1), pltpu.roll(z, 32, 1))


def _rope_tables(pos, half):
    freqs = ROPE_THETA ** (-jnp.arange(half, dtype=F32) / half)
    ang = pos.astype(F32)[:, None] * freqs[None, :]
    cos, sin = jnp.cos(ang), jnp.sin(ang)
    reps = LANES // (2 * half)
    return (jnp.tile(jnp.concatenate([cos, cos], -1), (1, reps)),
            jnp.tile(jnp.concatenate([-sin, sin], -1), (1, reps)))


_IN_GROUPS = (('qlat', 768), ('kvl', 512), ('kpe', 128), ('rq', 512), ('rk', 512),
              ('rv', 1024), ('rg', 1024), ('ga', 1024), ('gb', 1024))
_IN_OFFS = {}
_o = 0
for _n, _w in _IN_GROUPS:
    _IN_OFFS[_n] = (_o, _o + _w)
    _o += _w
IN_PACKED = _o


def _inproj_kernel(x_ref, gmix_ref, w_ref, gq_ref, gkv_ref, gkpe_ref,
                   c64_ref, s64_ref, c32_ref, s32_ref,
                   qlat_ref, ckv_ref, kpe_ref, rq_ref, rk_ref, rv_ref, rg_ref, ga_ref, gb_ref):
    xf = x_ref[...]
    h = (xf * _rms_scale(xf, xf.shape[-1]) * gmix_ref[...]).astype(BF16)

    def proj(name):
        a, b = _IN_OFFS[name]
        return jnp.dot(h, w_ref[:, a:b], preferred_element_type=F32)

    z = proj('qlat')
    qlat_ref[...] = (z * _rms_scale(z, z.shape[-1]) * gq_ref[...]).astype(BF16)
    z = proj('kvl')
    ckv_ref[...] = z * _rms_scale(z, z.shape[-1]) * gkv_ref[...]
    z = proj('kpe')
    zn = z * _rms_scale(z, QK_ROPE) * gkpe_ref[...]
    kpe = zn * c32_ref[...] + _swap_halves_64(zn) * s32_ref[...]
    kpe_ref[...] = kpe[:, :QK_ROPE]
    c64, s64 = c64_ref[...], s64_ref[...]
    for name, ref, scale in (('rq', rq_ref, 1.0), ('rk', rk_ref, RET_DK ** -0.5)):
        z = proj(name)
        for hd in range(RET_HEADS):
            zh = z[:, hd * RET_DK:(hd + 1) * RET_DK]
            r = zh * c64 + pltpu.roll(zh, RET_DK // 2, 1) * s64
            if scale != 1.0:
                r = r * scale
            ref[:, hd * RET_DK:(hd + 1) * RET_DK] = r.astype(BF16)
    rv_ref[...] = proj('rv').astype(BF16)
    rg_ref[...] = proj('rg')
    ga_ref[...] = proj('ga')
    gb_ref[...] = proj('gb')


def _in_proj(x, w_in_packed, gmix, gq, gkv, gkpe, tabs64, tabs32, *, seq_tiles, tm):
    n, d = x.shape
    row = lambda w: pl.BlockSpec((tm, w), lambda i: (i, 0))
    tab = pl.BlockSpec((tm, LANES), lambda i: (i % seq_tiles, 0))
    out_shapes = (
        jax.ShapeDtypeStruct((n, 768), BF16), jax.ShapeDtypeStruct((n, 512), F32),
        jax.ShapeDtypeStruct((n, QK_ROPE), F32), jax.ShapeDtypeStruct((n, 512), BF16),
        jax.ShapeDtypeStruct((n, 512), BF16), jax.ShapeDtypeStruct((n, 1024), BF16),
        jax.ShapeDtypeStruct((n, 1024), F32), jax.ShapeDtypeStruct((n, 1024), F32),
        jax.ShapeDtypeStruct((n, 1024), F32))
    return pl.pallas_call(
        _inproj_kernel,
        grid=(n // tm,),
        in_specs=[row(d), _const_spec(gmix.shape), _const_spec(w_in_packed.shape),
                  _const_spec(gq.shape), _const_spec(gkv.shape), _const_spec(gkpe.shape),
                  tab, tab, tab, tab],
        out_specs=[row(s.shape[1]) for s in out_shapes],
        out_shape=out_shapes,
        compiler_params=pltpu.CompilerParams(
            dimension_semantics=("arbitrary",), vmem_limit_bytes=_vmem_limit(56 << 20)),
        name="in_proj",
    )(x, gmix, w_in_packed, gq, gkv, gkpe, *tabs64, *tabs32)


def _qup_kernel(ql_ref, w_ref, gn_ref, gr_ref, c32_ref, s32_ref, q_ref):
    ql = ql_ref[...]
    zn = jnp.dot(ql, w_ref[:, :MLA_HEADS * QK_NOPE], preferred_element_type=F32)
    zr = jnp.dot(ql, w_ref[:, MLA_HEADS * QK_NOPE:], preferred_element_type=F32)
    gn = gn_ref[...]
    for hd in range(MLA_HEADS):
        zh = zn[:, hd * QK_NOPE:(hd + 1) * QK_NOPE]
        q_ref[hd, :, :QK_NOPE] = (zh * (_rms_scale(zh, QK_NOPE) * Q_PRESCALE) * gn).astype(BF16)
    c32, s32, gr = c32_ref[...], s32_ref[...], gr_ref[...]
    lane = lax.broadcasted_iota(jnp.int32, c32.shape, 1)
    for pair in range(MLA_HEADS // 2):
        zp = zr[:, pair * LANES:(pair + 1) * LANES]
        sq = zp * zp
        lo = jnp.sum(jnp.where(lane < QK_ROPE, sq, 0.0), axis=-1, keepdims=True)
        hi = jnp.sum(sq, axis=-1, keepdims=True) - lo
        inv = jnp.where(lane < QK_ROPE, lax.rsqrt(lo * (1.0 / QK_ROPE) + EPS),
                        lax.rsqrt(hi * (1.0 / QK_ROPE) + EPS))
        z = zp * (inv * Q_PRESCALE) * gr
        r = (z * c32 + _swap_halves_64(z) * s32).astype(BF16)
        q_ref[2 * pair, :, QK_NOPE:] = r[:, :QK_ROPE]
        q_ref[2 * pair + 1, :, QK_NOPE:] = r[:, QK_ROPE:]


def _q_up(qlat, w_q, gn, gr, tabs32, *, seq_tiles, tm):
    n = qlat.shape[0]
    tab = pl.BlockSpec((tm, LANES), lambda i: (i % seq_tiles, 0))
    return pl.pallas_call(
        _qup_kernel,
        grid=(n // tm,),
        in_specs=[pl.BlockSpec((tm, qlat.shape[1]), lambda i: (i, 0)), _const_spec(w_q.shape),
                  _const_spec(gn.shape), _const_spec(gr.shape), tab, tab],
        out_specs=pl.BlockSpec((MLA_HEADS, tm, QK_DIM), lambda i: (0, i, 0)),
        out_shape=jax.ShapeDtypeStruct((MLA_HEADS, n, QK_DIM), BF16),
        compiler_params=pltpu.CompilerParams(
            dimension_semantics=("arbitrary",), vmem_limit_bytes=_vmem_limit(40 << 20)),
        name="q_up",
    )(qlat, w_q, gn, gr, *tabs32)


def _kvup_kernel(c_ref, kpe_ref, w_ref, gk_ref, k_ref, v_ref, *, transpose_v):
    c = c_ref[...].astype(BF16)
    kpe = kpe_ref[...].astype(BF16)
    gk = gk_ref[...]
    kw = MLA_HEADS * QK_NOPE
    for pair in range(MLA_HEADS // 2):
        zk = jnp.dot(c, w_ref[:, pair * 256:(pair + 1) * 256], preferred_element_type=F32)
        zv = jnp.dot(c, w_ref[:, kw + pair * 256:kw + (pair + 1) * 256],
                     preferred_element_type=F32)
        for sub in range(2):
            hd = 2 * pair + sub
            zh = zk[:, sub * QK_NOPE:(sub + 1) * QK_NOPE]
            k_ref[hd, :, :QK_NOPE] = (zh * _rms_scale(zh, QK_NOPE) * gk).astype(BF16)
            k_ref[hd, :, QK_NOPE:] = kpe
            vh = zv[:, sub * V_HEAD:(sub + 1) * V_HEAD]
            v_ref[hd] = (vh.T if transpose_v else vh).astype(BF16)


def _kv_up(ckv, kpe, w_kv, gk, *, tm, transpose_v):
    n = ckv.shape[0]
    if transpose_v:
        v_spec = pl.BlockSpec((MLA_HEADS, V_HEAD, tm), lambda i: (0, 0, i))
        v_shape = (MLA_HEADS, V_HEAD, n)
    else:
        v_spec = pl.BlockSpec((MLA_HEADS, tm, V_HEAD), lambda i: (0, i, 0))
        v_shape = (MLA_HEADS, n, V_HEAD)
    return pl.pallas_call(
        functools.partial(_kvup_kernel, transpose_v=transpose_v),
        grid=(n // tm,),
        in_specs=[pl.BlockSpec((tm, ckv.shape[1]), lambda i: (i, 0)),
                  pl.BlockSpec((tm, QK_ROPE), lambda i: (i, 0)),
                  _const_spec(w_kv.shape), _const_spec(gk.shape)],
        out_specs=[pl.BlockSpec((MLA_HEADS, tm, QK_DIM), lambda i: (0, i, 0)), v_spec],
        out_shape=(jax.ShapeDtypeStruct((MLA_HEADS, n, QK_DIM), BF16),
                   jax.ShapeDtypeStruct(v_shape, BF16)),
        compiler_params=pltpu.CompilerParams(
            dimension_semantics=("arbitrary",), vmem_limit_bytes=_vmem_limit(40 << 20)),
        name="kv_up",
    )(ckv, kpe, w_kv, gk)


def _qk(q, k):
    return lax.dot_general(q, k, (((1,), (1,)), ((), ())), preferred_element_type=F32)


def _prompt_attn_kernel(q_ref, k_ref, vt_ref, o_ref, s_ref, p_ref, a_ref, m_ref, l_ref, acc_ref,
                        *, tq, tk):
    i = pl.program_id(2)

    def scores(b, slot):
        start = pl.multiple_of(b * tk, tk)
        s_ref[slot] = _qk(k_ref[0, pl.ds(start, tk), :], q_ref[0])

    def softmax(slot, diag_block=None):
        s = s_ref[slot]
        if diag_block is not None:
            kc = (lax.broadcasted_iota(jnp.int32, s.shape, 0) + diag_block * tk) // CHUNK
            qc = lax.broadcasted_iota(jnp.int32, s.shape, 1) // CHUNK
            s = jnp.where(kc <= qc, s, NEG_INF)
        m_prev = m_ref[...]
        m_new = jnp.maximum(m_prev, jnp.max(s, axis=0, keepdims=True))
        alpha = jnp.exp2(m_prev - m_new)
        p = jnp.exp2(s - m_new)
        l_ref[...] = alpha * l_ref[...] + jnp.sum(p, axis=0, keepdims=True)
        m_ref[...] = m_new
        a_ref[slot] = alpha
        p_ref[slot] = p.astype(BF16)

    def values(b, slot):
        start = pl.multiple_of(jnp.maximum(b, 0) * tk, tk)
        pv = jnp.dot(vt_ref[0, :, pl.ds(start, tk)], p_ref[slot], preferred_element_type=F32)
        acc_ref[...] = acc_ref[...] * a_ref[slot] + pv

    m_ref[...] = jnp.full(m_ref.shape, NEG_INF, F32)
    l_ref[...] = jnp.zeros(l_ref.shape, F32)
    acc_ref[...] = jnp.zeros(acc_ref.shape, F32)
    p_ref[1] = jnp.zeros(p_ref.shape[1:], BF16)
    a_ref[1] = jnp.ones(a_ref.shape[1:], F32)
    scores(0, 0)

    def pair(jj, carry):
        b = 2 * jj
        scores(b + 1, 1)
        values(b - 1, 1)
        softmax(0)
        scores(b + 2, 0)
        values(b, 0)
        softmax(1)
        return carry

    lax.fori_loop(0, i, pair, 0)
    b = 2 * i
    scores(b + 1, 1)
    values(b - 1, 1)
    softmax(0, diag_block=0)
    values(b, 0)
    softmax(1, diag_block=1)
    values(b + 1, 1)
    o = acc_ref[...] / l_ref[...]
    o_ref[...] = o.T.astype(o_ref.dtype)


def _prompt_attention(q, k, vt, *, batch, seq, tq):
    heads, n, _ = q.shape
    nq = seq // tq
    tk = tq // 2
    return pl.pallas_call(
        functools.partial(_prompt_attn_kernel, tq=tq, tk=tk),
        grid=(batch, heads, nq),
        in_specs=[pl.BlockSpec((1, tq, QK_DIM), lambda b, h, i: (h, b * nq + i, 0)),
                  pl.BlockSpec((1, seq, QK_DIM), lambda b, h, i: (h, b, 0)),
                  pl.BlockSpec((1, V_HEAD, seq), lambda b, h, i: (h, 0, b))],
        out_specs=pl.BlockSpec((tq, V_HEAD), lambda b, h, i: (b * nq + i, h)),
        out_shape=jax.ShapeDtypeStruct((n, heads * V_HEAD), BF16),
        scratch_shapes=[pltpu.VMEM((2, tk, tq), F32), pltpu.VMEM((2, tk, tq), BF16),
                        pltpu.VMEM((2, 1, tq), F32), pltpu.VMEM((1, tq), F32),
                        pltpu.VMEM((1, tq), F32), pltpu.VMEM((V_HEAD, tq), F32)],
        compiler_params=pltpu.CompilerParams(
            dimension_semantics=("arbitrary", "arbitrary", "arbitrary"),
            vmem_limit_bytes=_vmem_limit(40 << 20)),
        name="prompt_attention",
    )(q, k, vt)


def _sample_attn_kernel(q_ref, kc_ref, vc_ref, kn_ref, vn_ref, o_ref):
    q = q_ref[0]
    s_c = _qk(q, kc_ref[0])
    s_n = _qk(q, kn_ref[0])
    m = jnp.maximum(jnp.max(s_c, axis=-1, keepdims=True), jnp.max(s_n, axis=-1, keepdims=True))
    p_c = jnp.exp2(s_c - m)
    p_n = jnp.exp2(s_n - m)
    l = jnp.sum(p_c, axis=-1, keepdims=True) + jnp.sum(p_n, axis=-1, keepdims=True)
    acc = (jnp.dot(p_c.astype(BF16), vc_ref[0], preferred_element_type=F32)
           + jnp.dot(p_n.astype(BF16), vn_ref[0], preferred_element_type=F32))
    o_ref[...] = (acc / l).astype(o_ref.dtype)


def _sample_attention(q, kc, vc, kn, vn, *, batch, past, new):
    heads = q.shape[0]
    return pl.pallas_call(
        _sample_attn_kernel,
        grid=(batch, heads),
        in_specs=[pl.BlockSpec((1, new, QK_DIM), lambda b, h: (h, b, 0)),
                  pl.BlockSpec((1, past, QK_DIM), lambda b, h: (h, b, 0)),
                  pl.BlockSpec((1, past, V_HEAD), lambda b, h: (h, b, 0)),
                  pl.BlockSpec((1, new, QK_DIM), lambda b, h: (h, b, 0)),
                  pl.BlockSpec((1, new, V_HEAD), lambda b, h: (h, b, 0))],
        out_specs=pl.BlockSpec((new, V_HEAD), lambda b, h: (b, h)),
        out_shape=jax.ShapeDtypeStruct((batch * new, heads * V_HEAD), BF16),
        compiler_params=pltpu.CompilerParams(
            dimension_semantics=("arbitrary", "arbitrary")),
        name="sample_attention",
    )(q, kc, vc, kn, vn)


def _retention_kernel(q_ref, k_ref, v_ref, rg_ref, dec_ref, qd_ref, kd_ref, g_ref, s0_ref,
                      o_ref, sfin_ref, state_ref, *, chunk, n_chunks):
    c = pl.program_id(2)

    @pl.when(c == 0)
    def _():
        state_ref[...] = s0_ref[0, 0]

    q, k, v = q_ref[...], k_ref[...], v_ref[...]
    state = state_ref[...]
    qd = qd_ref[0]
    s = _qk(q, k) * dec_ref[0]
    inner = jnp.dot(s.astype(BF16), v, preferred_element_type=F32)
    cross = jnp.dot(q, state.astype(BF16), preferred_element_type=F32) * qd
    k_dec = (k.astype(F32) * kd_ref[0]).astype(BF16)
    kv = lax.dot_general(k_dec, v, (((0,), (0,)), ((), ())), preferred_element_type=F32)
    new_state = state * qd[chunk - 1:chunk, :] + kv
    state_ref[...] = new_state

    o = inner + cross
    mu = jnp.mean(o, axis=-1, keepdims=True)
    oc = o - mu
    var = jnp.mean(oc * oc, axis=-1, keepdims=True)
    y = oc * lax.rsqrt(var + EPS) * g_ref[0]
    rg = rg_ref[...]
    o_ref[...] = (rg * jax.nn.sigmoid(rg) * y).astype(o_ref.dtype)

    @pl.when(c == n_chunks - 1)
    def _():
        sfin_ref[0, 0] = new_state


def _retention_tables(chunk):
    lg = jnp.log1p(-(2.0 ** (-5.0 - jnp.arange(RET_HEADS, dtype=F32))))
    idx = jnp.arange(chunk, dtype=F32)
    diff = idx[:, None] - idx[None, :]
    decay = jnp.where(diff[None] >= 0,
                      jnp.exp(jnp.maximum(diff, 0.0)[None] * lg[:, None, None]), 0.0)
    q_dec = jnp.exp((idx + 1.0)[None, :] * lg[:, None])[..., None]
    k_dec = jnp.exp((chunk - 1.0 - idx)[None, :] * lg[:, None])[..., None]
    return decay, q_dec, k_dec


def _retention(rq, rk, rv, rg, g_ret, s0, *, batch, seq, chunk):
    n = rq.shape[0]
    nc = seq // chunk
    decay, q_dec, k_dec = _retention_tables(chunk)
    tok = lambda w: pl.BlockSpec((chunk, w), lambda b, h, c: (b * nc + c, h))
    per_head = lambda shape: pl.BlockSpec((1,) + shape, lambda b, h, c: (h, 0, 0))
    state_spec = pl.BlockSpec((1, 1, RET_DK, RET_DV), lambda b, h, c: (b, h, 0, 0))
    return pl.pallas_call(
        functools.partial(_retention_kernel, chunk=chunk, n_chunks=nc),
        grid=(batch, RET_HEADS, nc),
        in_specs=[tok(RET_DK), tok(RET_DK), tok(RET_DV), tok(RET_DV),
                  per_head((chunk, chunk)), per_head((chunk, 1)), per_head((chunk, 1)),
                  per_head((1, RET_DV)), state_spec],
        out_specs=[tok(RET_DV), state_spec],
        out_shape=(jax.ShapeDtypeStruct((n, RET_HEADS * RET_DV), BF16),
                   jax.ShapeDtypeStruct((batch, RET_HEADS, RET_DK, RET_DV), F32)),
        scratch_shapes=[pltpu.VMEM((RET_DK, RET_DV), F32)],
        compiler_params=pltpu.CompilerParams(
            dimension_semantics=("arbitrary", "arbitrary", "arbitrary")),
        name="retention",
    )(rq, rk, rv, rg, decay, q_dec, k_dec, g_ret.reshape(RET_HEADS, 1, RET_DV), s0)


def _merge_kernel(x_ref, a_ref, r_ref, ga_ref, gb_ref, wa_ref, wb_ref, wo_ref, o_ref):
    a_d = jnp.dot(a_ref[...], wa_ref[...], preferred_element_type=F32)
    r_d = jnp.dot(r_ref[...], wb_ref[...], preferred_element_type=F32)
    merged = jax.nn.sigmoid(ga_ref[...]) * a_d + jax.nn.sigmoid(gb_ref[...]) * r_d
    o_ref[...] = x_ref[...] + jnp.dot(merged.astype(BF16), wo_ref[...],
                                      preferred_element_type=F32)


def _merge(x, attn, ret, ga, gb, wa, wb, wo, *, tm):
    n, d = x.shape
    row = pl.BlockSpec((tm, d), lambda i: (i, 0))
    return pl.pallas_call(
        _merge_kernel,
        grid=(n // tm,),
        in_specs=[row, row, row, row, row,
                  _const_spec(wa.shape), _const_spec(wb.shape), _const_spec(wo.shape)],
        out_specs=row,
        out_shape=jax.ShapeDtypeStruct((n, d), F32),
        compiler_params=pltpu.CompilerParams(
            dimension_semantics=("arbitrary",), vmem_limit_bytes=_vmem_limit(40 << 20)),
        name="merge",
    )(x, attn, ret, ga, gb, wa, wb, wo)


_PAD_ROWS = 8


def _ffn_kernel(x_ref, g_ref, wup_ref, cw_ref, cb_ref, wdn_ref, prev_ref,
                y_ref, st_ref, ubuf_ref, *, tm, d_ff, n_tiles):
    j = pl.program_id(1)
    lo = _PAD_ROWS - (CONV_W - 1)

    @pl.when(j == 0)
    def _():
        ubuf_ref[0:_PAD_ROWS, :] = jnp.zeros((_PAD_ROWS, ubuf_ref.shape[1]), F32)
        ubuf_ref[lo:_PAD_ROWS, :] = prev_ref[0]

    @pl.when(j > 0)
    def _():
        ubuf_ref[0:_PAD_ROWS, :] = ubuf_ref[tm:tm + _PAD_ROWS, :]

    xf = x_ref[...]
    hn = (xf * _rms_scale(xf, xf.shape[-1]) * g_ref[...]).astype(BF16)
    ubuf_ref[_PAD_ROWS:_PAD_ROWS + tm, :] = jnp.dot(hn, wup_ref[...],
                                                    preferred_element_type=F32)
    c = cb_ref[...]
    for tap in range(CONV_W):
        c = c + ubuf_ref[lo + tap:lo + tap + tm, :] * cw_ref[tap:tap + 1, :]
    gate, val = c[:, :d_ff], c[:, d_ff:]
    act = (gate * jax.nn.sigmoid(gate) * val).astype(BF16)
    y_ref[...] = xf + jnp.dot(act, wdn_ref[...], preferred_element_type=F32)

    @pl.when(j == n_tiles - 1)
    def _():
        st_ref[0] = ubuf_ref[tm + lo:tm + _PAD_ROWS, :]


def _conv_ffn(x, g, w_up, cw, cb, w_dn, prev, *, batch, seq, tm):
    n, d = x.shape
    d_ff = w_dn.shape[0]
    nt = seq // tm
    row = pl.BlockSpec((tm, d), lambda b, j: (b * nt + j, 0))
    st = pl.BlockSpec((1, CONV_W - 1, 2 * d_ff), lambda b, j: (b, 0, 0))
    return pl.pallas_call(
        functools.partial(_ffn_kernel, tm=tm, d_ff=d_ff, n_tiles=nt),
        grid=(batch, nt),
        in_specs=[row, _const_spec(g.shape), _const_spec(w_up.shape), _const_spec(cw.shape),
                  _const_spec(cb.shape), _const_spec(w_dn.shape), st],
        out_specs=[row, st],
        out_shape=(jax.ShapeDtypeStruct((n, d), F32),
                   jax.ShapeDtypeStruct((batch, CONV_W - 1, 2 * d_ff), F32)),
        scratch_shapes=[pltpu.VMEM((tm + _PAD_ROWS, 2 * d_ff), F32)],
        compiler_params=pltpu.CompilerParams(
            dimension_semantics=("arbitrary", "arbitrary"),
            vmem_limit_bytes=_vmem_limit(56 << 20)),
        name="conv_ffn",
    )(x, g, w_up, cw, cb, w_dn, prev)


def _pack_weights(w_in, w_q_up, w_kv_up, w_o_branch, w_out, w_ffn_up, w_ffn_down):
    sizes = (768, 512, 64, 512, 512, 1024, 1024, 1024, 1024)
    offs = np.concatenate([[0], np.cumsum(sizes)])
    parts = []
    for (name, width), a, b in zip(_IN_GROUPS, offs[:-1], offs[1:]):
        part = w_in[:, a:b]
        if b - a < width:
            part = jnp.pad(part, ((0, 0), (0, width - (b - a))))
        parts.append(part)
    w_in_p = jnp.concatenate(parts, axis=1).astype(BF16)
    wq = w_q_up.reshape(w_q_up.shape[0], MLA_HEADS, QK_DIM)
    w_q_p = jnp.concatenate([wq[:, :, :QK_NOPE].reshape(wq.shape[0], -1),
                             wq[:, :, QK_NOPE:].reshape(wq.shape[0], -1)], axis=1).astype(BF16)
    wkv = w_kv_up.reshape(w_kv_up.shape[0], MLA_HEADS, QK_NOPE + V_HEAD)
    w_kv_p = jnp.concatenate([wkv[:, :, :QK_NOPE].reshape(wkv.shape[0], -1),
                              wkv[:, :, QK_NOPE:].reshape(wkv.shape[0], -1)], axis=1).astype(BF16)
    mla_width = MLA_HEADS * V_HEAD
    return (w_in_p, w_q_p, w_kv_p, w_o_branch[:mla_width].astype(BF16),
            w_o_branch[mla_width:].astype(BF16), w_out.astype(BF16),
            w_ffn_up.astype(BF16), w_ffn_down.astype(BF16))


def _layer(x, pos, past, w, *, batch, seq, tm, tq, ret_chunk, ffn_tm):
    n = x.shape[0]
    tabs64 = _rope_tables(pos, RET_DK // 2)
    tabs32 = _rope_tables(pos, QK_ROPE // 2)
    seq_tiles = max(seq // tm, 1)
    if seq < tm:
        tabs64 = tuple(jnp.tile(t, (tm // seq, 1)) for t in tabs64)
        tabs32 = tuple(jnp.tile(t, (tm // seq, 1)) for t in tabs32)
    qlat, ckv, kpe, rq, rk, rv, rg, ga, gb = _in_proj(
        x, w['w_in'], w['g_norm_mix'], w['g_q_lat'], w['g_kv_lat'], w['g_k_rope'],
        tabs64, tabs32, seq_tiles=seq_tiles, tm=tm)
    q = _q_up(qlat, w['w_q_up'], w['g_q_nope'], w['g_q_rope'], tabs32,
              seq_tiles=seq_tiles, tm=tm)
    k_new, v_new = _kv_up(ckv, kpe, w['w_kv_up'], w['g_k_nope'], tm=tm,
                          transpose_v=past is None)
    if past is None:
        attn = _prompt_attention(q, k_new, v_new, batch=batch, seq=seq, tq=tq)
        s0 = jnp.zeros((batch, RET_HEADS, RET_DK, RET_DV), F32)
        prev = jnp.zeros((batch, CONV_W - 1, w['w_ffn_up'].shape[1]), F32)
    else:
        lat_c, pe_c, s0, prev = past
        past_len = lat_c.shape[1]
        k_c, v_c = _kv_up(lat_c.reshape(batch * past_len, -1), pe_c.reshape(batch * past_len, -1),
                          w['w_kv_up'], w['g_k_nope'], tm=512, transpose_v=False)
        attn = _sample_attention(q, k_c, v_c, k_new, v_new, batch=batch, past=past_len, new=seq)
    ret, s_fin = _retention(rq, rk, rv, rg, w['g_ret_out'], s0,
                            batch=batch, seq=seq, chunk=ret_chunk)
    x1 = _merge(x, attn, ret, ga, gb, w['w_o_a'], w['w_o_b'], w['w_out'], tm=tm)
    y, conv_state = _conv_ffn(x1, w['g_norm_ffn'], w['w_ffn_up'], w['ffn_conv_w'],
                              w['ffn_conv_b'], w['w_ffn_down'], prev,
                              batch=batch, seq=seq, tm=ffn_tm)
    return y, (ckv, kpe, s_fin, conv_state)


def kernel(x_prompt, x_sample, cache_mla_latent, cache_mla_rope_key, state_retention, state_ffn_conv, g_norm_mix, w_in, g_q_lat, w_q_up, g_q_nope, g_q_rope, g_kv_lat, w_kv_up, g_k_nope, g_k_rope, g_ret_out, w_o_branch, w_out, g_norm_ffn, w_ffn_up, ffn_conv_w, ffn_conv_b, w_ffn_down):
    depth = w_in.shape[0]
    assert depth == 1, "single-layer trunk"
    bp, tp, d = x_prompt.shape
    bs, ts, _ = x_sample.shape
    past_len = cache_mla_latent.shape[2]
    (w_in_p, w_q_p, w_kv_p, w_o_a, w_o_b, w_out_p, w_up_p, w_dn_p) = _pack_weights(
        w_in[0], w_q_up[0], w_kv_up[0], w_o_branch[0], w_out[0], w_ffn_up[0], w_ffn_down[0])
    row = lambda g: g.reshape(1, -1).astype(F32)
    w = dict(
        w_in=w_in_p, w_q_up=w_q_p, w_kv_up=w_kv_p, w_o_a=w_o_a, w_o_b=w_o_b, w_out=w_out_p,
        w_ffn_up=w_up_p, w_ffn_down=w_dn_p,
        g_norm_mix=row(g_norm_mix[0]), g_q_lat=row(g_q_lat[0]), g_kv_lat=row(g_kv_lat[0]),
        g_k_rope=jnp.pad(row(g_k_rope[0]), ((0, 0), (0, LANES - QK_ROPE))),
        g_q_nope=row(g_q_nope[0]), g_q_rope=jnp.tile(row(g_q_rope[0]), (1, LANES // QK_ROPE)),
        g_k_nope=row(g_k_nope[0]), g_ret_out=g_ret_out[0], g_norm_ffn=row(g_norm_ffn[0]),
        ffn_conv_w=ffn_conv_w[0], ffn_conv_b=row(ffn_conv_b[0]))

    yp, sp = _layer(x_prompt.reshape(bp * tp, d), jnp.arange(tp), None, w,
                    batch=bp, seq=tp, tm=512, tq=1024, ret_chunk=256, ffn_tm=256)
    past = (cache_mla_latent[0], cache_mla_rope_key[0], state_retention[0], state_ffn_conv[0])
    ys, ss = _layer(x_sample.reshape(bs * ts, d), past_len + jnp.arange(ts), past, w,
                    batch=bs, seq=ts, tm=bs * ts, tq=None, ret_chunk=ts, ffn_tm=ts)

    def states(st, b, t):
        ckv, kpe, s_fin, conv = st
        return (ckv.reshape(1, b, t, -1), kpe.reshape(1, b, t, -1), s_fin[None], conv[None])

    return (yp.reshape(bp, tp, d), ys.reshape(bs, ts, d)) + states(sp, bp, tp) + states(ss, bs, ts)
```

```python
import functools
import math

import jax
import jax.numpy as jnp
import numpy as np
from jax import lax
from jax.experimental import pallas as pl
from jax.experimental.pallas import tpu as pltpu

F32 = jnp.float32
BF16 = jnp.bfloat16

EPS = 1e-6
NEG_INF = -1e30
ROPE_THETA = 10000.0
CHUNK = 64
LANES = 128
V7X_VMEM_BYTES = 64 * 1024 * 1024

MLA_HEADS = 8
QK_NOPE = 128
QK_ROPE = 64
QK_DIM = QK_NOPE + QK_ROPE
V_HEAD = 128
MLA_SCALE = QK_DIM ** -0.5
Q_PRESCALE = MLA_SCALE * math.log2(math.e)
RET_HEADS = 4
RET_DK = 128
RET_DV = 256
CONV_W = 3


def _vmem_limit(nbytes):
    return int(min(V7X_VMEM_BYTES - (4 << 20), max(32 << 20, nbytes)))


def _const_spec(shape):
    nd = len(shape)
    return pl.BlockSpec(shape, lambda *_: (0,) * nd, pipeline_mode=pl.Buffered(1))


def _rms_scale(z, width):
    return lax.rsqrt(jnp.sum(z * z, axis=-1, keepdims=True) * (1.0 / width) + EPS)


def _swap_halves_64(z):
    lane = lax.broadcasted_iota(jnp.int32, z.shape, 1)
    return jnp.where((lane % 64) < 32, pltpu.roll(z, 96, 1), pltpu.roll(z, 32, 1))


def _rope_tables(pos, half):
    freqs = ROPE_THETA ** (-jnp.arange(half, dtype=F32) / half)
    ang = pos.astype(F32)[:, None] * freqs[None, :]
    cos, sin = jnp.cos(ang), jnp.sin(ang)
    reps = LANES // (2 * half)
    return (jnp.tile(jnp.concatenate([cos, cos], -1), (1, reps)),
            jnp.tile(jnp.concatenate([-sin, sin], -1), (1, reps)))


_IN_GROUPS = (('qlat', 768), ('kvl', 512), ('kpe', 128), ('rq', 512), ('rk', 512),
              ('rv', 1024), ('rg', 1024), ('ga', 1024), ('gb', 1024))
_IN_OFFS = {}
_o = 0
for _n, _w in _IN_GROUPS:
    _IN_OFFS[_n] = (_o, _o + _w)
    _o += _w
IN_PACKED = _o


def _inproj_kernel(x_ref, gmix_ref, w_ref, gq_ref, gkv_ref, gkpe_ref,
                   c64_ref, s64_ref, c32_ref, s32_ref,
                   qlat_ref, ckv_ref, kpe_ref, rq_ref, rk_ref, rv_ref, rg_ref, ga_ref, gb_ref):
    xf = x_ref[...]
    h = (xf * _rms_scale(xf, xf.shape[-1]) * gmix_ref[...]).astype(BF16)

    def proj(name):
        a, b = _IN_OFFS[name]
        return jnp.dot(h, w_ref[:, a:b], preferred_element_type=F32)

    z = proj('qlat')
    qlat_ref[...] = (z * _rms_scale(z, z.shape[-1]) * gq_ref[...]).astype(BF16)
    z = proj('kvl')
    ckv_ref[...] = z * _rms_scale(z, z.shape[-1]) * gkv_ref[...]
    z = proj('kpe')
    zn = z * _rms_scale(z, QK_ROPE) * gkpe_ref[...]
    kpe = zn * c32_ref[...] + _swap_halves_64(zn) * s32_ref[...]
    kpe_ref[...] = kpe[:, :QK_ROPE]
    c64, s64 = c64_ref[...], s64_ref[...]
    for name, ref, scale in (('rq', rq_ref, 1.0), ('rk', rk_ref, RET_DK ** -0.5)):
        z = proj(name)
        for hd in range(RET_HEADS):
            zh = z[:, hd * RET_DK:(hd + 1) * RET_DK]
            r = zh * c64 + pltpu.roll(zh, RET_DK // 2, 1) * s64
            if scale != 1.0:
                r = r * scale
            ref[:, hd * RET_DK:(hd + 1) * RET_DK] = r.astype(BF16)
    rv_ref[...] = proj('rv').astype(BF16)
    rg_ref[...] = proj('rg')
    ga_ref[...] = proj('ga')
    gb_ref[...] = proj('gb')


def _in_proj(x, w_in_packed, gmix, gq, gkv, gkpe, tabs64, tabs32, *, seq_tiles, tm):
    n, d = x.shape
    row = lambda w: pl.BlockSpec((tm, w), lambda i: (i, 0))
    tab = pl.BlockSpec((tm, LANES), lambda i: (i % seq_tiles, 0))
    out_shapes = (
        jax.ShapeDtypeStruct((n, 768), BF16), jax.ShapeDtypeStruct((n, 512), F32),
        jax.ShapeDtypeStruct((n, QK_ROPE), F32), jax.ShapeDtypeStruct((n, 512), BF16),
        jax.ShapeDtypeStruct((n, 512), BF16), jax.ShapeDtypeStruct((n, 1024), BF16),
        jax.ShapeDtypeStruct((n, 1024), F32), jax.ShapeDtypeStruct((n, 1024), F32),
        jax.ShapeDtypeStruct((n, 1024), F32))
    return pl.pallas_call(
        _inproj_kernel,
        grid=(n // tm,),
        in_specs=[row(d), _const_spec(gmix.shape), _const_spec(w_in_packed.shape),
                  _const_spec(gq.shape), _const_spec(gkv.shape), _const_spec(gkpe.shape),
                  tab, tab, tab, tab],
        out_specs=[row(s.shape[1]) for s in out_shapes],
        out_shape=out_shapes,
        compiler_params=pltpu.CompilerParams(
            dimension_semantics=("arbitrary",), vmem_limit_bytes=_vmem_limit(56 << 20)),
        name="in_proj",
    )(x, gmix, w_in_packed, gq, gkv, gkpe, *tabs64, *tabs32)


def _qup_kernel(ql_ref, w_ref, gn_ref, gr_ref, c32_ref, s32_ref, q_ref):
    ql = ql_ref[...]
    zn = jnp.dot(ql, w_ref[:, :MLA_HEADS * QK_NOPE], preferred_element_type=F32)
    zr = jnp.dot(ql, w_ref[:, MLA_HEADS * QK_NOPE:], preferred_element_type=F32)
    gn = gn_ref[...]
    for hd in range(MLA_HEADS):
        zh = zn[:, hd * QK_NOPE:(hd + 1) * QK_NOPE]
        q_ref[hd, :, :QK_NOPE] = (zh * (_rms_scale(zh, QK_NOPE) * Q_PRESCALE) * gn).astype(BF16)
    c32, s32, gr = c32_ref[...], s32_ref[...], gr_ref[...]
    lane = lax.broadcasted_iota(jnp.int32, c32.shape, 1)
    for pair in range(MLA_HEADS // 2):
        zp = zr[:, pair * LANES:(pair + 1) * LANES]
        sq = zp * zp
        lo = jnp.sum(jnp.where(lane < QK_ROPE, sq, 0.0), axis=-1, keepdims=True)
        hi = jnp.sum(sq, axis=-1, keepdims=True) - lo
        inv = jnp.where(lane < QK_ROPE, lax.rsqrt(lo * (1.0 / QK_ROPE) + EPS),
                        lax.rsqrt(hi * (1.0 / QK_ROPE) + EPS))
        z = zp * (inv * Q_PRESCALE) * gr
        r = (z * c32 + _swap_halves_64(z) * s32).astype(BF16)
        q_ref[2 * pair, :, QK_NOPE:] = r[:, :QK_ROPE]
        q_ref[2 * pair + 1, :, QK_NOPE:] = r[:, QK_ROPE:]


def _q_up(qlat, w_q, gn, gr, tabs32, *, seq_tiles, tm):
    n = qlat.shape[0]
    tab = pl.BlockSpec((tm, LANES), lambda i: (i % seq_tiles, 0))
    return pl.pallas_call(
        _qup_kernel,
        grid=(n // tm,),
        in_specs=[pl.BlockSpec((tm, qlat.shape[1]), lambda i: (i, 0)), _const_spec(w_q.shape),
                  _const_spec(gn.shape), _const_spec(gr.shape), tab, tab],
        out_specs=pl.BlockSpec((MLA_HEADS, tm, QK_DIM), lambda i: (0, i, 0)),
        out_shape=jax.ShapeDtypeStruct((MLA_HEADS, n, QK_DIM), BF16),
        compiler_params=pltpu.CompilerParams(
            dimension_semantics=("arbitrary",), vmem_limit_bytes=_vmem_limit(40 << 20)),
        name="q_up",
    )(qlat, w_q, gn, gr, *tabs32)


def _kvup_kernel(c_ref, kpe_ref, w_ref, gk_ref, k_ref, v_ref, *, transpose_v):
    c = c_ref[...].astype(BF16)
    kpe = kpe_ref[...].astype(BF16)
    gk = gk_ref[...]
    kw = MLA_HEADS * QK_NOPE
    for pair in range(MLA_HEADS // 2):
        zk = jnp.dot(c, w_ref[:, pair * 256:(pair + 1) * 256], preferred_element_type=F32)
        zv = jnp.dot(c, w_ref[:, kw + pair * 256:kw + (pair + 1) * 256],
                     preferred_element_type=F32)
        for sub in range(2):
            hd = 2 * pair + sub
            zh = zk[:, sub * QK_NOPE:(sub + 1) * QK_NOPE]
            k_ref[hd, :, :QK_NOPE] = (zh * _rms_scale(zh, QK_NOPE) * gk).astype(BF16)
            k_ref[hd, :, QK_NOPE:] = kpe
            vh = zv[:, sub * V_HEAD:(sub + 1) * V_HEAD]
            if transpose_v:
                v_ref[hd, :V_HEAD, :] = vh.T.astype(BF16)
                v_ref[hd, V_HEAD:, :] = jnp.ones((_ONES_ROWS, vh.shape[0]), BF16)
            else:
                v_ref[hd] = vh.astype(BF16)


_ONES_ROWS = 16


def _kv_up(ckv, kpe, w_kv, gk, *, tm, transpose_v):
    n = ckv.shape[0]
    if transpose_v:
        v_spec = pl.BlockSpec((MLA_HEADS, V_HEAD + _ONES_ROWS, tm), lambda i: (0, 0, i))
        v_shape = (MLA_HEADS, V_HEAD + _ONES_ROWS, n)
    else:
        v_spec = pl.BlockSpec((MLA_HEADS, tm, V_HEAD), lambda i: (0, i, 0))
        v_shape = (MLA_HEADS, n, V_HEAD)
    return pl.pallas_call(
        functools.partial(_kvup_kernel, transpose_v=transpose_v),
        grid=(n // tm,),
        in_specs=[pl.BlockSpec((tm, ckv.shape[1]), lambda i: (i, 0)),
                  pl.BlockSpec((tm, QK_ROPE), lambda i: (i, 0)),
                  _const_spec(w_kv.shape), _const_spec(gk.shape)],
        out_specs=[pl.BlockSpec((MLA_HEADS, tm, QK_DIM), lambda i: (0, i, 0)), v_spec],
        out_shape=(jax.ShapeDtypeStruct((MLA_HEADS, n, QK_DIM), BF16),
                   jax.ShapeDtypeStruct(v_shape, BF16)),
        compiler_params=pltpu.CompilerParams(
            dimension_semantics=("arbitrary",), vmem_limit_bytes=_vmem_limit(40 << 20)),
        name="kv_up",
    )(ckv, kpe, w_kv, gk)


def _qk(q, k):
    return lax.dot_general(q, k, (((1,), (1,)), ((), ())), preferred_element_type=F32)


_STRIP = 256


def _prompt_attn_kernel(q_ref, k_ref, vt_ref, o_ref, s_ref, smax_ref, p_ref, a_ref, m_ref,
                        acc_ref, *, tq, tk):
    i = pl.program_id(2)
    n_strips = tq // _STRIP
    all_strips = tuple(range(n_strips))
    cols = lambda c: slice(c * _STRIP, (c + 1) * _STRIP)

    def scores(b, slot, c):
        start = pl.multiple_of(b * tk, tk)
        s = _qk(k_ref[0, pl.ds(start, tk), :], q_ref[0, cols(c), :])
        s_ref[slot, :, cols(c)] = s
        smax_ref[slot, :, cols(c)] = jnp.max(s, axis=0, keepdims=True)

    def softmax(slot, c, diag_block=None):
        s = s_ref[slot, :, cols(c)]
        if diag_block is None:
            s_max = smax_ref[slot, :, cols(c)]
        else:
            kc = (lax.broadcasted_iota(jnp.int32, s.shape, 0) + diag_block * tk) // CHUNK
            qc = (lax.broadcasted_iota(jnp.int32, s.shape, 1) + c * _STRIP) // CHUNK
            s = jnp.where(kc <= qc, s, NEG_INF)
            s_max = jnp.max(s, axis=0, keepdims=True)
        m_prev = m_ref[:, cols(c)]
        m_new = jnp.maximum(m_prev, s_max)
        alpha = jnp.exp2(m_prev - m_new)
        p = jnp.exp2(s - m_new)
        m_ref[:, cols(c)] = m_new
        a_ref[slot, :, cols(c)] = alpha
        p_ref[slot, :, cols(c)] = p.astype(BF16)

    def values(b, slot, c):
        start = pl.multiple_of(jnp.maximum(b, 0) * tk, tk)
        pv = jnp.dot(vt_ref[0, :, pl.ds(start, tk)], p_ref[slot, :, cols(c)],
                     preferred_element_type=F32)
        acc_ref[:, cols(c)] = acc_ref[:, cols(c)] * a_ref[slot, :, cols(c)] + pv

    m_ref[...] = jnp.full(m_ref.shape, NEG_INF, F32)
    acc_ref[...] = jnp.zeros(acc_ref.shape, F32)
    p_ref[1] = jnp.zeros(p_ref.shape[1:], BF16)
    a_ref[1] = jnp.ones(a_ref.shape[1:], F32)
    for c in all_strips:
        scores(0, 0, c)

    def pair(jj, carry):
        b = 2 * jj
        for c in all_strips:
            scores(b + 1, 1, c)
        for c in all_strips:
            values(b - 1, 1, c)
        for c in all_strips:
            softmax(0, c)
        for c in all_strips:
            scores(b + 2, 0, c)
        for c in all_strips:
            values(b, 0, c)
        for c in all_strips:
            softmax(1, c)
        return carry

    lax.fori_loop(0, i, pair, 0)
    b = 2 * i
    late_strips = all_strips[n_strips // 2:]
    for c in all_strips:
        if c in late_strips:
            scores(b + 1, 1, c)
        softmax(0, c, diag_block=0)
        values(b - 1, 1, c)
    for c in all_strips:
        if c in late_strips:
            softmax(1, c, diag_block=1)
        values(b, 0, c)
    for c in late_strips:
        values(b + 1, 1, c)
    o = acc_ref[:V_HEAD, :] / acc_ref[V_HEAD:V_HEAD + 1, :]
    o_ref[...] = o.T.astype(o_ref.dtype)


def _prompt_attention(q, k, vt, *, batch, seq, tq):
    heads, n, _ = q.shape
    nq = seq // tq
    tk = tq // 2
    return pl.pallas_call(
        functools.partial(_prompt_attn_kernel, tq=tq, tk=tk),
        grid=(batch, heads, nq),
        in_specs=[pl.BlockSpec((1, tq, QK_DIM), lambda b, h, i: (h, b * nq + i, 0)),
                  pl.BlockSpec((1, seq, QK_DIM), lambda b, h, i: (h, b, 0)),
                  pl.BlockSpec((1, vt.shape[1], seq), lambda b, h, i: (h, 0, b))],
        out_specs=pl.BlockSpec((tq, V_HEAD), lambda b, h, i: (b * nq + i, h)),
        out_shape=jax.ShapeDtypeStruct((n, heads * V_HEAD), BF16),
        scratch_shapes=[pltpu.VMEM((2, tk, tq), F32), pltpu.VMEM((2, 1, tq), F32),
                        pltpu.VMEM((2, tk, tq), BF16),
                        pltpu.VMEM((2, 1, tq), F32), pltpu.VMEM((1, tq), F32),
                        pltpu.VMEM((vt.shape[1], tq), F32)],
        compiler_params=pltpu.CompilerParams(
            dimension_semantics=("arbitrary", "arbitrary", "arbitrary"),
            vmem_limit_bytes=_vmem_limit(40 << 20)),
        name="prompt_attention",
    )(q, k, vt)


def _sample_attn_kernel(q_ref, kc_ref, vc_ref, kn_ref, vn_ref, o_ref):
    q = q_ref[0]
    s_c = _qk(q, kc_ref[0])
    s_n = _qk(q, kn_ref[0])
    m = jnp.maximum(jnp.max(s_c, axis=-1, keepdims=True), jnp.max(s_n, axis=-1, keepdims=True))
    p_c = jnp.exp2(s_c - m)
    p_n = jnp.exp2(s_n - m)
    l = jnp.sum(p_c, axis=-1, keepdims=True) + jnp.sum(p_n, axis=-1, keepdims=True)
    acc = (jnp.dot(p_c.astype(BF16), vc_ref[0], preferred_element_type=F32)
           + jnp.dot(p_n.astype(BF16), vn_ref[0], preferred_element_type=F32))
    o_ref[...] = (acc / l).astype(o_ref.dtype)


def _sample_attention(q, kc, vc, kn, vn, *, batch, past, new):
    heads = q.shape[0]
    return pl.pallas_call(
        _sample_attn_kernel,
        grid=(batch, heads),
        in_specs=[pl.BlockSpec((1, new, QK_DIM), lambda b, h: (h, b, 0)),
                  pl.BlockSpec((1, past, QK_DIM), lambda b, h: (h, b, 0)),
                  pl.BlockSpec((1, past, V_HEAD), lambda b, h: (h, b, 0)),
                  pl.BlockSpec((1, new, QK_DIM), lambda b, h: (h, b, 0)),
                  pl.BlockSpec((1, new, V_HEAD), lambda b, h: (h, b, 0))],
        out_specs=pl.BlockSpec((new, V_HEAD), lambda b, h: (b, h)),
        out_shape=jax.ShapeDtypeStruct((batch * new, heads * V_HEAD), BF16),
        compiler_params=pltpu.CompilerParams(
            dimension_semantics=("arbitrary", "arbitrary")),
        name="sample_attention",
    )(q, kc, vc, kn, vn)


def _retention_kernel(q_ref, k_ref, v_ref, rg_ref, dec_ref, qd_ref, kd_ref, g_ref, s0_ref,
                      o_ref, sfin_ref, state_ref, *, chunk, n_chunks):
    c = pl.program_id(1)

    @pl.when(c == 0)
    def _():
        state_ref[...] = s0_ref[0]

    heads = range(RET_HEADS)
    kcols = [slice(hd * RET_DK, (hd + 1) * RET_DK) for hd in heads]
    vcols = [slice(hd * RET_DV, (hd + 1) * RET_DV) for hd in heads]
    q = [q_ref[:, kcols[hd]] for hd in heads]
    k = [k_ref[:, kcols[hd]] for hd in heads]
    v = [v_ref[:, vcols[hd]] for hd in heads]
    state = [state_ref[hd] for hd in heads]
    qd = [qd_ref[hd] for hd in heads]
    s = [_qk(q[hd], k[hd]) for hd in heads]
    cross = [jnp.dot(q[hd], state[hd].astype(BF16), preferred_element_type=F32) for hd in heads]
    s = [(s[hd] * dec_ref[hd]).astype(BF16) for hd in heads]
    k_dec = [(k[hd].astype(F32) * kd_ref[hd]).astype(BF16) for hd in heads]
    inner = [jnp.dot(s[hd], v[hd], preferred_element_type=F32) for hd in heads]
    kv = [lax.dot_general(k_dec[hd], v[hd], (((0,), (0,)), ((), ())),
                          preferred_element_type=F32) for hd in heads]
    for hd in heads:
        state_ref[hd] = state[hd] * qd[hd][chunk - 1:chunk, :] + kv[hd]
    for hd in heads:
        o = inner[hd] + cross[hd] * qd[hd]
        mu = jnp.mean(o, axis=-1, keepdims=True)
        oc = o - mu
        var = jnp.mean(oc * oc, axis=-1, keepdims=True)
        y = oc * lax.rsqrt(var + EPS) * g_ref[hd]
        rg = rg_ref[:, vcols[hd]]
        o_ref[:, vcols[hd]] = (rg * jax.nn.sigmoid(rg) * y).astype(o_ref.dtype)

    @pl.when(c == n_chunks - 1)
    def _():
        sfin_ref[0] = state_ref[...]


def _retention_tables(chunk):
    lg = jnp.log1p(-(2.0 ** (-5.0 - jnp.arange(RET_HEADS, dtype=F32))))
    idx = jnp.arange(chunk, dtype=F32)
    diff = idx[:, None] - idx[None, :]
    decay = jnp.where(diff[None] >= 0,
                      jnp.exp(jnp.maximum(diff, 0.0)[None] * lg[:, None, None]), 0.0)
    q_dec = jnp.exp((idx + 1.0)[None, :] * lg[:, None])[..., None]
    k_dec = jnp.exp((chunk - 1.0 - idx)[None, :] * lg[:, None])[..., None]
    return decay, q_dec, k_dec


def _retention(rq, rk, rv, rg, g_ret, s0, *, batch, seq, chunk):
    n = rq.shape[0]
    nc = seq // chunk
    decay, q_dec, k_dec = _retention_tables(chunk)
    tok = lambda w: pl.BlockSpec((chunk, w), lambda b, c: (b * nc + c, 0))
    state_spec = pl.BlockSpec((1, RET_HEADS, RET_DK, RET_DV), lambda b, c: (b, 0, 0, 0))
    g_ret = g_ret.reshape(RET_HEADS, 1, RET_DV)
    return pl.pallas_call(
        functools.partial(_retention_kernel, chunk=chunk, n_chunks=nc),
        grid=(batch, nc),
        in_specs=[tok(RET_HEADS * RET_DK), tok(RET_HEADS * RET_DK), tok(RET_HEADS * RET_DV),
                  tok(RET_HEADS * RET_DV), _const_spec(decay.shape), _const_spec(q_dec.shape),
                  _const_spec(k_dec.shape), _const_spec(g_ret.shape), state_spec],
        out_specs=[tok(RET_HEADS * RET_DV), state_spec],
        out_shape=(jax.ShapeDtypeStruct((n, RET_HEADS * RET_DV), BF16),
                   jax.ShapeDtypeStruct((batch, RET_HEADS, RET_DK, RET_DV), F32)),
        scratch_shapes=[pltpu.VMEM((RET_HEADS, RET_DK, RET_DV), F32)],
        compiler_params=pltpu.CompilerParams(dimension_semantics=("arbitrary", "arbitrary")),
        name="retention",
    )(rq, rk, rv, rg, decay, q_dec, k_dec, g_ret, s0)


def _merge_kernel(x_ref, a_ref, r_ref, ga_ref, gb_ref, wa_ref, wb_ref, wo_ref, o_ref):
    a_d = jnp.dot(a_ref[...], wa_ref[...], preferred_element_type=F32)
    r_d = jnp.dot(r_ref[...], wb_ref[...], preferred_element_type=F32)
    merged = jax.nn.sigmoid(ga_ref[...]) * a_d + jax.nn.sigmoid(gb_ref[...]) * r_d
    o_ref[...] = x_ref[...] + jnp.dot(merged.astype(BF16), wo_ref[...],
                                      preferred_element_type=F32)


def _merge(x, attn, ret, ga, gb, wa, wb, wo, *, tm):
    n, d = x.shape
    row = pl.BlockSpec((tm, d), lambda i: (i, 0))
    return pl.pallas_call(
        _merge_kernel,
        grid=(n // tm,),
        in_specs=[row, row, row, row, row,
                  _const_spec(wa.shape), _const_spec(wb.shape), _const_spec(wo.shape)],
        out_specs=row,
        out_shape=jax.ShapeDtypeStruct((n, d), F32),
        compiler_params=pltpu.CompilerParams(
            dimension_semantics=("arbitrary",), vmem_limit_bytes=_vmem_limit(40 << 20)),
        name="merge",
    )(x, attn, ret, ga, gb, wa, wb, wo)


_PAD_ROWS = 8


def _ffn_kernel(x_ref, g_ref, wup_ref, cw_ref, cb_ref, wdn_ref, prev_ref,
                y_ref, st_ref, ubuf_ref, *, tm, d_ff, n_tiles):
    j = pl.program_id(1)
    lo = _PAD_ROWS - (CONV_W - 1)

    @pl.when(j == 0)
    def _():
        ubuf_ref[0:_PAD_ROWS, :] = jnp.zeros((_PAD_ROWS, ubuf_ref.shape[1]), F32)
        ubuf_ref[lo:_PAD_ROWS, :] = prev_ref[0]

    @pl.when(j > 0)
    def _():
        ubuf_ref[0:_PAD_ROWS, :] = ubuf_ref[tm:tm + _PAD_ROWS, :]

    xf = x_ref[...]
    hn = (xf * _rms_scale(xf, xf.shape[-1]) * g_ref[...]).astype(BF16)
    ubuf_ref[_PAD_ROWS:_PAD_ROWS + tm, :] = jnp.dot(hn, wup_ref[...],
                                                    preferred_element_type=F32)
    c = cb_ref[...]
    for tap in range(CONV_W):
        c = c + ubuf_ref[lo + tap:lo + tap + tm, :] * cw_ref[tap:tap + 1, :]
    gate, val = c[:, :d_ff], c[:, d_ff:]
    act = (gate * jax.nn.sigmoid(gate) * val).astype(BF16)
    y_ref[...] = xf + jnp.dot(act, wdn_ref[...], preferred_element_type=F32)

    @pl.when(j == n_tiles - 1)
    def _():
        st_ref[0] = ubuf_ref[tm + lo:tm + _PAD_ROWS, :]


def _conv_ffn(x, g, w_up, cw, cb, w_dn, prev, *, batch, seq, tm):
    n, d = x.shape
    d_ff = w_dn.shape[0]
    nt = seq // tm
    row = pl.BlockSpec((tm, d), lambda b, j: (b * nt + j, 0))
    st = pl.BlockSpec((1, CONV_W - 1, 2 * d_ff), lambda b, j: (b, 0, 0))
    return pl.pallas_call(
        functools.partial(_ffn_kernel, tm=tm, d_ff=d_ff, n_tiles=nt),
        grid=(batch, nt),
        in_specs=[row, _const_spec(g.shape), _const_spec(w_up.shape), _const_spec(cw.shape),
                  _const_spec(cb.shape), _const_spec(w_dn.shape), st],
        out_specs=[row, st],
        out_shape=(jax.ShapeDtypeStruct((n, d), F32),
                   jax.ShapeDtypeStruct((batch, CONV_W - 1, 2 * d_ff), F32)),
        scratch_shapes=[pltpu.VMEM((tm + _PAD_ROWS, 2 * d_ff), F32)],
        compiler_params=pltpu.CompilerParams(
            dimension_semantics=("arbitrary", "arbitrary"),
            vmem_limit_bytes=_vmem_limit(56 << 20)),
        name="conv_ffn",
    )(x, g, w_up, cw, cb, w_dn, prev)


def _pack_weights(w_in, w_q_up, w_kv_up, w_o_branch, w_out, w_ffn_up, w_ffn_down):
    sizes = (768, 512, 64, 512, 512, 1024, 1024, 1024, 1024)
    offs = np.concatenate([[0], np.cumsum(sizes)])
    parts = []
    for (name, width), a, b in zip(_IN_GROUPS, offs[:-1], offs[1:]):
        part = w_in[:, a:b]
        if b - a < width:
            part = jnp.pad(part, ((0, 0), (0, width - (b - a))))
        parts.append(part)
    w_in_p = jnp.concatenate(parts, axis=1).astype(BF16)
    wq = w_q_up.reshape(w_q_up.shape[0], MLA_HEADS, QK_DIM)
    w_q_p = jnp.concatenate([wq[:, :, :QK_NOPE].reshape(wq.shape[0], -1),
                             wq[:, :, QK_NOPE:].reshape(wq.shape[0], -1)], axis=1).astype(BF16)
    wkv = w_kv_up.reshape(w_kv_up.shape[0], MLA_HEADS, QK_NOPE + V_HEAD)
    w_kv_p = jnp.concatenate([wkv[:, :, :QK_NOPE].reshape(wkv.shape[0], -1),
                              wkv[:, :, QK_NOPE:].reshape(wkv.shape[0], -1)], axis=1).astype(BF16)
    mla_width = MLA_HEADS * V_HEAD
    return (w_in_p, w_q_p, w_kv_p, w_o_branch[:mla_width].astype(BF16),
            w_o_branch[mla_width:].astype(BF16), w_out.astype(BF16),
            w_ffn_up.astype(BF16), w_ffn_down.astype(BF16))


def _layer(x, pos, past, w, *, batch, seq, tm, tq, ret_chunk, ffn_tm):
    n = x.shape[0]
    tabs64 = _rope_tables(pos, RET_DK // 2)
    tabs32 = _rope_tables(pos, QK_ROPE // 2)
    seq_tiles = max(seq // tm, 1)
    if seq < tm:
        tabs64 = tuple(jnp.tile(t, (tm // seq, 1)) for t in tabs64)
        tabs32 = tuple(jnp.tile(t, (tm // seq, 1)) for t in tabs32)
    qlat, ckv, kpe, rq, rk, rv, rg, ga, gb = _in_proj(
        x, w['w_in'], w['g_norm_mix'], w['g_q_lat'], w['g_kv_lat'], w['g_k_rope'],
        tabs64, tabs32, seq_tiles=seq_tiles, tm=tm)
    q = _q_up(qlat, w['w_q_up'], w['g_q_nope'], w['g_q_rope'], tabs32,
              seq_tiles=seq_tiles, tm=tm)
    k_new, v_new = _kv_up(ckv, kpe, w['w_kv_up'], w['g_k_nope'], tm=tm,
                          transpose_v=past is None)
    if past is None:
        attn = _prompt_attention(q, k_new, v_new, batch=batch, seq=seq, tq=tq)
        s0 = jnp.zeros((batch, RET_HEADS, RET_DK, RET_DV), F32)
        prev = jnp.zeros((batch, CONV_W - 1, w['w_ffn_up'].shape[1]), F32)
    else:
        lat_c, pe_c, s0, prev = past
        past_len = lat_c.shape[1]
        k_c, v_c = _kv_up(lat_c.reshape(batch * past_len, -1), pe_c.reshape(batch * past_len, -1),
                          w['w_kv_up'], w['g_k_nope'], tm=512, transpose_v=False)
        attn = _sample_attention(q, k_c, v_c, k_new, v_new, batch=batch, past=past_len, new=seq)
    ret, s_fin = _retention(rq, rk, rv, rg, w['g_ret_out'], s0,
                            batch=batch, seq=seq, chunk=ret_chunk)
    x1 = _merge(x, attn, ret, ga, gb, w['w_o_a'], w['w_o_b'], w['w_out'], tm=tm)
    y, conv_state = _conv_ffn(x1, w['g_norm_ffn'], w['w_ffn_up'], w['ffn_conv_w'],
                              w['ffn_conv_b'], w['w_ffn_down'], prev,
                              batch=batch, seq=seq, tm=ffn_tm)
    return y, (ckv, kpe, s_fin, conv_state)


def kernel(x_prompt, x_sample, cache_mla_latent, cache_mla_rope_key, state_retention, state_ffn_conv, g_norm_mix, w_in, g_q_lat, w_q_up, g_q_nope, g_q_rope, g_kv_lat, w_kv_up, g_k_nope, g_k_rope, g_ret_out, w_o_branch, w_out, g_norm_ffn, w_ffn_up, ffn_conv_w, ffn_conv_b, w_ffn_down):
    depth = w_in.shape[0]
    assert depth == 1, "single-layer trunk"
    bp, tp, d = x_prompt.shape
    bs, ts, _ = x_sample.shape
    past_len = cache_mla_latent.shape[2]
    (w_in_p, w_q_p, w_kv_p, w_o_a, w_o_b, w_out_p, w_up_p, w_dn_p) = _pack_weights(
        w_in[0], w_q_up[0], w_kv_up[0], w_o_branch[0], w_out[0], w_ffn_up[0], w_ffn_down[0])
    row = lambda g: g.reshape(1, -1).astype(F32)
    w = dict(
        w_in=w_in_p, w_q_up=w_q_p, w_kv_up=w_kv_p, w_o_a=w_o_a, w_o_b=w_o_b, w_out=w_out_p,
        w_ffn_up=w_up_p, w_ffn_down=w_dn_p,
        g_norm_mix=row(g_norm_mix[0]), g_q_lat=row(g_q_lat[0]), g_kv_lat=row(g_kv_lat[0]),
        g_k_rope=jnp.pad(row(g_k_rope[0]), ((0, 0), (0, LANES - QK_ROPE))),
        g_q_nope=row(g_q_nope[0]), g_q_rope=jnp.tile(row(g_q_rope[0]), (1, LANES // QK_ROPE)),
        g_k_nope=row(g_k_nope[0]), g_ret_out=g_ret_out[0], g_norm_ffn=row(g_norm_ffn[0]),
        ffn_conv_w=ffn_conv_w[0], ffn_conv_b=row(ffn_conv_b[0]))

    yp, sp = _layer(x_prompt.reshape(bp * tp, d), jnp.arange(tp), None, w,
                    batch=bp, seq=tp, tm=512, tq=1024, ret_chunk=256, ffn_tm=256)
    past = (cache_mla_latent[0], cache_mla_rope_key[0], state_retention[0], state_ffn_conv[0])
    ys, ss = _layer(x_sample.reshape(bs * ts, d), past_len + jnp.arange(ts), past, w,
                    batch=bs, seq=ts, tm=bs * ts, tq=None, ret_chunk=ts, ffn_tm=ts)

    def states(st, b, t):
        ckv, kpe, s_fin, conv = st
        return (ckv.reshape(1, b, t, -1), kpe.reshape(1, b, t, -1), s_fin[None], conv[None])

    return (yp.reshape(bp, tp, d), ys.reshape(bs, ts, d)) + states(sp, bp, tp) + states(ss, bs, ts)
```

```python
import functools
import math

import jax
import jax.numpy as jnp
import numpy as np
from jax import lax
from jax.experimental import pallas as pl
from jax.experimental.pallas import tpu as pltpu

F32 = jnp.float32
BF16 = jnp.bfloat16

EPS = 1e-6
NEG_INF = -1e30
ROPE_THETA = 10000.0
CHUNK = 64
LANES = 128
V7X_VMEM_BYTES = 64 * 1024 * 1024

MLA_HEADS = 8
QK_NOPE = 128
QK_ROPE = 64
QK_DIM = QK_NOPE + QK_ROPE
V_HEAD = 128
MLA_SCALE = QK_DIM ** -0.5
Q_PRESCALE = MLA_SCALE * math.log2(math.e)
RET_HEADS = 4
RET_DK = 128
RET_DV = 256
CONV_W = 3


def _vmem_limit(nbytes):
    return int(min(V7X_VMEM_BYTES - (4 << 20), max(32 << 20, nbytes)))


def _const_spec(shape):
    nd = len(shape)
    return pl.BlockSpec(shape, lambda *_: (0,) * nd, pipeline_mode=pl.Buffered(1))


def _rms_scale(z, width):
    return lax.rsqrt(jnp.sum(z * z, axis=-1, keepdims=True) * (1.0 / width) + EPS)


def _swap_halves_64(z):
    lane = lax.broadcasted_iota(jnp.int32, z.shape, 1)
    return jnp.where((lane % 64) < 32, pltpu.roll(z, 96, 1), pltpu.roll(z, 32, 1))


def _rope_tables(pos, half):
    freqs = ROPE_THETA ** (-jnp.arange(half, dtype=F32) / half)
    ang = pos.astype(F32)[:, None] * freqs[None, :]
    cos, sin = jnp.cos(ang), jnp.sin(ang)
    reps = LANES // (2 * half)
    return (jnp.tile(jnp.concatenate([cos, cos], -1), (1, reps)),
            jnp.tile(jnp.concatenate([-sin, sin], -1), (1, reps)))


_IN_GROUPS = (('qlat', 768), ('kvl', 512), ('kpe', 128), ('rq', 512), ('rk', 512),
              ('rv', 1024), ('rg', 1024), ('ga', 1024), ('gb', 1024))
_IN_OFFS = {}
_o = 0
for _n, _w in _IN_GROUPS:
    _IN_OFFS[_n] = (_o, _o + _w)
    _o += _w
IN_PACKED = _o


def _inproj_kernel(x_ref, gmix_ref, w_ref, gq_ref, gkv_ref, gkpe_ref,
                   c64_ref, s64_ref, c32_ref, s32_ref,
                   qlat_ref, ckv_ref, kpe_ref, rq_ref, rk_ref, rv_ref, rg_ref, ga_ref, gb_ref):
    xf = x_ref[...]
    h = (xf * _rms_scale(xf, xf.shape[-1]) * gmix_ref[...]).astype(BF16)

    def proj(name):
        a, b = _IN_OFFS[name]
        return jnp.dot(h, w_ref[:, a:b], preferred_element_type=F32)

    z = proj('qlat')
    qlat_ref[...] = (z * _rms_scale(z, z.shape[-1]) * gq_ref[...]).astype(BF16)
    z = proj('kvl')
    ckv_ref[...] = z * _rms_scale(z, z.shape[-1]) * gkv_ref[...]
    z = proj('kpe')
    zn = z * _rms_scale(z, QK_ROPE) * gkpe_ref[...]
    kpe = zn * c32_ref[...] + _swap_halves_64(zn) * s32_ref[...]
    kpe_ref[...] = kpe[:, :QK_ROPE]
    c64, s64 = c64_ref[...], s64_ref[...]
    for name, ref, scale in (('rq', rq_ref, 1.0), ('rk', rk_ref, RET_DK ** -0.5)):
        z = proj(name)
        for hd in range(RET_HEADS):
            zh = z[:, hd * RET_DK:(hd + 1) * RET_DK]
            r = zh * c64 + pltpu.roll(zh, RET_DK // 2, 1) * s64
            if scale != 1.0:
                r = r * scale
            ref[:, hd * RET_DK:(hd + 1) * RET_DK] = r.astype(BF16)
    rv_ref[...] = proj('rv').astype(BF16)
    rg_ref[...] = proj('rg')
    ga_ref[...] = proj('ga')
    gb_ref[...] = proj('gb')


def _in_proj(x, w_in_packed, gmix, gq, gkv, gkpe, tabs64, tabs32, *, seq_tiles, tm):
    n, d = x.shape
    row = lambda w: pl.BlockSpec((tm, w), lambda i: (i, 0))
    tab = pl.BlockSpec((tm, LANES), lambda i: (i % seq_tiles, 0))
    out_shapes = (
        jax.ShapeDtypeStruct((n, 768), BF16), jax.ShapeDtypeStruct((n, 512), F32),
        jax.ShapeDtypeStruct((n, QK_ROPE), F32), jax.ShapeDtypeStruct((n, 512), BF16),
        jax.ShapeDtypeStruct((n, 512), BF16), jax.ShapeDtypeStruct((n, 1024), BF16),
        jax.ShapeDtypeStruct((n, 1024), F32), jax.ShapeDtypeStruct((n, 1024), F32),
        jax.ShapeDtypeStruct((n, 1024), F32))
    return pl.pallas_call(
        _inproj_kernel,
        grid=(n // tm,),
        in_specs=[row(d), _const_spec(gmix.shape), _const_spec(w_in_packed.shape),
                  _const_spec(gq.shape), _const_spec(gkv.shape), _const_spec(gkpe.shape),
                  tab, tab, tab, tab],
        out_specs=[row(s.shape[1]) for s in out_shapes],
        out_shape=out_shapes,
        compiler_params=pltpu.CompilerParams(
            dimension_semantics=("arbitrary",), vmem_limit_bytes=_vmem_limit(56 << 20)),
        name="in_proj",
    )(x, gmix, w_in_packed, gq, gkv, gkpe, *tabs64, *tabs32)


def _qup_kernel(ql_ref, w_ref, gn_ref, gr_ref, c32_ref, s32_ref, q_ref):
    ql = ql_ref[...]
    zn = jnp.dot(ql, w_ref[:, :MLA_HEADS * QK_NOPE], preferred_element_type=F32)
    zr = jnp.dot(ql, w_ref[:, MLA_HEADS * QK_NOPE:], preferred_element_type=F32)
    gn = gn_ref[...]
    for hd in range(MLA_HEADS):
        zh = zn[:, hd * QK_NOPE:(hd + 1) * QK_NOPE]
        q_ref[hd, :, :QK_NOPE] = (zh * (_rms_scale(zh, QK_NOPE) * Q_PRESCALE) * gn).astype(BF16)
    c32, s32, gr = c32_ref[...], s32_ref[...], gr_ref[...]
    lane = lax.broadcasted_iota(jnp.int32, c32.shape, 1)
    for pair in range(MLA_HEADS // 2):
        zp = zr[:, pair * LANES:(pair + 1) * LANES]
        sq = zp * zp
        lo = jnp.sum(jnp.where(lane < QK_ROPE, sq, 0.0), axis=-1, keepdims=True)
        hi = jnp.sum(sq, axis=-1, keepdims=True) - lo
        inv = jnp.where(lane < QK_ROPE, lax.rsqrt(lo * (1.0 / QK_ROPE) + EPS),
                        lax.rsqrt(hi * (1.0 / QK_ROPE) + EPS))
        z = zp * (inv * Q_PRESCALE) * gr
        r = (z * c32 + _swap_halves_64(z) * s32).astype(BF16)
        q_ref[2 * pair, :, QK_NOPE:] = r[:, :QK_ROPE]
        q_ref[2 * pair + 1, :, QK_NOPE:] = r[:, QK_ROPE:]


def _q_up(qlat, w_q, gn, gr, tabs32, *, seq_tiles, tm):
    n = qlat.shape[0]
    tab = pl.BlockSpec((tm, LANES), lambda i: (i % seq_tiles, 0))
    return pl.pallas_call(
        _qup_kernel,
        grid=(n // tm,),
        in_specs=[pl.BlockSpec((tm, qlat.shape[1]), lambda i: (i, 0)), _const_spec(w_q.shape),
                  _const_spec(gn.shape), _const_spec(gr.shape), tab, tab],
        out_specs=pl.BlockSpec((MLA_HEADS, tm, QK_DIM), lambda i: (0, i, 0)),
        out_shape=jax.ShapeDtypeStruct((MLA_HEADS, n, QK_DIM), BF16),
        compiler_params=pltpu.CompilerParams(
            dimension_semantics=("arbitrary",), vmem_limit_bytes=_vmem_limit(40 << 20)),
        name="q_up",
    )(qlat, w_q, gn, gr, *tabs32)


def _kvup_kernel(c_ref, kpe_ref, w_ref, gk_ref, k_ref, v_ref, *, transpose_v):
    c = c_ref[...].astype(BF16)
    kpe = kpe_ref[...].astype(BF16)
    gk = gk_ref[...]
    kw = MLA_HEADS * QK_NOPE
    for pair in range(MLA_HEADS // 2):
        zk = jnp.dot(c, w_ref[:, pair * 256:(pair + 1) * 256], preferred_element_type=F32)
        zv = jnp.dot(c, w_ref[:, kw + pair * 256:kw + (pair + 1) * 256],
                     preferred_element_type=F32)
        for sub in range(2):
            hd = 2 * pair + sub
            zh = zk[:, sub * QK_NOPE:(sub + 1) * QK_NOPE]
            k_ref[hd, :, :QK_NOPE] = (zh * _rms_scale(zh, QK_NOPE) * gk).astype(BF16)
            k_ref[hd, :, QK_NOPE:] = kpe
            vh = zv[:, sub * V_HEAD:(sub + 1) * V_HEAD]
            if transpose_v:
                v_ref[hd, :V_HEAD, :] = vh.T.astype(BF16)
                v_ref[hd, V_HEAD:, :] = jnp.ones((_ONES_ROWS, vh.shape[0]), BF16)
            else:
                v_ref[hd] = vh.astype(BF16)


_ONES_ROWS = 16


def _kv_up(ckv, kpe, w_kv, gk, *, tm, transpose_v):
    n = ckv.shape[0]
    if transpose_v:
        v_spec = pl.BlockSpec((MLA_HEADS, V_HEAD + _ONES_ROWS, tm), lambda i: (0, 0, i))
        v_shape = (MLA_HEADS, V_HEAD + _ONES_ROWS, n)
    else:
        v_spec = pl.BlockSpec((MLA_HEADS, tm, V_HEAD), lambda i: (0, i, 0))
        v_shape = (MLA_HEADS, n, V_HEAD)
    return pl.pallas_call(
        functools.partial(_kvup_kernel, transpose_v=transpose_v),
        grid=(n // tm,),
        in_specs=[pl.BlockSpec((tm, ckv.shape[1]), lambda i: (i, 0)),
                  pl.BlockSpec((tm, QK_ROPE), lambda i: (i, 0)),
                  _const_spec(w_kv.shape), _const_spec(gk.shape)],
        out_specs=[pl.BlockSpec((MLA_HEADS, tm, QK_DIM), lambda i: (0, i, 0)), v_spec],
        out_shape=(jax.ShapeDtypeStruct((MLA_HEADS, n, QK_DIM), BF16),
                   jax.ShapeDtypeStruct(v_shape, BF16)),
        compiler_params=pltpu.CompilerParams(
            dimension_semantics=("arbitrary",), vmem_limit_bytes=_vmem_limit(40 << 20)),
        name="kv_up",
    )(ckv, kpe, w_kv, gk)


def _qk(q, k):
    return lax.dot_general(q, k, (((1,), (1,)), ((), ())), preferred_element_type=F32)


_STRIP = 256
_ATTN_KEY_BLOCK = 512


def _prompt_attn_kernel(q_ref, k_ref, vt_ref, o_ref, s_ref, smax_ref, p_ref, a_ref, m_ref,
                        acc_ref, *, tq, tk):
    i = pl.program_id(2)
    n_strips = tq // _STRIP
    n_diag = tq // tk
    assert n_diag % 2 == 0 and tk % _STRIP == 0 and _STRIP % CHUNK == 0
    all_strips = tuple(range(n_strips))
    cols = lambda c: slice(c * _STRIP, (c + 1) * _STRIP)
    seen = lambda d: tuple(c for c in all_strips if (c + 1) * _STRIP > d * tk)
    unmasked = lambda d, c: c * _STRIP >= (d + 1) * tk

    def scores(b, slot, c):
        start = pl.multiple_of(b * tk, tk)
        s = _qk(k_ref[0, pl.ds(start, tk), :], q_ref[0, cols(c), :])
        s_ref[slot, :, cols(c)] = s
        smax_ref[slot, :, cols(c)] = jnp.max(s, axis=0, keepdims=True)

    def softmax(slot, c, diag_block=None):
        s = s_ref[slot, :, cols(c)]
        if diag_block is None or unmasked(diag_block, c):
            s_max = smax_ref[slot, :, cols(c)]
        else:
            kc = (lax.broadcasted_iota(jnp.int32, s.shape, 0) + diag_block * tk) // CHUNK
            qc = (lax.broadcasted_iota(jnp.int32, s.shape, 1) + c * _STRIP) // CHUNK
            s = jnp.where(kc <= qc, s, NEG_INF)
            s_max = jnp.max(s, axis=0, keepdims=True)
        m_prev = m_ref[:, cols(c)]
        m_new = jnp.maximum(m_prev, s_max)
        alpha = jnp.exp2(m_prev - m_new)
        p = jnp.exp2(s - m_new)
        m_ref[:, cols(c)] = m_new
        a_ref[slot, :, cols(c)] = alpha
        p_ref[slot, :, cols(c)] = p.astype(BF16)

    def values(b, slot, c):
        start = pl.multiple_of(jnp.maximum(b, 0) * tk, tk)
        pv = jnp.dot(vt_ref[0, :, pl.ds(start, tk)], p_ref[slot, :, cols(c)],
                     preferred_element_type=F32)
        acc_ref[:, cols(c)] = acc_ref[:, cols(c)] * a_ref[slot, :, cols(c)] + pv

    m_ref[...] = jnp.full(m_ref.shape, NEG_INF, F32)
    acc_ref[...] = jnp.zeros(acc_ref.shape, F32)
    p_ref[1] = jnp.zeros(p_ref.shape[1:], BF16)
    a_ref[1] = jnp.ones(a_ref.shape[1:], F32)
    for c in all_strips:
        scores(0, 0, c)

    def pair(jj, carry):
        b = 2 * jj
        for c in all_strips:
            scores(b + 1, 1, c)
        for c in all_strips:
            values(b - 1, 1, c)
        for c in all_strips:
            softmax(0, c)
        for c in all_strips:
            scores(b + 2, 0, c)
        for c in all_strips:
            values(b, 0, c)
        for c in all_strips:
            softmax(1, c)
        return carry

    lax.fori_loop(0, i * (n_diag // 2), pair, 0)
    b0 = n_diag * i
    for t in range(n_diag + 2):
        if 1 <= t + 1 < n_diag:
            for c in seen(t + 1):
                scores(b0 + t + 1, (t + 1) % 2, c)
        if t == 0:
            for c in all_strips:
                values(b0 - 1, 1, c)
        elif t - 1 < n_diag:
            for c in seen(t - 1):
                values(b0 + t - 1, (t - 1) % 2, c)
        if t < n_diag:
            for c in seen(t):
                softmax(t % 2, c, diag_block=t)
    o = acc_ref[:V_HEAD, :] / acc_ref[V_HEAD:V_HEAD + 1, :]
    o_ref[...] = o.T.astype(o_ref.dtype)


def _prompt_attention(q, k, vt, *, batch, seq, tq):
    heads, n, _ = q.shape
    nq = seq // tq
    tk = _ATTN_KEY_BLOCK
    return pl.pallas_call(
        functools.partial(_prompt_attn_kernel, tq=tq, tk=tk),
        grid=(batch, heads, nq),
        in_specs=[pl.BlockSpec((1, tq, QK_DIM), lambda b, h, i: (h, b * nq + i, 0)),
                  pl.BlockSpec((1, seq, QK_DIM), lambda b, h, i: (h, b, 0)),
                  pl.BlockSpec((1, vt.shape[1], seq), lambda b, h, i: (h, 0, b))],
        out_specs=pl.BlockSpec((tq, V_HEAD), lambda b, h, i: (b * nq + i, h)),
        out_shape=jax.ShapeDtypeStruct((n, heads * V_HEAD), BF16),
        scratch_shapes=[pltpu.VMEM((2, tk, tq), F32), pltpu.VMEM((2, 1, tq), F32),
                        pltpu.VMEM((2, tk, tq), BF16),
                        pltpu.VMEM((2, 1, tq), F32), pltpu.VMEM((1, tq), F32),
                        pltpu.VMEM((vt.shape[1], tq), F32)],
        compiler_params=pltpu.CompilerParams(
            dimension_semantics=("arbitrary", "arbitrary", "arbitrary"),
            vmem_limit_bytes=_vmem_limit(40 << 20)),
        name="prompt_attention",
    )(q, k, vt)


def _sample_attn_kernel(q_ref, kc_ref, vc_ref, kn_ref, vn_ref, o_ref):
    q = q_ref[0]
    s_c = _qk(q, kc_ref[0])
    s_n = _qk(q, kn_ref[0])
    m = jnp.maximum(jnp.max(s_c, axis=-1, keepdims=True), jnp.max(s_n, axis=-1, keepdims=True))
    p_c = jnp.exp2(s_c - m)
    p_n = jnp.exp2(s_n - m)
    l = jnp.sum(p_c, axis=-1, keepdims=True) + jnp.sum(p_n, axis=-1, keepdims=True)
    acc = (jnp.dot(p_c.astype(BF16), vc_ref[0], preferred_element_type=F32)
           + jnp.dot(p_n.astype(BF16), vn_ref[0], preferred_element_type=F32))
    o_ref[...] = (acc / l).astype(o_ref.dtype)


def _sample_attention(q, kc, vc, kn, vn, *, batch, past, new):
    heads = q.shape[0]
    return pl.pallas_call(
        _sample_attn_kernel,
        grid=(batch, heads),
        in_specs=[pl.BlockSpec((1, new, QK_DIM), lambda b, h: (h, b, 0)),
                  pl.BlockSpec((1, past, QK_DIM), lambda b, h: (h, b, 0)),
                  pl.BlockSpec((1, past, V_HEAD), lambda b, h: (h, b, 0)),
                  pl.BlockSpec((1, new, QK_DIM), lambda b, h: (h, b, 0)),
                  pl.BlockSpec((1, new, V_HEAD), lambda b, h: (h, b, 0))],
        out_specs=pl.BlockSpec((new, V_HEAD), lambda b, h: (b, h)),
        out_shape=jax.ShapeDtypeStruct((batch * new, heads * V_HEAD), BF16),
        compiler_params=pltpu.CompilerParams(
            dimension_semantics=("arbitrary", "arbitrary")),
        name="sample_attention",
    )(q, kc, vc, kn, vn)


def _retention_kernel(q_ref, k_ref, v_ref, rg_ref, dec_ref, qd_ref, kd_ref, g_ref, s0_ref,
                      o_ref, sfin_ref, state_ref, *, chunk, n_chunks):
    c = pl.program_id(1)

    @pl.when(c == 0)
    def _():
        state_ref[...] = s0_ref[0]

    heads = range(RET_HEADS)
    kcols = [slice(hd * RET_DK, (hd + 1) * RET_DK) for hd in heads]
    vcols = [slice(hd * RET_DV, (hd + 1) * RET_DV) for hd in heads]
    q = [q_ref[:, kcols[hd]] for hd in heads]
    k = [k_ref[:, kcols[hd]] for hd in heads]
    v = [v_ref[:, vcols[hd]] for hd in heads]
    state = [state_ref[hd] for hd in heads]
    qd = [qd_ref[hd] for hd in heads]
    s = [_qk(q[hd], k[hd]) for hd in heads]
    cross = [jnp.dot(q[hd], state[hd].astype(BF16), preferred_element_type=F32) for hd in heads]
    s = [(s[hd] * dec_ref[hd]).astype(BF16) for hd in heads]
    k_dec = [(k[hd].astype(F32) * kd_ref[hd]).astype(BF16) for hd in heads]
    inner = [jnp.dot(s[hd], v[hd], preferred_element_type=F32) for hd in heads]
    kv = [lax.dot_general(k_dec[hd], v[hd], (((0,), (0,)), ((), ())),
                          preferred_element_type=F32) for hd in heads]
    for hd in heads:
        state_ref[hd] = state[hd] * qd[hd][chunk - 1:chunk, :] + kv[hd]
    for hd in heads:
        o = inner[hd] + cross[hd] * qd[hd]
        mu = jnp.mean(o, axis=-1, keepdims=True)
        oc = o - mu
        var = jnp.mean(oc * oc, axis=-1, keepdims=True)
        y = oc * lax.rsqrt(var + EPS) * g_ref[hd]
        rg = rg_ref[:, vcols[hd]]
        o_ref[:, vcols[hd]] = (rg * jax.nn.sigmoid(rg) * y).astype(o_ref.dtype)

    @pl.when(c == n_chunks - 1)
    def _():
        sfin_ref[0] = state_ref[...]


def _retention_tables(chunk):
    lg = jnp.log1p(-(2.0 ** (-5.0 - jnp.arange(RET_HEADS, dtype=F32))))
    idx = jnp.arange(chunk, dtype=F32)
    diff = idx[:, None] - idx[None, :]
    decay = jnp.where(diff[None] >= 0,
                      jnp.exp(jnp.maximum(diff, 0.0)[None] * lg[:, None, None]), 0.0)
    q_dec = jnp.exp((idx + 1.0)[None, :] * lg[:, None])[..., None]
    k_dec = jnp.exp((chunk - 1.0 - idx)[None, :] * lg[:, None])[..., None]
    return decay, q_dec, k_dec


def _retention(rq, rk, rv, rg, g_ret, s0, *, batch, seq, chunk):
    n = rq.shape[0]
    nc = seq // chunk
    decay, q_dec, k_dec = _retention_tables(chunk)
    tok = lambda w: pl.BlockSpec((chunk, w), lambda b, c: (b * nc + c, 0))
    state_spec = pl.BlockSpec((1, RET_HEADS, RET_DK, RET_DV), lambda b, c: (b, 0, 0, 0))
    g_ret = g_ret.reshape(RET_HEADS, 1, RET_DV)
    return pl.pallas_call(
        functools.partial(_retention_kernel, chunk=chunk, n_chunks=nc),
        grid=(batch, nc),
        in_specs=[tok(RET_HEADS * RET_DK), tok(RET_HEADS * RET_DK), tok(RET_HEADS * RET_DV),
                  tok(RET_HEADS * RET_DV), _const_spec(decay.shape), _const_spec(q_dec.shape),
                  _const_spec(k_dec.shape), _const_spec(g_ret.shape), state_spec],
        out_specs=[tok(RET_HEADS * RET_DV), state_spec],
        out_shape=(jax.ShapeDtypeStruct((n, RET_HEADS * RET_DV), BF16),
                   jax.ShapeDtypeStruct((batch, RET_HEADS, RET_DK, RET_DV), F32)),
        scratch_shapes=[pltpu.VMEM((RET_HEADS, RET_DK, RET_DV), F32)],
        compiler_params=pltpu.CompilerParams(dimension_semantics=("arbitrary", "arbitrary")),
        name="retention",
    )(rq, rk, rv, rg, decay, q_dec, k_dec, g_ret, s0)


def _merge_kernel(x_ref, a_ref, r_ref, ga_ref, gb_ref, wa_ref, wb_ref, wo_ref, o_ref):
    a_d = jnp.dot(a_ref[...], wa_ref[...], preferred_element_type=F32)
    r_d = jnp.dot(r_ref[...], wb_ref[...], preferred_element_type=F32)
    merged = jax.nn.sigmoid(ga_ref[...]) * a_d + jax.nn.sigmoid(gb_ref[...]) * r_d
    o_ref[...] = x_ref[...] + jnp.dot(merged.astype(BF16), wo_ref[...],
                                      preferred_element_type=F32)


def _merge(x, attn, ret, ga, gb, wa, wb, wo, *, tm):
    n, d = x.shape
    row = pl.BlockSpec((tm, d), lambda i: (i, 0))
    return pl.pallas_call(
        _merge_kernel,
        grid=(n // tm,),
        in_specs=[row, row, row, row, row,
                  _const_spec(wa.shape), _const_spec(wb.shape), _const_spec(wo.shape)],
        out_specs=row,
        out_shape=jax.ShapeDtypeStruct((n, d), F32),
        compiler_params=pltpu.CompilerParams(
            dimension_semantics=("arbitrary",), vmem_limit_bytes=_vmem_limit(40 << 20)),
        name="merge",
    )(x, attn, ret, ga, gb, wa, wb, wo)


_PAD_ROWS = 8


def _ffn_kernel(x_ref, g_ref, wup_ref, cw_ref, cb_ref, wdn_ref, prev_ref,
                y_ref, st_ref, ubuf_ref, *, tm, d_ff, n_tiles):
    j = pl.program_id(1)
    lo = _PAD_ROWS - (CONV_W - 1)

    @pl.when(j == 0)
    def _():
        ubuf_ref[0:_PAD_ROWS, :] = jnp.zeros((_PAD_ROWS, ubuf_ref.shape[1]), F32)
        ubuf_ref[lo:_PAD_ROWS, :] = prev_ref[0]

    @pl.when(j > 0)
    def _():
        ubuf_ref[0:_PAD_ROWS, :] = ubuf_ref[tm:tm + _PAD_ROWS, :]

    xf = x_ref[...]
    hn = (xf * _rms_scale(xf, xf.shape[-1]) * g_ref[...]).astype(BF16)
    ubuf_ref[_PAD_ROWS:_PAD_ROWS + tm, :] = jnp.dot(hn, wup_ref[...],
                                                    preferred_element_type=F32)
    c = cb_ref[...]
    for tap in range(CONV_W):
        c = c + ubuf_ref[lo + tap:lo + tap + tm, :] * cw_ref[tap:tap + 1, :]
    gate, val = c[:, :d_ff], c[:, d_ff:]
    act = (gate * jax.nn.sigmoid(gate) * val).astype(BF16)
    y_ref[...] = xf + jnp.dot(act, wdn_ref[...], preferred_element_type=F32)

    @pl.when(j == n_tiles - 1)
    def _():
        st_ref[0] = ubuf_ref[tm + lo:tm + _PAD_ROWS, :]


def _conv_ffn(x, g, w_up, cw, cb, w_dn, prev, *, batch, seq, tm):
    n, d = x.shape
    d_ff = w_dn.shape[0]
    nt = seq // tm
    row = pl.BlockSpec((tm, d), lambda b, j: (b * nt + j, 0))
    st = pl.BlockSpec((1, CONV_W - 1, 2 * d_ff), lambda b, j: (b, 0, 0))
    return pl.pallas_call(
        functools.partial(_ffn_kernel, tm=tm, d_ff=d_ff, n_tiles=nt),
        grid=(batch, nt),
        in_specs=[row, _const_spec(g.shape), _const_spec(w_up.shape), _const_spec(cw.shape),
                  _const_spec(cb.shape), _const_spec(w_dn.shape), st],
        out_specs=[row, st],
        out_shape=(jax.ShapeDtypeStruct((n, d), F32),
                   jax.ShapeDtypeStruct((batch, CONV_W - 1, 2 * d_ff), F32)),
        scratch_shapes=[pltpu.VMEM((tm + _PAD_ROWS, 2 * d_ff), F32)],
        compiler_params=pltpu.CompilerParams(
            dimension_semantics=("arbitrary", "arbitrary"),
            vmem_limit_bytes=_vmem_limit(56 << 20)),
        name="conv_ffn",
    )(x, g, w_up, cw, cb, w_dn, prev)


def _pack_weights(w_in, w_q_up, w_kv_up, w_o_branch, w_out, w_ffn_up, w_ffn_down):
    sizes = (768, 512, 64, 512, 512, 1024, 1024, 1024, 1024)
    offs = np.concatenate([[0], np.cumsum(sizes)])
    parts = []
    for (name, width), a, b in zip(_IN_GROUPS, offs[:-1], offs[1:]):
        part = w_in[:, a:b]
        if b - a < width:
            part = jnp.pad(part, ((0, 0), (0, width - (b - a))))
        parts.append(part)
    w_in_p = jnp.concatenate(parts, axis=1).astype(BF16)
    wq = w_q_up.reshape(w_q_up.shape[0], MLA_HEADS, QK_DIM)
    w_q_p = jnp.concatenate([wq[:, :, :QK_NOPE].reshape(wq.shape[0], -1),
                             wq[:, :, QK_NOPE:].reshape(wq.shape[0], -1)], axis=1).astype(BF16)
    wkv = w_kv_up.reshape(w_kv_up.shape[0], MLA_HEADS, QK_NOPE + V_HEAD)
    w_kv_p = jnp.concatenate([wkv[:, :, :QK_NOPE].reshape(wkv.shape[0], -1),
                              wkv[:, :, QK_NOPE:].reshape(wkv.shape[0], -1)], axis=1).astype(BF16)
    mla_width = MLA_HEADS * V_HEAD
    return (w_in_p, w_q_p, w_kv_p, w_o_branch[:mla_width].astype(BF16),
            w_o_branch[mla_width:].astype(BF16), w_out.astype(BF16),
            w_ffn_up.astype(BF16), w_ffn_down.astype(BF16))


def _layer(x, pos, past, w, *, batch, seq, tm, tq, ret_chunk, ffn_tm):
    n = x.shape[0]
    tabs64 = _rope_tables(pos, RET_DK // 2)
    tabs32 = _rope_tables(pos, QK_ROPE // 2)
    seq_tiles = max(seq // tm, 1)
    if seq < tm:
        tabs64 = tuple(jnp.tile(t, (tm // seq, 1)) for t in tabs64)
        tabs32 = tuple(jnp.tile(t, (tm // seq, 1)) for t in tabs32)
    qlat, ckv, kpe, rq, rk, rv, rg, ga, gb = _in_proj(
        x, w['w_in'], w['g_norm_mix'], w['g_q_lat'], w['g_kv_lat'], w['g_k_rope'],
        tabs64, tabs32, seq_tiles=seq_tiles, tm=tm)
    q = _q_up(qlat, w['w_q_up'], w['g_q_nope'], w['g_q_rope'], tabs32,
              seq_tiles=seq_tiles, tm=tm)
    k_new, v_new = _kv_up(ckv, kpe, w['w_kv_up'], w['g_k_nope'], tm=tm,
                          transpose_v=past is None)
    if past is None:
        attn = _prompt_attention(q, k_new, v_new, batch=batch, seq=seq, tq=tq)
        s0 = jnp.zeros((batch, RET_HEADS, RET_DK, RET_DV), F32)
        prev = jnp.zeros((batch, CONV_W - 1, w['w_ffn_up'].shape[1]), F32)
    else:
        lat_c, pe_c, s0, prev = past
        past_len = lat_c.shape[1]
        k_c, v_c = _kv_up(lat_c.reshape(batch * past_len, -1), pe_c.reshape(batch * past_len, -1),
                          w['w_kv_up'], w['g_k_nope'], tm=512, transpose_v=False)
        attn = _sample_attention(q, k_c, v_c, k_new, v_new, batch=batch, past=past_len, new=seq)
    ret, s_fin = _retention(rq, rk, rv, rg, w['g_ret_out'], s0,
                            batch=batch, seq=seq, chunk=ret_chunk)
    x1 = _merge(x, attn, ret, ga, gb, w['w_o_a'], w['w_o_b'], w['w_out'], tm=tm)
    y, conv_state = _conv_ffn(x1, w['g_norm_ffn'], w['w_ffn_up'], w['ffn_conv_w'],
                              w['ffn_conv_b'], w['w_ffn_down'], prev,
                              batch=batch, seq=seq, tm=ffn_tm)
    return y, (ckv, kpe, s_fin, conv_state)


def kernel(x_prompt, x_sample, cache_mla_latent, cache_mla_rope_key, state_retention, state_ffn_conv, g_norm_mix, w_in, g_q_lat, w_q_up, g_q_nope, g_q_rope, g_kv_lat, w_kv_up, g_k_nope, g_k_rope, g_ret_out, w_o_branch, w_out, g_norm_ffn, w_ffn_up, ffn_conv_w, ffn_conv_b, w_ffn_down):
    depth = w_in.shape[0]
    assert depth == 1, "single-layer trunk"
    bp, tp, d = x_prompt.shape
    bs, ts, _ = x_sample.shape
    past_len = cache_mla_latent.shape[2]
    (w_in_p, w_q_p, w_kv_p, w_o_a, w_o_b, w_out_p, w_up_p, w_dn_p) = _pack_weights(
        w_in[0], w_q_up[0], w_kv_up[0], w_o_branch[0], w_out[0], w_ffn_up[0], w_ffn_down[0])
    row = lambda g: g.reshape(1, -1).astype(F32)
    w = dict(
        w_in=w_in_p, w_q_up=w_q_p, w_kv_up=w_kv_p, w_o_a=w_o_a, w_o_b=w_o_b, w_out=w_out_p,
        w_ffn_up=w_up_p, w_ffn_down=w_dn_p,
        g_norm_mix=row(g_norm_mix[0]), g_q_lat=row(g_q_lat[0]), g_kv_lat=row(g_kv_lat[0]),
        g_k_rope=jnp.pad(row(g_k_rope[0]), ((0, 0), (0, LANES - QK_ROPE))),
        g_q_nope=row(g_q_nope[0]), g_q_rope=jnp.tile(row(g_q_rope[0]), (1, LANES // QK_ROPE)),
        g_k_nope=row(g_k_nope[0]), g_ret_out=g_ret_out[0], g_norm_ffn=row(g_norm_ffn[0]),
        ffn_conv_w=ffn_conv_w[0], ffn_conv_b=row(ffn_conv_b[0]))

    yp, sp = _layer(x_prompt.reshape(bp * tp, d), jnp.arange(tp), None, w,
                    batch=bp, seq=tp, tm=512, tq=2048, ret_chunk=256, ffn_tm=256)
    past = (cache_mla_latent[0], cache_mla_rope_key[0], state_retention[0], state_ffn_conv[0])
    ys, ss = _layer(x_sample.reshape(bs * ts, d), past_len + jnp.arange(ts), past, w,
                    batch=bs, seq=ts, tm=bs * ts, tq=None, ret_chunk=ts, ffn_tm=ts)

    def states(st, b, t):
        ckv, kpe, s_fin, conv = st
        return (ckv.reshape(1, b, t, -1), kpe.reshape(1, b, t, -1), s_fin[None], conv[None])

    return (yp.reshape(bp, tp, d), ys.reshape(bs, ts, d)) + states(sp, bp, tp) + states(ss, bs, ts)
```

```python
import functools
import math

import jax
import jax.numpy as jnp
import numpy as np
from jax import lax
from jax.experimental import pallas as pl
from jax.experimental.pallas import tpu as pltpu

F32 = jnp.float32
BF16 = jnp.bfloat16

EPS = 1e-6
NEG_INF = -1e30
ROPE_THETA = 10000.0
CHUNK = 64
LANES = 128
V7X_VMEM_BYTES = 64 * 1024 * 1024

MLA_HEADS = 8
QK_NOPE = 128
QK_ROPE = 64
QK_DIM = QK_NOPE + QK_ROPE
V_HEAD = 128
MLA_SCALE = QK_DIM ** -0.5
Q_PRESCALE = MLA_SCALE * math.log2(math.e)
RET_HEADS = 4
RET_DK = 128
RET_DV = 256
CONV_W = 3


def _vmem_limit(nbytes):
    return int(min(V7X_VMEM_BYTES - (4 << 20), max(32 << 20, nbytes)))


def _const_spec(shape):
    nd = len(shape)
    return pl.BlockSpec(shape, lambda *_: (0,) * nd, pipeline_mode=pl.Buffered(1))


def _rms_scale(z, width):
    return lax.rsqrt(jnp.sum(z * z, axis=-1, keepdims=True) * (1.0 / width) + EPS)


def _swap_halves_64(z):
    lane = lax.broadcasted_iota(jnp.int32, z.shape, 1)
    return jnp.where((lane % 64) < 32, pltpu.roll(z, 96, 1), pltpu.roll(z, 32, 1))


def _rope_tables(pos, half):
    freqs = ROPE_THETA ** (-jnp.arange(half, dtype=F32) / half)
    ang = pos.astype(F32)[:, None] * freqs[None, :]
    cos, sin = jnp.cos(ang), jnp.sin(ang)
    reps = LANES // (2 * half)
    return (jnp.tile(jnp.concatenate([cos, cos], -1), (1, reps)),
            jnp.tile(jnp.concatenate([-sin, sin], -1), (1, reps)))


_IN_GROUPS = (('qlat', 768), ('kvl', 512), ('kpe', 128), ('rq', 512), ('rk', 512),
              ('rv', 1024), ('rg', 1024), ('ga', 1024), ('gb', 1024))
_IN_OFFS = {}
_o = 0
for _n, _w in _IN_GROUPS:
    _IN_OFFS[_n] = (_o, _o + _w)
    _o += _w
IN_PACKED = _o


def _inproj_kernel(x_ref, gmix_ref, w_ref, gq_ref, gkv_ref, gkpe_ref,
                   c64_ref, s64_ref, c32_ref, s32_ref,
                   qlat_ref, ckv_ref, kpe_ref, rq_ref, rk_ref, rv_ref, rg_ref, ga_ref, gb_ref):
    xf = x_ref[...]
    h = (xf * _rms_scale(xf, xf.shape[-1]) * gmix_ref[...]).astype(BF16)

    def proj(name):
        a, b = _IN_OFFS[name]
        return jnp.dot(h, w_ref[:, a:b], preferred_element_type=F32)

    z = proj('qlat')
    qlat_ref[...] = (z * _rms_scale(z, z.shape[-1]) * gq_ref[...]).astype(BF16)
    z = proj('kvl')
    ckv_ref[...] = z * _rms_scale(z, z.shape[-1]) * gkv_ref[...]
    z = proj('kpe')
    zn = z * _rms_scale(z, QK_ROPE) * gkpe_ref[...]
    kpe = zn * c32_ref[...] + _swap_halves_64(zn) * s32_ref[...]
    kpe_ref[...] = kpe[:, :QK_ROPE]
    c64, s64 = c64_ref[...], s64_ref[...]
    for name, ref, scale in (('rq', rq_ref, 1.0), ('rk', rk_ref, RET_DK ** -0.5)):
        z = proj(name)
        for hd in range(RET_HEADS):
            zh = z[:, hd * RET_DK:(hd + 1) * RET_DK]
            r = zh * c64 + pltpu.roll(zh, RET_DK // 2, 1) * s64
            if scale != 1.0:
                r = r * scale
            ref[:, hd * RET_DK:(hd + 1) * RET_DK] = r.astype(BF16)
    rv_ref[...] = proj('rv').astype(BF16)
    rg_ref[...] = proj('rg')
    ga_ref[...] = proj('ga')
    gb_ref[...] = proj('gb')


def _in_proj(x, w_in_packed, gmix, gq, gkv, gkpe, tabs64, tabs32, *, seq_tiles, tm):
    n, d = x.shape
    row = lambda w: pl.BlockSpec((tm, w), lambda i: (i, 0))
    tab = pl.BlockSpec((tm, LANES), lambda i: (i % seq_tiles, 0))
    out_shapes = (
        jax.ShapeDtypeStruct((n, 768), BF16), jax.ShapeDtypeStruct((n, 512), F32),
        jax.ShapeDtypeStruct((n, QK_ROPE), F32), jax.ShapeDtypeStruct((n, 512), BF16),
        jax.ShapeDtypeStruct((n, 512), BF16), jax.ShapeDtypeStruct((n, 1024), BF16),
        jax.ShapeDtypeStruct((n, 1024), F32), jax.ShapeDtypeStruct((n, 1024), F32),
        jax.ShapeDtypeStruct((n, 1024), F32))
    return pl.pallas_call(
        _inproj_kernel,
        grid=(n // tm,),
        in_specs=[row(d), _const_spec(gmix.shape), _const_spec(w_in_packed.shape),
                  _const_spec(gq.shape), _const_spec(gkv.shape), _const_spec(gkpe.shape),
                  tab, tab, tab, tab],
        out_specs=[row(s.shape[1]) for s in out_shapes],
        out_shape=out_shapes,
        compiler_params=pltpu.CompilerParams(
            dimension_semantics=("arbitrary",), vmem_limit_bytes=_vmem_limit(56 << 20)),
        name="in_proj",
    )(x, gmix, w_in_packed, gq, gkv, gkpe, *tabs64, *tabs32)


def _qup_kernel(ql_ref, wt_ref, gn_ref, gr_ref, c_ref, s_ref, qt_ref):
    zt = lax.dot_general(wt_ref[...], ql_ref[...], (((1,), (1,)), ((), ())),
                         preferred_element_type=F32)
    gn, gr, cos, sin = gn_ref[...], gr_ref[...], c_ref[...], s_ref[...]
    half = QK_ROPE // 2
    for hd in range(MLA_HEADS):
        zn = zt[hd * QK_DIM:hd * QK_DIM + QK_NOPE, :]
        inv = lax.rsqrt(jnp.sum(zn * zn, axis=0, keepdims=True) * (1.0 / QK_NOPE) + EPS)
        qt_ref[hd, :QK_NOPE, :] = (zn * (inv * Q_PRESCALE) * gn).astype(BF16)
        zr = zt[hd * QK_DIM + QK_NOPE:(hd + 1) * QK_DIM, :]
        inv = lax.rsqrt(jnp.sum(zr * zr, axis=0, keepdims=True) * (1.0 / QK_ROPE) + EPS)
        z = zr * (inv * Q_PRESCALE) * gr
        swapped = jnp.concatenate([z[half:], z[:half]], axis=0)
        qt_ref[hd, QK_NOPE:, :] = (z * cos + swapped * sin).astype(BF16)


def _q_up(qlat, w_qt, gn, gr, tabs32_t, *, seq_tiles, tm):
    n = qlat.shape[0]
    tab = pl.BlockSpec((QK_ROPE, tm), lambda i: (0, i % seq_tiles))
    return pl.pallas_call(
        _qup_kernel,
        grid=(n // tm,),
        in_specs=[pl.BlockSpec((tm, qlat.shape[1]), lambda i: (i, 0)), _const_spec(w_qt.shape),
                  _const_spec(gn.shape), _const_spec(gr.shape), tab, tab],
        out_specs=pl.BlockSpec((MLA_HEADS, QK_DIM, tm), lambda i: (0, 0, i)),
        out_shape=jax.ShapeDtypeStruct((MLA_HEADS, QK_DIM, n), BF16),
        compiler_params=pltpu.CompilerParams(
            dimension_semantics=("arbitrary",), vmem_limit_bytes=_vmem_limit(40 << 20)),
        name="q_up",
    )(qlat, w_qt, gn, gr, *tabs32_t)


_ONES_ROWS = 16


def _kvup_kernel(c_ref, kpe_ref, w_ref, gk_ref, k_ref, vt_ref):
    c = c_ref[...].astype(BF16)
    kpe = kpe_ref[...].astype(BF16)
    gk = gk_ref[...]
    kw = MLA_HEADS * QK_NOPE
    for pair in range(MLA_HEADS // 2):
        zk = jnp.dot(c, w_ref[:, pair * 256:(pair + 1) * 256], preferred_element_type=F32)
        zv = jnp.dot(c, w_ref[:, kw + pair * 256:kw + (pair + 1) * 256],
                     preferred_element_type=F32)
        for sub in range(2):
            hd = 2 * pair + sub
            zh = zk[:, sub * QK_NOPE:(sub + 1) * QK_NOPE]
            k_ref[hd, :, :QK_NOPE] = (zh * _rms_scale(zh, QK_NOPE) * gk).astype(BF16)
            k_ref[hd, :, QK_NOPE:] = kpe
            vh = zv[:, sub * V_HEAD:(sub + 1) * V_HEAD]
            vt_ref[hd, :V_HEAD, :] = vh.T.astype(BF16)
            vt_ref[hd, V_HEAD:, :] = jnp.ones((_ONES_ROWS, vh.shape[0]), BF16)


def _kv_up(ckv, kpe, w_kv, gk, *, tm):
    n = ckv.shape[0]
    v_spec = pl.BlockSpec((MLA_HEADS, V_HEAD + _ONES_ROWS, tm), lambda i: (0, 0, i))
    v_shape = (MLA_HEADS, V_HEAD + _ONES_ROWS, n)
    return pl.pallas_call(
        _kvup_kernel,
        grid=(n // tm,),
        in_specs=[pl.BlockSpec((tm, ckv.shape[1]), lambda i: (i, 0)),
                  pl.BlockSpec((tm, QK_ROPE), lambda i: (i, 0)),
                  _const_spec(w_kv.shape), _const_spec(gk.shape)],
        out_specs=[pl.BlockSpec((MLA_HEADS, tm, QK_DIM), lambda i: (0, i, 0)), v_spec],
        out_shape=(jax.ShapeDtypeStruct((MLA_HEADS, n, QK_DIM), BF16),
                   jax.ShapeDtypeStruct(v_shape, BF16)),
        compiler_params=pltpu.CompilerParams(
            dimension_semantics=("arbitrary",), vmem_limit_bytes=_vmem_limit(40 << 20)),
        name="kv_up",
    )(ckv, kpe, w_kv, gk)


def _qk(q, k):
    return lax.dot_general(q, k, (((1,), (1,)), ((), ())), preferred_element_type=F32)


_STRIP = 256
_ATTN_KEY_BLOCK = 512


def _prompt_attn_kernel(qt_ref, k_ref, vt_ref, o_ref, s_ref, smax_ref, p_ref, a_ref, m_ref,
                        acc_ref, *, tq, tk):
    i = pl.program_id(2)
    n_strips = tq // _STRIP
    n_diag = tq // tk
    assert n_diag % 2 == 0 and tk % _STRIP == 0 and _STRIP % CHUNK == 0
    all_strips = tuple(range(n_strips))
    cols = lambda c: slice(c * _STRIP, (c + 1) * _STRIP)
    seen = lambda d: tuple(c for c in all_strips if (c + 1) * _STRIP > d * tk)
    unmasked = lambda d, c: c * _STRIP >= (d + 1) * tk

    def scores(b, slot, c):
        start = pl.multiple_of(b * tk, tk)
        s = jnp.dot(k_ref[0, pl.ds(start, tk), :], qt_ref[0, :, cols(c)],
                    preferred_element_type=F32)
        s_ref[slot, :, cols(c)] = s
        smax_ref[slot, :, cols(c)] = jnp.max(s, axis=0, keepdims=True)

    def softmax(slot, c, diag_block=None):
        s = s_ref[slot, :, cols(c)]
        if diag_block is None or unmasked(diag_block, c):
            s_max = smax_ref[slot, :, cols(c)]
        else:
            kc = (lax.broadcasted_iota(jnp.int32, s.shape, 0) + diag_block * tk) // CHUNK
            qc = (lax.broadcasted_iota(jnp.int32, s.shape, 1) + c * _STRIP) // CHUNK
            s = jnp.where(kc <= qc, s, NEG_INF)
            s_max = jnp.max(s, axis=0, keepdims=True)
        m_prev = m_ref[:, cols(c)]
        m_new = jnp.maximum(m_prev, s_max)
        alpha = jnp.exp2(m_prev - m_new)
        p = jnp.exp2(s - m_new)
        m_ref[:, cols(c)] = m_new
        a_ref[slot, :, cols(c)] = alpha
        p_ref[slot, :, cols(c)] = p.astype(BF16)

    def values(b, slot, c):
        start = pl.multiple_of(jnp.maximum(b, 0) * tk, tk)
        pv = jnp.dot(vt_ref[0, :, pl.ds(start, tk)], p_ref[slot, :, cols(c)],
                     preferred_element_type=F32)
        acc_ref[:, cols(c)] = acc_ref[:, cols(c)] * a_ref[slot, :, cols(c)] + pv

    m_ref[...] = jnp.full(m_ref.shape, NEG_INF, F32)
    acc_ref[...] = jnp.zeros(acc_ref.shape, F32)
    p_ref[1] = jnp.zeros(p_ref.shape[1:], BF16)
    a_ref[1] = jnp.ones(a_ref.shape[1:], F32)
    for c in all_strips:
        scores(0, 0, c)

    def pair(jj, carry):
        b = 2 * jj
        for c in all_strips:
            scores(b + 1, 1, c)
        for c in all_strips:
            values(b - 1, 1, c)
        for c in all_strips:
            softmax(0, c)
        for c in all_strips:
            scores(b + 2, 0, c)
        for c in all_strips:
            values(b, 0, c)
        for c in all_strips:
            softmax(1, c)
        return carry

    lax.fori_loop(0, i * (n_diag // 2), pair, 0)
    b0 = n_diag * i
    for t in range(n_diag + 2):
        if 1 <= t + 1 < n_diag:
            for c in seen(t + 1):
                scores(b0 + t + 1, (t + 1) % 2, c)
        if t == 0:
            for c in all_strips:
                values(b0 - 1, 1, c)
        elif t - 1 < n_diag:
            for c in seen(t - 1):
                values(b0 + t - 1, (t - 1) % 2, c)
        if t < n_diag:
            for c in seen(t):
                softmax(t % 2, c, diag_block=t)
    o = acc_ref[:V_HEAD, :] / acc_ref[V_HEAD:V_HEAD + 1, :]
    o_ref[...] = o.T.astype(o_ref.dtype)


def _prompt_attention(qt, k, vt, *, batch, seq, tq):
    heads, _, n = qt.shape
    nq = seq // tq
    tk = _ATTN_KEY_BLOCK
    return pl.pallas_call(
        functools.partial(_prompt_attn_kernel, tq=tq, tk=tk),
        grid=(batch, heads, nq),
        in_specs=[pl.BlockSpec((1, QK_DIM, tq), lambda b, h, i: (h, 0, b * nq + i)),
                  pl.BlockSpec((1, seq, QK_DIM), lambda b, h, i: (h, b, 0)),
                  pl.BlockSpec((1, vt.shape[1], seq), lambda b, h, i: (h, 0, b))],
        out_specs=pl.BlockSpec((tq, V_HEAD), lambda b, h, i: (b * nq + i, h)),
        out_shape=jax.ShapeDtypeStruct((n, heads * V_HEAD), BF16),
        scratch_shapes=[pltpu.VMEM((2, tk, tq), F32), pltpu.VMEM((2, 1, tq), F32),
                        pltpu.VMEM((2, tk, tq), BF16),
                        pltpu.VMEM((2, 1, tq), F32), pltpu.VMEM((1, tq), F32),
                        pltpu.VMEM((vt.shape[1], tq), F32)],
        compiler_params=pltpu.CompilerParams(
            dimension_semantics=("arbitrary", "arbitrary", "arbitrary"),
            vmem_limit_bytes=_vmem_limit(40 << 20)),
        name="prompt_attention",
    )(qt, k, vt)


def _sample_attn_kernel(qbd_ref, qr_ref, latc_ref, kpec_ref, latn_ref, kpen_ref, w_ref, gk_ref,
                        o_ref, *, new):
    gk = gk_ref[...]
    kw = MLA_HEADS * QK_NOPE

    def scores(lat_ref, kpe_ref):
        c = lat_ref[0].astype(BF16)
        k_heads = []
        for pair in range(MLA_HEADS // 2):
            zk = jnp.dot(c, w_ref[:, pair * 256:(pair + 1) * 256], preferred_element_type=F32)
            for sub in range(2):
                zh = zk[:, sub * QK_NOPE:(sub + 1) * QK_NOPE]
                k_heads.append((zh * _rms_scale(zh, QK_NOPE) * gk).astype(BF16))
        k_all = jnp.concatenate(k_heads, axis=1)
        s = (jnp.dot(k_all, qbd_ref[0], preferred_element_type=F32)
             + jnp.dot(kpe_ref[0].astype(BF16), qr_ref[0], preferred_element_type=F32))
        return s, c

    s_c, c_c = scores(latc_ref, kpec_ref)
    s_n, c_n = scores(latn_ref, kpen_ref)
    m = jnp.maximum(jnp.max(s_c, axis=0, keepdims=True), jnp.max(s_n, axis=0, keepdims=True))
    p_c = jnp.exp2(s_c - m).astype(BF16)
    p_n = jnp.exp2(s_n - m).astype(BF16)
    tn = (((0,), (0,)), ((), ()))
    ctx = (lax.dot_general(p_c, c_c, tn, preferred_element_type=F32)
           + lax.dot_general(p_n, c_n, tn, preferred_element_type=F32))
    den = (lax.dot_general(p_c, jnp.ones((p_c.shape[0], V_HEAD), BF16), tn,
                           preferred_element_type=F32)
           + lax.dot_general(p_n, jnp.ones((p_n.shape[0], V_HEAD), BF16), tn,
                             preferred_element_type=F32))
    ctx = ctx.astype(BF16)
    for hd in range(MLA_HEADS):
        rows = slice(hd * new, (hd + 1) * new)
        o = jnp.dot(ctx[rows, :], w_ref[:, kw + hd * V_HEAD:kw + (hd + 1) * V_HEAD],
                    preferred_element_type=F32)
        o_ref[0, :, hd * V_HEAD:(hd + 1) * V_HEAD] = (o / den[rows, :]).astype(o_ref.dtype)


def _sample_attention(qt, lat_c, kpe_c, lat_n, kpe_n, w_kv, gk, *, batch, new):
    heads = qt.shape[0]
    q4 = qt.reshape(heads, QK_DIM, batch, new)
    qn = jnp.transpose(q4[:, :QK_NOPE], (2, 0, 1, 3))
    qbd = jnp.einsum('bhdj,hg->bhdgj', qn, jnp.eye(heads, dtype=qn.dtype))
    qbd = qbd.reshape(batch, heads * QK_NOPE, heads * new)
    qr = jnp.transpose(q4[:, QK_NOPE:], (2, 1, 0, 3)).reshape(batch, QK_ROPE, heads * new)
    per_batch = lambda a: pl.BlockSpec((1,) + a.shape[1:], lambda b: (b, 0, 0))
    return pl.pallas_call(
        functools.partial(_sample_attn_kernel, new=new),
        grid=(batch,),
        in_specs=[per_batch(qbd), per_batch(qr), per_batch(lat_c), per_batch(kpe_c),
                  per_batch(lat_n), per_batch(kpe_n), _const_spec(w_kv.shape),
                  _const_spec(gk.shape)],
        out_specs=pl.BlockSpec((1, new, heads * V_HEAD), lambda b: (b, 0, 0)),
        out_shape=jax.ShapeDtypeStruct((batch, new, heads * V_HEAD), BF16),
        compiler_params=pltpu.CompilerParams(
            dimension_semantics=("arbitrary",), vmem_limit_bytes=_vmem_limit(48 << 20)),
        name="sample_attention",
    )(qbd, qr, lat_c, kpe_c, lat_n, kpe_n, w_kv, gk)


def _retention_kernel(q_ref, k_ref, v_ref, rg_ref, dec_ref, qd_ref, kd_ref, g_ref, s0_ref,
                      o_ref, sfin_ref, state_ref, *, chunk, n_chunks):
    c = pl.program_id(1)

    @pl.when(c == 0)
    def _():
        state_ref[...] = s0_ref[0]

    heads = range(RET_HEADS)
    kcols = [slice(hd * RET_DK, (hd + 1) * RET_DK) for hd in heads]
    vcols = [slice(hd * RET_DV, (hd + 1) * RET_DV) for hd in heads]
    q = [q_ref[:, kcols[hd]] for hd in heads]
    k = [k_ref[:, kcols[hd]] for hd in heads]
    v = [v_ref[:, vcols[hd]] for hd in heads]
    state = [state_ref[hd] for hd in heads]
    qd = [qd_ref[hd] for hd in heads]
    s = [_qk(q[hd], k[hd]) for hd in heads]
    cross = [jnp.dot(q[hd], state[hd].astype(BF16), preferred_element_type=F32) for hd in heads]
    s = [(s[hd] * dec_ref[hd]).astype(BF16) for hd in heads]
    k_dec = [(k[hd].astype(F32) * kd_ref[hd]).astype(BF16) for hd in heads]
    inner = [jnp.dot(s[hd], v[hd], preferred_element_type=F32) for hd in heads]
    kv = [lax.dot_general(k_dec[hd], v[hd], (((0,), (0,)), ((), ())),
                          preferred_element_type=F32) for hd in heads]
    for hd in heads:
        state_ref[hd] = state[hd] * qd[hd][chunk - 1:chunk, :] + kv[hd]
    for hd in heads:
        o = inner[hd] + cross[hd] * qd[hd]
        mu = jnp.mean(o, axis=-1, keepdims=True)
        oc = o - mu
        var = jnp.mean(oc * oc, axis=-1, keepdims=True)
        y = oc * lax.rsqrt(var + EPS) * g_ref[hd]
        rg = rg_ref[:, vcols[hd]]
        o_ref[:, vcols[hd]] = (rg * jax.nn.sigmoid(rg) * y).astype(o_ref.dtype)

    @pl.when(c == n_chunks - 1)
    def _():
        sfin_ref[0] = state_ref[...]


def _retention_tables(chunk):
    lg = jnp.log1p(-(2.0 ** (-5.0 - jnp.arange(RET_HEADS, dtype=F32))))
    idx = jnp.arange(chunk, dtype=F32)
    diff = idx[:, None] - idx[None, :]
    decay = jnp.where(diff[None] >= 0,
                      jnp.exp(jnp.maximum(diff, 0.0)[None] * lg[:, None, None]), 0.0)
    q_dec = jnp.exp((idx + 1.0)[None, :] * lg[:, None])[..., None]
    k_dec = jnp.exp((chunk - 1.0 - idx)[None, :] * lg[:, None])[..., None]
    return decay, q_dec, k_dec


def _retention(rq, rk, rv, rg, g_ret, s0, *, batch, seq, chunk):
    n = rq.shape[0]
    nc = seq // chunk
    decay, q_dec, k_dec = _retention_tables(chunk)
    tok = lambda w: pl.BlockSpec((chunk, w), lambda b, c: (b * nc + c, 0))
    state_spec = pl.BlockSpec((1, RET_HEADS, RET_DK, RET_DV), lambda b, c: (b, 0, 0, 0))
    g_ret = g_ret.reshape(RET_HEADS, 1, RET_DV)
    return pl.pallas_call(
        functools.partial(_retention_kernel, chunk=chunk, n_chunks=nc),
        grid=(batch, nc),
        in_specs=[tok(RET_HEADS * RET_DK), tok(RET_HEADS * RET_DK), tok(RET_HEADS * RET_DV),
                  tok(RET_HEADS * RET_DV), _const_spec(decay.shape), _const_spec(q_dec.shape),
                  _const_spec(k_dec.shape), _const_spec(g_ret.shape), state_spec],
        out_specs=[tok(RET_HEADS * RET_DV), state_spec],
        out_shape=(jax.ShapeDtypeStruct((n, RET_HEADS * RET_DV), BF16),
                   jax.ShapeDtypeStruct((batch, RET_HEADS, RET_DK, RET_DV), F32)),
        scratch_shapes=[pltpu.VMEM((RET_HEADS, RET_DK, RET_DV), F32)],
        compiler_params=pltpu.CompilerParams(dimension_semantics=("arbitrary", "arbitrary")),
        name="retention",
    )(rq, rk, rv, rg, decay, q_dec, k_dec, g_ret, s0)


def _merge_kernel(x_ref, a_ref, r_ref, ga_ref, gb_ref, wa_ref, wb_ref, wo_ref, o_ref):
    a_d = jnp.dot(a_ref[...], wa_ref[...], preferred_element_type=F32)
    r_d = jnp.dot(r_ref[...], wb_ref[...], preferred_element_type=F32)
    merged = jax.nn.sigmoid(ga_ref[...]) * a_d + jax.nn.sigmoid(gb_ref[...]) * r_d
    o_ref[...] = x_ref[...] + jnp.dot(merged.astype(BF16), wo_ref[...],
                                      preferred_element_type=F32)


def _merge(x, attn, ret, ga, gb, wa, wb, wo, *, tm):
    n, d = x.shape
    row = pl.BlockSpec((tm, d), lambda i: (i, 0))
    return pl.pallas_call(
        _merge_kernel,
        grid=(n // tm,),
        in_specs=[row, row, row, row, row,
                  _const_spec(wa.shape), _const_spec(wb.shape), _const_spec(wo.shape)],
        out_specs=row,
        out_shape=jax.ShapeDtypeStruct((n, d), F32),
        compiler_params=pltpu.CompilerParams(
            dimension_semantics=("arbitrary",), vmem_limit_bytes=_vmem_limit(40 << 20)),
        name="merge",
    )(x, attn, ret, ga, gb, wa, wb, wo)


_PAD_ROWS = 8


def _ffn_kernel(x_ref, g_ref, wup_ref, cw_ref, cb_ref, wdn_ref, prev_ref,
                y_ref, st_ref, ubuf_ref, *, tm, d_ff, n_tiles):
    j = pl.program_id(1)
    lo = _PAD_ROWS - (CONV_W - 1)

    @pl.when(j == 0)
    def _():
        ubuf_ref[0:_PAD_ROWS, :] = jnp.zeros((_PAD_ROWS, ubuf_ref.shape[1]), F32)
        ubuf_ref[lo:_PAD_ROWS, :] = prev_ref[0]

    @pl.when(j > 0)
    def _():
        ubuf_ref[0:_PAD_ROWS, :] = ubuf_ref[tm:tm + _PAD_ROWS, :]

    xf = x_ref[...]
    hn = (xf * _rms_scale(xf, xf.shape[-1]) * g_ref[...]).astype(BF16)
    ubuf_ref[_PAD_ROWS:_PAD_ROWS + tm, :] = jnp.dot(hn, wup_ref[...],
                                                    preferred_element_type=F32)
    c = cb_ref[...]
    for tap in range(CONV_W):
        c = c + ubuf_ref[lo + tap:lo + tap + tm, :] * cw_ref[tap:tap + 1, :]
    gate, val = c[:, :d_ff], c[:, d_ff:]
    act = (gate * jax.nn.sigmoid(gate) * val).astype(BF16)
    y_ref[...] = xf + jnp.dot(act, wdn_ref[...], preferred_element_type=F32)

    @pl.when(j == n_tiles - 1)
    def _():
        st_ref[0] = ubuf_ref[tm + lo:tm + _PAD_ROWS, :]


def _conv_ffn(x, g, w_up, cw, cb, w_dn, prev, *, batch, seq, tm):
    n, d = x.shape
    d_ff = w_dn.shape[0]
    nt = seq // tm
    row = pl.BlockSpec((tm, d), lambda b, j: (b * nt + j, 0))
    st = pl.BlockSpec((1, CONV_W - 1, 2 * d_ff), lambda b, j: (b, 0, 0))
    return pl.pallas_call(
        functools.partial(_ffn_kernel, tm=tm, d_ff=d_ff, n_tiles=nt),
        grid=(batch, nt),
        in_specs=[row, _const_spec(g.shape), _const_spec(w_up.shape), _const_spec(cw.shape),
                  _const_spec(cb.shape), _const_spec(w_dn.shape), st],
        out_specs=[row, st],
        out_shape=(jax.ShapeDtypeStruct((n, d), F32),
                   jax.ShapeDtypeStruct((batch, CONV_W - 1, 2 * d_ff), F32)),
        scratch_shapes=[pltpu.VMEM((tm + _PAD_ROWS, 2 * d_ff), F32)],
        compiler_params=pltpu.CompilerParams(
            dimension_semantics=("arbitrary", "arbitrary"),
            vmem_limit_bytes=_vmem_limit(56 << 20)),
        name="conv_ffn",
    )(x, g, w_up, cw, cb, w_dn, prev)


def _pack_weights(w_in, w_q_up, w_kv_up, w_o_branch, w_out, w_ffn_up, w_ffn_down):
    sizes = (768, 512, 64, 512, 512, 1024, 1024, 1024, 1024)
    offs = np.concatenate([[0], np.cumsum(sizes)])
    parts = []
    for (name, width), a, b in zip(_IN_GROUPS, offs[:-1], offs[1:]):
        part = w_in[:, a:b]
        if b - a < width:
            part = jnp.pad(part, ((0, 0), (0, width - (b - a))))
        parts.append(part)
    w_in_p = jnp.concatenate(parts, axis=1).astype(BF16)
    w_q_p = w_q_up.T.astype(BF16)
    wkv = w_kv_up.reshape(w_kv_up.shape[0], MLA_HEADS, QK_NOPE + V_HEAD)
    w_kv_p = jnp.concatenate([wkv[:, :, :QK_NOPE].reshape(wkv.shape[0], -1),
                              wkv[:, :, QK_NOPE:].reshape(wkv.shape[0], -1)], axis=1).astype(BF16)
    mla_width = MLA_HEADS * V_HEAD
    return (w_in_p, w_q_p, w_kv_p, w_o_branch[:mla_width].astype(BF16),
            w_o_branch[mla_width:].astype(BF16), w_out.astype(BF16),
            w_ffn_up.astype(BF16), w_ffn_down.astype(BF16))


def _layer(x, pos, past, w, *, batch, seq, tm, tq, ret_chunk, ffn_tm):
    n = x.shape[0]
    tabs64 = _rope_tables(pos, RET_DK // 2)
    tabs32 = _rope_tables(pos, QK_ROPE // 2)
    seq_tiles = max(seq // tm, 1)
    if seq < tm:
        tabs64 = tuple(jnp.tile(t, (tm // seq, 1)) for t in tabs64)
        tabs32 = tuple(jnp.tile(t, (tm // seq, 1)) for t in tabs32)
    qlat, ckv, kpe, rq, rk, rv, rg, ga, gb = _in_proj(
        x, w['w_in'], w['g_norm_mix'], w['g_q_lat'], w['g_kv_lat'], w['g_k_rope'],
        tabs64, tabs32, seq_tiles=seq_tiles, tm=tm)
    tabs32_t = tuple(t[:, :QK_ROPE].T for t in tabs32)
    bcast = lambda g: jnp.broadcast_to(g.reshape(-1, 1), (g.size, tm))
    qt = _q_up(qlat, w['w_q_up'], bcast(w['g_q_nope']), bcast(w['g_q_rope']), tabs32_t,
               seq_tiles=seq_tiles, tm=tm)
    if past is None:
        k_new, vt_new = _kv_up(ckv, kpe, w['w_kv_up'], w['g_k_nope'], tm=tm)
        attn = _prompt_attention(qt, k_new, vt_new, batch=batch, seq=seq, tq=tq)
        s0 = jnp.zeros((batch, RET_HEADS, RET_DK, RET_DV), F32)
        prev = jnp.zeros((batch, CONV_W - 1, w['w_ffn_up'].shape[1]), F32)
    else:
        lat_c, pe_c, s0, prev = past
        attn = _sample_attention(qt, lat_c, pe_c, ckv.reshape(batch, seq, -1),
                                 kpe.reshape(batch, seq, -1), w['w_kv_up'], w['g_k_nope'],
                                 batch=batch, new=seq).reshape(n, -1)
    ret, s_fin = _retention(rq, rk, rv, rg, w['g_ret_out'], s0,
                            batch=batch, seq=seq, chunk=ret_chunk)
    x1 = _merge(x, attn, ret, ga, gb, w['w_o_a'], w['w_o_b'], w['w_out'], tm=tm)
    y, conv_state = _conv_ffn(x1, w['g_norm_ffn'], w['w_ffn_up'], w['ffn_conv_w'],
                              w['ffn_conv_b'], w['w_ffn_down'], prev,
                              batch=batch, seq=seq, tm=ffn_tm)
    return y, (ckv, kpe, s_fin, conv_state)


def kernel(x_prompt, x_sample, cache_mla_latent, cache_mla_rope_key, state_retention, state_ffn_conv, g_norm_mix, w_in, g_q_lat, w_q_up, g_q_nope, g_q_rope, g_kv_lat, w_kv_up, g_k_nope, g_k_rope, g_ret_out, w_o_branch, w_out, g_norm_ffn, w_ffn_up, ffn_conv_w, ffn_conv_b, w_ffn_down):
    depth = w_in.shape[0]
    assert depth == 1, "single-layer trunk"
    bp, tp, d = x_prompt.shape
    bs, ts, _ = x_sample.shape
    past_len = cache_mla_latent.shape[2]
    (w_in_p, w_q_p, w_kv_p, w_o_a, w_o_b, w_out_p, w_up_p, w_dn_p) = _pack_weights(
        w_in[0], w_q_up[0], w_kv_up[0], w_o_branch[0], w_out[0], w_ffn_up[0], w_ffn_down[0])
    row = lambda g: g.reshape(1, -1).astype(F32)
    w = dict(
        w_in=w_in_p, w_q_up=w_q_p, w_kv_up=w_kv_p, w_o_a=w_o_a, w_o_b=w_o_b, w_out=w_out_p,
        w_ffn_up=w_up_p, w_ffn_down=w_dn_p,
        g_norm_mix=row(g_norm_mix[0]), g_q_lat=row(g_q_lat[0]), g_kv_lat=row(g_kv_lat[0]),
        g_k_rope=jnp.pad(row(g_k_rope[0]), ((0, 0), (0, LANES - QK_ROPE))),
        g_q_nope=g_q_nope[0].astype(F32), g_q_rope=g_q_rope[0].astype(F32),
        g_k_nope=row(g_k_nope[0]), g_ret_out=g_ret_out[0], g_norm_ffn=row(g_norm_ffn[0]),
        ffn_conv_w=ffn_conv_w[0], ffn_conv_b=row(ffn_conv_b[0]))

    yp, sp = _layer(x_prompt.reshape(bp * tp, d), jnp.arange(tp), None, w,
                    batch=bp, seq=tp, tm=512, tq=2048, ret_chunk=256, ffn_tm=256)
    past = (cache_mla_latent[0], cache_mla_rope_key[0], state_retention[0], state_ffn_conv[0])
    ys, ss = _layer(x_sample.reshape(bs * ts, d), past_len + jnp.arange(ts), past, w,
                    batch=bs, seq=ts, tm=bs * ts, tq=None, ret_chunk=ts, ffn_tm=ts)

    def states(st, b, t):
        ckv, kpe, s_fin, conv = st
        return (ckv.reshape(1, b, t, -1), kpe.reshape(1, b, t, -1), s_fin[None], conv[None])

    return (yp.reshape(bp, tp, d), ys.reshape(bs, ts, d)) + states(sp, bp, tp) + states(ss, bs, ts)
```

```python
import functools
import math

import jax
import jax.numpy as jnp
import numpy as np
from jax import lax
from jax.experimental import pallas as pl
from jax.experimental.pallas import tpu as pltpu

F32 = jnp.float32
BF16 = jnp.bfloat16

EPS = 1e-6
NEG_INF = -1e30
ROPE_THETA = 10000.0
CHUNK = 64
LANES = 128
V7X_VMEM_BYTES = 64 * 1024 * 1024

MLA_HEADS = 8
QK_NOPE = 128
QK_ROPE = 64
QK_DIM = QK_NOPE + QK_ROPE
V_HEAD = 128
MLA_SCALE = QK_DIM ** -0.5
Q_PRESCALE = MLA_SCALE * math.log2(math.e)
RET_HEADS = 4
RET_DK = 128
RET_DV = 256
CONV_W = 3


def _vmem_limit(nbytes):
    return int(min(V7X_VMEM_BYTES - (4 << 20), max(32 << 20, nbytes)))


def _const_spec(shape):
    nd = len(shape)
    return pl.BlockSpec(shape, lambda *_: (0,) * nd, pipeline_mode=pl.Buffered(1))


def _rms_scale(z, width):
    return lax.rsqrt(jnp.sum(z * z, axis=-1, keepdims=True) * (1.0 / width) + EPS)


def _swap_halves_64(z):
    lane = lax.broadcasted_iota(jnp.int32, z.shape, 1)
    return jnp.where((lane % 64) < 32, pltpu.roll(z, 96, 1), pltpu.roll(z, 32, 1))


def _rope_tables(pos, half):
    freqs = ROPE_THETA ** (-np.arange(half, dtype=np.float64) / half)
    ang = pos.astype(np.float64)[:, None] * freqs[None, :]
    cos, sin = np.cos(ang), np.sin(ang)
    reps = LANES // (2 * half)
    return (np.tile(np.concatenate([cos, cos], -1), (1, reps)).astype(np.float32),
            np.tile(np.concatenate([-sin, sin], -1), (1, reps)).astype(np.float32))


_IN_HEAD = (('qlat', 768), ('kvl', 512), ('kpe', LANES))
_IN_TAIL = (('rq', 512), ('rk', 512), ('rv', 1024), ('rg', 1024), ('ga', 1024), ('gb', 1024))
_IN_HEAD_SRC_WIDTH = 768 + 512 + QK_ROPE


def _group_offsets(groups):
    offs, o = {}, 0
    for name, width in groups:
        offs[name] = (o, o + width)
        o += width
    return offs


_IN_OFFS = {**{n: (0,) + ab for n, ab in _group_offsets(_IN_HEAD).items()},
            **{n: (1,) + ab for n, ab in _group_offsets(_IN_TAIL).items()}}


def _inproj_kernel(x_ref, gmix_ref, wh_ref, wt_ref, gq_ref, gkv_ref, gkpe_ref,
                   c64_ref, s64_ref, c32_ref, s32_ref,
                   qlat_ref, ckv_ref, kpe_ref, rq_ref, rk_ref, rv_ref, rg_ref, ga_ref, gb_ref):
    xf = x_ref[...]
    h = (xf * _rms_scale(xf, xf.shape[-1]) * gmix_ref[...]).astype(BF16)

    def proj(name):
        part, a, b = _IN_OFFS[name]
        w_ref = (wh_ref, wt_ref)[part]
        return jnp.dot(h, w_ref[:, a:b], preferred_element_type=F32)

    z = proj('qlat')
    qlat_ref[...] = (z * _rms_scale(z, z.shape[-1]) * gq_ref[...]).astype(BF16)
    z = proj('kvl')
    ckv_ref[...] = z * _rms_scale(z, z.shape[-1]) * gkv_ref[...]
    z = proj('kpe')
    zn = z * _rms_scale(z, QK_ROPE) * gkpe_ref[...]
    kpe = zn * c32_ref[...] + _swap_halves_64(zn) * s32_ref[...]
    kpe_ref[...] = kpe[:, :QK_ROPE]
    c64, s64 = c64_ref[...], s64_ref[...]
    for name, ref, scale in (('rq', rq_ref, 1.0), ('rk', rk_ref, RET_DK ** -0.5)):
        z = proj(name)
        for hd in range(RET_HEADS):
            zh = z[:, hd * RET_DK:(hd + 1) * RET_DK]
            r = zh * c64 + pltpu.roll(zh, RET_DK // 2, 1) * s64
            if scale != 1.0:
                r = r * scale
            ref[:, hd * RET_DK:(hd + 1) * RET_DK] = r.astype(BF16)
    rv_ref[...] = proj('rv').astype(BF16)
    rg_ref[...] = proj('rg')
    ga_ref[...] = proj('ga')
    gb_ref[...] = proj('gb')


def _in_proj(x, w_in_parts, gmix, gq, gkv, gkpe, tabs64, tabs32, *, seq_tiles, tm):
    n, d = x.shape
    w_head, w_tail = w_in_parts
    row = lambda w: pl.BlockSpec((tm, w), lambda i: (i, 0))
    tab = pl.BlockSpec((tm, LANES), lambda i: (i % seq_tiles, 0))
    out_shapes = (
        jax.ShapeDtypeStruct((n, 768), BF16), jax.ShapeDtypeStruct((n, 512), F32),
        jax.ShapeDtypeStruct((n, QK_ROPE), F32), jax.ShapeDtypeStruct((n, 512), BF16),
        jax.ShapeDtypeStruct((n, 512), BF16), jax.ShapeDtypeStruct((n, 1024), BF16),
        jax.ShapeDtypeStruct((n, 1024), F32), jax.ShapeDtypeStruct((n, 1024), F32),
        jax.ShapeDtypeStruct((n, 1024), F32))
    return pl.pallas_call(
        _inproj_kernel,
        grid=(n // tm,),
        in_specs=[row(d), _const_spec(gmix.shape), _const_spec(w_head.shape),
                  _const_spec(w_tail.shape), _const_spec(gq.shape), _const_spec(gkv.shape),
                  _const_spec(gkpe.shape), tab, tab, tab, tab],
        out_specs=[row(s.shape[1]) for s in out_shapes],
        out_shape=out_shapes,
        compiler_params=pltpu.CompilerParams(
            dimension_semantics=("arbitrary",), vmem_limit_bytes=_vmem_limit(56 << 20)),
        name="in_proj",
    )(x, gmix, w_head, w_tail, gq, gkv, gkpe, *tabs64, *tabs32)


def _qup_kernel(ql_ref, wt_ref, gn_ref, gr_ref, c_ref, s_ref, qt_ref):
    zt = lax.dot_general(wt_ref[...], ql_ref[...], (((1,), (1,)), ((), ())),
                         preferred_element_type=F32)
    gn, gr, cos, sin = gn_ref[...], gr_ref[...], c_ref[...], s_ref[...]
    half = QK_ROPE // 2
    for hd in range(MLA_HEADS):
        zn = zt[hd * QK_DIM:hd * QK_DIM + QK_NOPE, :]
        inv = lax.rsqrt(jnp.sum(zn * zn, axis=0, keepdims=True) * (1.0 / QK_NOPE) + EPS)
        qt_ref[hd, :QK_NOPE, :] = (zn * (inv * Q_PRESCALE) * gn).astype(BF16)
        zr = zt[hd * QK_DIM + QK_NOPE:(hd + 1) * QK_DIM, :]
        inv = lax.rsqrt(jnp.sum(zr * zr, axis=0, keepdims=True) * (1.0 / QK_ROPE) + EPS)
        z = zr * (inv * Q_PRESCALE) * gr
        swapped = jnp.concatenate([z[half:], z[:half]], axis=0)
        qt_ref[hd, QK_NOPE:, :] = (z * cos + swapped * sin).astype(BF16)


def _q_up(qlat, w_qt, gn, gr, tabs32_t, *, seq_tiles, tm):
    n = qlat.shape[0]
    tab = pl.BlockSpec((QK_ROPE, tm), lambda i: (0, i % seq_tiles))
    return pl.pallas_call(
        _qup_kernel,
        grid=(n // tm,),
        in_specs=[pl.BlockSpec((tm, qlat.shape[1]), lambda i: (i, 0)), _const_spec(w_qt.shape),
                  _const_spec(gn.shape), _const_spec(gr.shape), tab, tab],
        out_specs=pl.BlockSpec((MLA_HEADS, QK_DIM, tm), lambda i: (0, 0, i)),
        out_shape=jax.ShapeDtypeStruct((MLA_HEADS, QK_DIM, n), BF16),
        compiler_params=pltpu.CompilerParams(
            dimension_semantics=("arbitrary",), vmem_limit_bytes=_vmem_limit(40 << 20)),
        name="q_up",
    )(qlat, w_qt, gn, gr, *tabs32_t)


_ONES_ROWS = 16


def _kvup_kernel(c_ref, kpe_ref, w_ref, gk_ref, k_ref, vt_ref):
    c = c_ref[...].astype(BF16)
    kpe = kpe_ref[...].astype(BF16)
    gk = gk_ref[...]
    kw = MLA_HEADS * QK_NOPE
    for pair in range(MLA_HEADS // 2):
        zk = jnp.dot(c, w_ref[:, pair * 256:(pair + 1) * 256], preferred_element_type=F32)
        zv = jnp.dot(c, w_ref[:, kw + pair * 256:kw + (pair + 1) * 256],
                     preferred_element_type=F32)
        for sub in range(2):
            hd = 2 * pair + sub
            zh = zk[:, sub * QK_NOPE:(sub + 1) * QK_NOPE]
            k_ref[hd, :, :QK_NOPE] = (zh * _rms_scale(zh, QK_NOPE) * gk).astype(BF16)
            k_ref[hd, :, QK_NOPE:] = kpe
            vh = zv[:, sub * V_HEAD:(sub + 1) * V_HEAD]
            vt_ref[hd, :V_HEAD, :] = vh.T.astype(BF16)
            vt_ref[hd, V_HEAD:, :] = jnp.ones((_ONES_ROWS, vh.shape[0]), BF16)


def _kv_up(ckv, kpe, w_kv, gk, *, tm):
    n = ckv.shape[0]
    v_spec = pl.BlockSpec((MLA_HEADS, V_HEAD + _ONES_ROWS, tm), lambda i: (0, 0, i))
    v_shape = (MLA_HEADS, V_HEAD + _ONES_ROWS, n)
    return pl.pallas_call(
        _kvup_kernel,
        grid=(n // tm,),
        in_specs=[pl.BlockSpec((tm, ckv.shape[1]), lambda i: (i, 0)),
                  pl.BlockSpec((tm, QK_ROPE), lambda i: (i, 0)),
                  _const_spec(w_kv.shape), _const_spec(gk.shape)],
        out_specs=[pl.BlockSpec((MLA_HEADS, tm, QK_DIM), lambda i: (0, i, 0)), v_spec],
        out_shape=(jax.ShapeDtypeStruct((MLA_HEADS, n, QK_DIM), BF16),
                   jax.ShapeDtypeStruct(v_shape, BF16)),
        compiler_params=pltpu.CompilerParams(
            dimension_semantics=("arbitrary",), vmem_limit_bytes=_vmem_limit(40 << 20)),
        name="kv_up",
    )(ckv, kpe, w_kv, gk)


def _qk(q, k):
    return lax.dot_general(q, k, (((1,), (1,)), ((), ())), preferred_element_type=F32)


_STRIP = 256
_ATTN_KEY_BLOCK = 512


def _prompt_attn_kernel(qt_ref, k_ref, vt_ref, o_ref, s_ref, smax_ref, p_ref, a_ref, m_ref,
                        acc_ref, *, tq, tk):
    i = pl.program_id(2)
    n_strips = tq // _STRIP
    n_diag = tq // tk
    assert n_diag % 2 == 0 and tk % _STRIP == 0 and _STRIP % CHUNK == 0
    all_strips = tuple(range(n_strips))
    cols = lambda c: slice(c * _STRIP, (c + 1) * _STRIP)
    seen = lambda d: tuple(c for c in all_strips if (c + 1) * _STRIP > d * tk)
    unmasked = lambda d, c: c * _STRIP >= (d + 1) * tk

    def scores(b, slot, c):
        start = pl.multiple_of(b * tk, tk)
        s = jnp.dot(k_ref[0, pl.ds(start, tk), :], qt_ref[0, :, cols(c)],
                    preferred_element_type=F32)
        s_ref[slot, :, cols(c)] = s
        smax_ref[slot, :, cols(c)] = jnp.max(s, axis=0, keepdims=True)

    def softmax(slot, c, diag_block=None):
        s = s_ref[slot, :, cols(c)]
        if diag_block is None or unmasked(diag_block, c):
            s_max = smax_ref[slot, :, cols(c)]
        else:
            kc = (lax.broadcasted_iota(jnp.int32, s.shape, 0) + diag_block * tk) // CHUNK
            qc = (lax.broadcasted_iota(jnp.int32, s.shape, 1) + c * _STRIP) // CHUNK
            s = jnp.where(kc <= qc, s, NEG_INF)
            s_max = jnp.max(s, axis=0, keepdims=True)
        m_prev = m_ref[:, cols(c)]
        m_new = jnp.maximum(m_prev, s_max)
        alpha = jnp.exp2(m_prev - m_new)
        p = jnp.exp2(s - m_new)
        m_ref[:, cols(c)] = m_new
        a_ref[slot, :, cols(c)] = alpha
        p_ref[slot, :, cols(c)] = p.astype(BF16)

    def values(b, slot, c):
        start = pl.multiple_of(jnp.maximum(b, 0) * tk, tk)
        pv = jnp.dot(vt_ref[0, :, pl.ds(start, tk)], p_ref[slot, :, cols(c)],
                     preferred_element_type=F32)
        acc_ref[:, cols(c)] = acc_ref[:, cols(c)] * a_ref[slot, :, cols(c)] + pv

    m_ref[...] = jnp.full(m_ref.shape, NEG_INF, F32)
    acc_ref[...] = jnp.zeros(acc_ref.shape, F32)
    p_ref[1] = jnp.zeros(p_ref.shape[1:], BF16)
    a_ref[1] = jnp.ones(a_ref.shape[1:], F32)
    for c in all_strips:
        scores(0, 0, c)

    def pair(jj, carry):
        b = 2 * jj
        for c in all_strips:
            scores(b + 1, 1, c)
        for c in all_strips:
            values(b - 1, 1, c)
        for c in all_strips:
            softmax(0, c)
        for c in all_strips:
            scores(b + 2, 0, c)
        for c in all_strips:
            values(b, 0, c)
        for c in all_strips:
            softmax(1, c)
        return carry

    lax.fori_loop(0, i * (n_diag // 2), pair, 0)
    b0 = n_diag * i
    for t in range(n_diag + 2):
        if 1 <= t + 1 < n_diag:
            for c in seen(t + 1):
                scores(b0 + t + 1, (t + 1) % 2, c)
        if t == 0:
            for c in all_strips:
                values(b0 - 1, 1, c)
        elif t - 1 < n_diag:
            for c in seen(t - 1):
                values(b0 + t - 1, (t - 1) % 2, c)
        if t < n_diag:
            for c in seen(t):
                softmax(t % 2, c, diag_block=t)
    o = acc_ref[:V_HEAD, :] / acc_ref[V_HEAD:V_HEAD + 1, :]
    o_ref[...] = o.T.astype(o_ref.dtype)


def _prompt_attention(qt, k, vt, *, batch, seq, tq):
    heads, _, n = qt.shape
    nq = seq // tq
    tk = _ATTN_KEY_BLOCK
    return pl.pallas_call(
        functools.partial(_prompt_attn_kernel, tq=tq, tk=tk),
        grid=(batch, heads, nq),
        in_specs=[pl.BlockSpec((1, QK_DIM, tq), lambda b, h, i: (h, 0, b * nq + i)),
                  pl.BlockSpec((1, seq, QK_DIM), lambda b, h, i: (h, b, 0)),
                  pl.BlockSpec((1, vt.shape[1], seq), lambda b, h, i: (h, 0, b))],
        out_specs=pl.BlockSpec((tq, V_HEAD), lambda b, h, i: (b * nq + i, h)),
        out_shape=jax.ShapeDtypeStruct((n, heads * V_HEAD), BF16),
        scratch_shapes=[pltpu.VMEM((2, tk, tq), F32), pltpu.VMEM((2, 1, tq), F32),
                        pltpu.VMEM((2, tk, tq), BF16),
                        pltpu.VMEM((2, 1, tq), F32), pltpu.VMEM((1, tq), F32),
                        pltpu.VMEM((vt.shape[1], tq), F32)],
        compiler_params=pltpu.CompilerParams(
            dimension_semantics=("arbitrary", "arbitrary", "arbitrary"),
            vmem_limit_bytes=_vmem_limit(40 << 20)),
        name="prompt_attention",
    )(qt, k, vt)


def _sample_attn_kernel(qbd_ref, qr_ref, latc_ref, kpec_ref, latn_ref, kpen_ref, w_ref, gk_ref,
                        o_ref, *, new):
    gk = gk_ref[...]
    kw = MLA_HEADS * QK_NOPE

    def scores(lat_ref, kpe_ref):
        c = lat_ref[0].astype(BF16)
        k_heads = []
        for pair in range(MLA_HEADS // 2):
            zk = jnp.dot(c, w_ref[:, pair * 256:(pair + 1) * 256], preferred_element_type=F32)
            for sub in range(2):
                zh = zk[:, sub * QK_NOPE:(sub + 1) * QK_NOPE]
                k_heads.append((zh * _rms_scale(zh, QK_NOPE) * gk).astype(BF16))
        k_all = jnp.concatenate(k_heads, axis=1)
        s = (jnp.dot(k_all, qbd_ref[0], preferred_element_type=F32)
             + jnp.dot(kpe_ref[0].astype(BF16), qr_ref[0], preferred_element_type=F32))
        return s, c

    s_c, c_c = scores(latc_ref, kpec_ref)
    s_n, c_n = scores(latn_ref, kpen_ref)
    m = jnp.maximum(jnp.max(s_c, axis=0, keepdims=True), jnp.max(s_n, axis=0, keepdims=True))
    p_c = jnp.exp2(s_c - m).astype(BF16)
    p_n = jnp.exp2(s_n - m).astype(BF16)
    tn = (((0,), (0,)), ((), ()))
    ctx = (lax.dot_general(p_c, c_c, tn, preferred_element_type=F32)
           + lax.dot_general(p_n, c_n, tn, preferred_element_type=F32))
    den = (lax.dot_general(p_c, jnp.ones((p_c.shape[0], V_HEAD), BF16), tn,
                           preferred_element_type=F32)
           + lax.dot_general(p_n, jnp.ones((p_n.shape[0], V_HEAD), BF16), tn,
                             preferred_element_type=F32))
    ctx = ctx.astype(BF16)
    for hd in range(MLA_HEADS):
        rows = slice(hd * new, (hd + 1) * new)
        o = jnp.dot(ctx[rows, :], w_ref[:, kw + hd * V_HEAD:kw + (hd + 1) * V_HEAD],
                    preferred_element_type=F32)
        o_ref[0, :, hd * V_HEAD:(hd + 1) * V_HEAD] = (o / den[rows, :]).astype(o_ref.dtype)


def _sample_attention(qt, lat_c, kpe_c, lat_n, kpe_n, w_kv, gk, *, batch, new):
    heads = qt.shape[0]
    q4 = qt.reshape(heads, QK_DIM, batch, new)
    qn = jnp.transpose(q4[:, :QK_NOPE], (2, 0, 1, 3))
    qbd = jnp.einsum('bhdj,hg->bhdgj', qn, jnp.eye(heads, dtype=qn.dtype))
    qbd = qbd.reshape(batch, heads * QK_NOPE, heads * new)
    qr = jnp.transpose(q4[:, QK_NOPE:], (2, 1, 0, 3)).reshape(batch, QK_ROPE, heads * new)
    per_batch = lambda a: pl.BlockSpec((1,) + a.shape[1:], lambda b: (b, 0, 0))
    return pl.pallas_call(
        functools.partial(_sample_attn_kernel, new=new),
        grid=(batch,),
        in_specs=[per_batch(qbd), per_batch(qr), per_batch(lat_c), per_batch(kpe_c),
                  per_batch(lat_n), per_batch(kpe_n), _const_spec(w_kv.shape),
                  _const_spec(gk.shape)],
        out_specs=pl.BlockSpec((1, new, heads * V_HEAD), lambda b: (b, 0, 0)),
        out_shape=jax.ShapeDtypeStruct((batch, new, heads * V_HEAD), BF16),
        compiler_params=pltpu.CompilerParams(
            dimension_semantics=("arbitrary",), vmem_limit_bytes=_vmem_limit(48 << 20)),
        name="sample_attention",
    )(qbd, qr, lat_c, kpe_c, lat_n, kpe_n, w_kv, gk)


def _retention_kernel(q_ref, k_ref, v_ref, rg_ref, dec_ref, qd_ref, kd_ref, g_ref, s0_ref,
                      o_ref, sfin_ref, state_ref, *, chunk, n_chunks):
    c = pl.program_id(1)

    @pl.when(c == 0)
    def _():
        state_ref[...] = s0_ref[0]

    heads = range(RET_HEADS)
    kcols = [slice(hd * RET_DK, (hd + 1) * RET_DK) for hd in heads]
    vcols = [slice(hd * RET_DV, (hd + 1) * RET_DV) for hd in heads]
    q = [q_ref[:, kcols[hd]] for hd in heads]
    k = [k_ref[:, kcols[hd]] for hd in heads]
    v = [v_ref[:, vcols[hd]] for hd in heads]
    state = [state_ref[hd] for hd in heads]
    qd = [qd_ref[hd] for hd in heads]
    s = [_qk(q[hd], k[hd]) for hd in heads]
    cross = [jnp.dot(q[hd], state[hd].astype(BF16), preferred_element_type=F32) for hd in heads]
    s = [(s[hd] * dec_ref[hd]).astype(BF16) for hd in heads]
    k_dec = [(k[hd].astype(F32) * kd_ref[hd]).astype(BF16) for hd in heads]
    inner = [jnp.dot(s[hd], v[hd], preferred_element_type=F32) for hd in heads]
    kv = [lax.dot_general(k_dec[hd], v[hd], (((0,), (0,)), ((), ())),
                          preferred_element_type=F32) for hd in heads]
    for hd in heads:
        state_ref[hd] = state[hd] * qd[hd][chunk - 1:chunk, :] + kv[hd]
    for hd in heads:
        o = inner[hd] + cross[hd] * qd[hd]
        mu = jnp.mean(o, axis=-1, keepdims=True)
        oc = o - mu
        var = jnp.mean(oc * oc, axis=-1, keepdims=True)
        y = oc * lax.rsqrt(var + EPS) * g_ref[hd]
        rg = rg_ref[:, vcols[hd]]
        o_ref[:, vcols[hd]] = (rg * jax.nn.sigmoid(rg) * y).astype(o_ref.dtype)

    @pl.when(c == n_chunks - 1)
    def _():
        sfin_ref[0] = state_ref[...]


def _retention_tables(chunk):
    lg = jnp.log1p(-(2.0 ** (-5.0 - jnp.arange(RET_HEADS, dtype=F32))))
    idx = jnp.arange(chunk, dtype=F32)
    diff = idx[:, None] - idx[None, :]
    decay = jnp.where(diff[None] >= 0,
                      jnp.exp(jnp.maximum(diff, 0.0)[None] * lg[:, None, None]), 0.0)
    q_dec = jnp.exp((idx + 1.0)[None, :] * lg[:, None])[..., None]
    k_dec = jnp.exp((chunk - 1.0 - idx)[None, :] * lg[:, None])[..., None]
    return decay, q_dec, k_dec


def _retention(rq, rk, rv, rg, g_ret, s0, *, batch, seq, chunk):
    n = rq.shape[0]
    nc = seq // chunk
    decay, q_dec, k_dec = _retention_tables(chunk)
    tok = lambda w: pl.BlockSpec((chunk, w), lambda b, c: (b * nc + c, 0))
    state_spec = pl.BlockSpec((1, RET_HEADS, RET_DK, RET_DV), lambda b, c: (b, 0, 0, 0))
    g_ret = g_ret.reshape(RET_HEADS, 1, RET_DV)
    return pl.pallas_call(
        functools.partial(_retention_kernel, chunk=chunk, n_chunks=nc),
        grid=(batch, nc),
        in_specs=[tok(RET_HEADS * RET_DK), tok(RET_HEADS * RET_DK), tok(RET_HEADS * RET_DV),
                  tok(RET_HEADS * RET_DV), _const_spec(decay.shape), _const_spec(q_dec.shape),
                  _const_spec(k_dec.shape), _const_spec(g_ret.shape), state_spec],
        out_specs=[tok(RET_HEADS * RET_DV), state_spec],
        out_shape=(jax.ShapeDtypeStruct((n, RET_HEADS * RET_DV), BF16),
                   jax.ShapeDtypeStruct((batch, RET_HEADS, RET_DK, RET_DV), F32)),
        scratch_shapes=[pltpu.VMEM((RET_HEADS, RET_DK, RET_DV), F32)],
        compiler_params=pltpu.CompilerParams(dimension_semantics=("arbitrary", "arbitrary")),
        name="retention",
    )(rq, rk, rv, rg, decay, q_dec, k_dec, g_ret, s0)


def _merge_kernel(x_ref, a_ref, r_ref, ga_ref, gb_ref, wa_ref, wb_ref, wo_ref, o_ref):
    a_d = jnp.dot(a_ref[...], wa_ref[...], preferred_element_type=F32)
    r_d = jnp.dot(r_ref[...], wb_ref[...], preferred_element_type=F32)
    merged = jax.nn.sigmoid(ga_ref[...]) * a_d + jax.nn.sigmoid(gb_ref[...]) * r_d
    o_ref[...] = x_ref[...] + jnp.dot(merged.astype(BF16), wo_ref[...],
                                      preferred_element_type=F32)


def _merge(x, attn, ret, ga, gb, wa, wb, wo, *, tm):
    n, d = x.shape
    row = pl.BlockSpec((tm, d), lambda i: (i, 0))
    return pl.pallas_call(
        _merge_kernel,
        grid=(n // tm,),
        in_specs=[row, row, row, row, row,
                  _const_spec(wa.shape), _const_spec(wb.shape), _const_spec(wo.shape)],
        out_specs=row,
        out_shape=jax.ShapeDtypeStruct((n, d), F32),
        compiler_params=pltpu.CompilerParams(
            dimension_semantics=("arbitrary",), vmem_limit_bytes=_vmem_limit(40 << 20)),
        name="merge",
    )(x, attn, ret, ga, gb, wa, wb, wo)


_PAD_ROWS = 8


def _ffn_kernel(x_ref, g_ref, wup_ref, cw_ref, cb_ref, wdn_ref, prev_ref,
                y_ref, st_ref, ubuf_ref, *, rows, n_seqs, d_ff, n_tiles):
    j = pl.program_id(1)
    lo = _PAD_ROWS - (CONV_W - 1)
    stride = _PAD_ROWS + rows
    assert n_seqs == 1 or n_tiles == 1

    @pl.when(j == 0)
    def _():
        for b in range(n_seqs):
            ubuf_ref[b * stride + lo:b * stride + _PAD_ROWS, :] = prev_ref[b]

    if n_tiles > 1:
        @pl.when(j > 0)
        def _():
            ubuf_ref[0:_PAD_ROWS, :] = ubuf_ref[rows:rows + _PAD_ROWS, :]

    xf = x_ref[...]
    hn = (xf * _rms_scale(xf, xf.shape[-1]) * g_ref[...]).astype(BF16)
    u = jnp.dot(hn, wup_ref[...], preferred_element_type=F32)
    for b in range(n_seqs):
        ubuf_ref[b * stride + _PAD_ROWS:(b + 1) * stride, :] = u[b * rows:(b + 1) * rows, :]
    runs = []
    for b in range(n_seqs):
        c = cb_ref[...]
        for tap in range(CONV_W):
            first = b * stride + lo + tap
            c = c + ubuf_ref[first:first + rows, :] * cw_ref[tap:tap + 1, :]
        runs.append(c)
    c = runs[0] if n_seqs == 1 else jnp.concatenate(runs, axis=0)
    gate, val = c[:, :d_ff], c[:, d_ff:]
    act = (gate * jax.nn.sigmoid(gate) * val).astype(BF16)
    y_ref[...] = xf + jnp.dot(act, wdn_ref[...], preferred_element_type=F32)

    @pl.when(j == n_tiles - 1)
    def _():
        for b in range(n_seqs):
            st_ref[b] = ubuf_ref[(b + 1) * stride - (CONV_W - 1):(b + 1) * stride, :]


def _conv_ffn(x, g, w_up, cw, cb, w_dn, prev, *, batch, seq, tm):
    n, d = x.shape
    d_ff = w_dn.shape[0]
    n_seqs = max(tm // seq, 1)
    rows = tm // n_seqs
    nt = seq // rows
    row = pl.BlockSpec((tm, d), lambda b, j: (b * nt + j, 0))
    st = pl.BlockSpec((n_seqs, CONV_W - 1, 2 * d_ff), lambda b, j: (b, 0, 0))
    return pl.pallas_call(
        functools.partial(_ffn_kernel, rows=rows, n_seqs=n_seqs, d_ff=d_ff, n_tiles=nt),
        grid=(batch // n_seqs, nt),
        in_specs=[row, _const_spec(g.shape), _const_spec(w_up.shape), _const_spec(cw.shape),
                  _const_spec(cb.shape), _const_spec(w_dn.shape), st],
        out_specs=[row, st],
        out_shape=(jax.ShapeDtypeStruct((n, d), F32),
                   jax.ShapeDtypeStruct((batch, CONV_W - 1, 2 * d_ff), F32)),
        scratch_shapes=[pltpu.VMEM((n_seqs * (_PAD_ROWS + rows), 2 * d_ff), F32)],
        compiler_params=pltpu.CompilerParams(
            dimension_semantics=("arbitrary", "arbitrary"),
            vmem_limit_bytes=_vmem_limit(56 << 20)),
        name="conv_ffn",
    )(x, g, w_up, cw, cb, w_dn, prev)


def _pack_weights(w_in, w_q_up, w_kv_up, w_o_branch, w_out, w_ffn_up, w_ffn_down):
    head_width = sum(width for _, width in _IN_HEAD)
    w_head = jnp.pad(w_in[:, :_IN_HEAD_SRC_WIDTH].astype(BF16),
                     ((0, 0), (0, head_width - _IN_HEAD_SRC_WIDTH)))
    w_in_p = (w_head, w_in[:, _IN_HEAD_SRC_WIDTH:].astype(BF16))
    w_q_p = w_q_up.T.astype(BF16)
    wkv = w_kv_up.reshape(w_kv_up.shape[0], MLA_HEADS, QK_NOPE + V_HEAD)
    w_kv_p = jnp.concatenate([wkv[:, :, :QK_NOPE].reshape(wkv.shape[0], -1),
                              wkv[:, :, QK_NOPE:].reshape(wkv.shape[0], -1)], axis=1).astype(BF16)
    mla_width = MLA_HEADS * V_HEAD
    return (w_in_p, w_q_p, w_kv_p, w_o_branch[:mla_width].astype(BF16),
            w_o_branch[mla_width:].astype(BF16), w_out.astype(BF16),
            w_ffn_up.astype(BF16), w_ffn_down.astype(BF16))


def _layer(x, pos, past, w, *, batch, seq, tm, tq, ret_chunk, ffn_tm):
    n = x.shape[0]
    tabs64 = _rope_tables(pos, RET_DK // 2)
    tabs32 = _rope_tables(pos, QK_ROPE // 2)
    seq_tiles = max(seq // tm, 1)
    if seq < tm:
        tabs64 = tuple(np.tile(t, (tm // seq, 1)) for t in tabs64)
        tabs32 = tuple(np.tile(t, (tm // seq, 1)) for t in tabs32)
    qlat, ckv, kpe, rq, rk, rv, rg, ga, gb = _in_proj(
        x, w['w_in'], w['g_norm_mix'], w['g_q_lat'], w['g_kv_lat'], w['g_k_rope'],
        tabs64, tabs32, seq_tiles=seq_tiles, tm=tm)
    tabs32_t = tuple(np.ascontiguousarray(t[:, :QK_ROPE].T) for t in tabs32)
    bcast = lambda g: jnp.broadcast_to(g.reshape(-1, 1), (g.size, tm))
    qt = _q_up(qlat, w['w_q_up'], bcast(w['g_q_nope']), bcast(w['g_q_rope']), tabs32_t,
               seq_tiles=seq_tiles, tm=tm)
    if past is None:
        k_new, vt_new = _kv_up(ckv, kpe, w['w_kv_up'], w['g_k_nope'], tm=tm)
        attn = _prompt_attention(qt, k_new, vt_new, batch=batch, seq=seq, tq=tq)
        s0 = jnp.zeros((batch, RET_HEADS, RET_DK, RET_DV), F32)
        prev = jnp.zeros((batch, CONV_W - 1, w['w_ffn_up'].shape[1]), F32)
    else:
        lat_c, pe_c, s0, prev = past
        attn = _sample_attention(qt, lat_c, pe_c, ckv.reshape(batch, seq, -1),
                                 kpe.reshape(batch, seq, -1), w['w_kv_up'], w['g_k_nope'],
                                 batch=batch, new=seq).reshape(n, -1)
    ret, s_fin = _retention(rq, rk, rv, rg, w['g_ret_out'], s0,
                            batch=batch, seq=seq, chunk=ret_chunk)
    x1 = _merge(x, attn, ret, ga, gb, w['w_o_a'], w['w_o_b'], w['w_out'], tm=tm)
    y, conv_state = _conv_ffn(x1, w['g_norm_ffn'], w['w_ffn_up'], w['ffn_conv_w'],
                              w['ffn_conv_b'], w['w_ffn_down'], prev,
                              batch=batch, seq=seq, tm=ffn_tm)
    return y, (ckv, kpe, s_fin, conv_state)


def kernel(x_prompt, x_sample, cache_mla_latent, cache_mla_rope_key, state_retention, state_ffn_conv, g_norm_mix, w_in, g_q_lat, w_q_up, g_q_nope, g_q_rope, g_kv_lat, w_kv_up, g_k_nope, g_k_rope, g_ret_out, w_o_branch, w_out, g_norm_ffn, w_ffn_up, ffn_conv_w, ffn_conv_b, w_ffn_down):
    depth = w_in.shape[0]
    assert depth == 1, "single-layer trunk"
    bp, tp, d = x_prompt.shape
    bs, ts, _ = x_sample.shape
    past_len = cache_mla_latent.shape[2]
    (w_in_p, w_q_p, w_kv_p, w_o_a, w_o_b, w_out_p, w_up_p, w_dn_p) = _pack_weights(
        w_in[0], w_q_up[0], w_kv_up[0], w_o_branch[0], w_out[0], w_ffn_up[0], w_ffn_down[0])
    row = lambda g: g.reshape(1, -1).astype(F32)
    w = dict(
        w_in=w_in_p, w_q_up=w_q_p, w_kv_up=w_kv_p, w_o_a=w_o_a, w_o_b=w_o_b, w_out=w_out_p,
        w_ffn_up=w_up_p, w_ffn_down=w_dn_p,
        g_norm_mix=row(g_norm_mix[0]), g_q_lat=row(g_q_lat[0]), g_kv_lat=row(g_kv_lat[0]),
        g_k_rope=jnp.pad(row(g_k_rope[0]), ((0, 0), (0, LANES - QK_ROPE))),
        g_q_nope=g_q_nope[0].astype(F32), g_q_rope=g_q_rope[0].astype(F32),
        g_k_nope=row(g_k_nope[0]), g_ret_out=g_ret_out[0], g_norm_ffn=row(g_norm_ffn[0]),
        ffn_conv_w=ffn_conv_w[0], ffn_conv_b=row(ffn_conv_b[0]))

    yp, sp = _layer(x_prompt.reshape(bp * tp, d), np.arange(tp), None, w,
                    batch=bp, seq=tp, tm=512, tq=2048, ret_chunk=256, ffn_tm=256)
    past = (cache_mla_latent[0], cache_mla_rope_key[0], state_retention[0], state_ffn_conv[0])
    ys, ss = _layer(x_sample.reshape(bs * ts, d), past_len + np.arange(ts), past, w,
                    batch=bs, seq=ts, tm=bs * ts, tq=None, ret_chunk=ts, ffn_tm=bs * ts)

    def states(st, b, t):
        ckv, kpe, s_fin, conv = st
        return (ckv.reshape(1, b, t, -1), kpe.reshape(1, b, t, -1), s_fin[None], conv[None])

    return (yp.reshape(bp, tp, d), ys.reshape(bs, ts, d)) + states(sp, bp, tp) + states(ss, bs, ts)
```

```python
import functools
import math

import jax
import jax.numpy as jnp
import numpy as np
from jax import lax
from jax.experimental import pallas as pl
from jax.experimental.pallas import tpu as pltpu

F32 = jnp.float32
BF16 = jnp.bfloat16

EPS = 1e-6
NEG_INF = -1e30
ROPE_THETA = 10000.0
CHUNK = 64
LANES = 128
V7X_VMEM_BYTES = 64 * 1024 * 1024

MLA_HEADS = 8
QK_NOPE = 128
QK_ROPE = 64
QK_DIM = QK_NOPE + QK_ROPE
V_HEAD = 128
MLA_SCALE = QK_DIM ** -0.5
Q_PRESCALE = MLA_SCALE * math.log2(math.e)
RET_HEADS = 4
RET_DK = 128
RET_DV = 256
CONV_W = 3


def _vmem_limit(nbytes):
    return int(min(V7X_VMEM_BYTES - (4 << 20), max(32 << 20, nbytes)))


def _const_spec(shape):
    nd = len(shape)
    return pl.BlockSpec(shape, lambda *_: (0,) * nd, pipeline_mode=pl.Buffered(1))


def _rms_scale(z, width):
    return lax.rsqrt(jnp.sum(z * z, axis=-1, keepdims=True) * (1.0 / width) + EPS)


def _swap_halves_64(z):
    lane = lax.broadcasted_iota(jnp.int32, z.shape, 1)
    return jnp.where((lane % 64) < 32, pltpu.roll(z, 96, 1), pltpu.roll(z, 32, 1))


def _rope_tables(pos, half):
    freqs = ROPE_THETA ** (-np.arange(half, dtype=np.float64) / half)
    ang = pos.astype(np.float64)[:, None] * freqs[None, :]
    cos, sin = np.cos(ang), np.sin(ang)
    reps = LANES // (2 * half)
    return (np.tile(np.concatenate([cos, cos], -1), (1, reps)).astype(np.float32),
            np.tile(np.concatenate([-sin, sin], -1), (1, reps)).astype(np.float32))


_IN_HEAD = (('qlat', 768), ('kvl', 512), ('kpe', LANES))
_IN_TAIL = (('rq', 512), ('rk', 512), ('rv', 1024), ('rg', 1024), ('ga', 1024), ('gb', 1024))
_IN_HEAD_SRC_WIDTH = 768 + 512 + QK_ROPE


def _group_offsets(groups):
    offs, o = {}, 0
    for name, width in groups:
        offs[name] = (o, o + width)
        o += width
    return offs


_IN_OFFS = {**{n: (0,) + ab for n, ab in _group_offsets(_IN_HEAD).items()},
            **{n: (1,) + ab for n, ab in _group_offsets(_IN_TAIL).items()}}


def _inproj_kernel(x_ref, gmix_ref, wh_ref, wt_ref, gq_ref, gkv_ref, gkpe_ref,
                   c64_ref, s64_ref, c32_ref, s32_ref,
                   qlat_ref, ckv_ref, kpe_ref, rq_ref, rk_ref, rv_ref, rg_ref, ga_ref, gb_ref):
    xf = x_ref[...]
    h = (xf * _rms_scale(xf, xf.shape[-1]) * gmix_ref[...]).astype(BF16)

    def proj(name):
        part, a, b = _IN_OFFS[name]
        w_ref = (wh_ref, wt_ref)[part]
        return jnp.dot(h, w_ref[:, a:b], preferred_element_type=F32)

    z = proj('qlat')
    qlat_ref[...] = (z * _rms_scale(z, z.shape[-1]) * gq_ref[...]).astype(BF16)
    z = proj('kvl')
    ckv_ref[...] = z * _rms_scale(z, z.shape[-1]) * gkv_ref[...]
    z = proj('kpe')
    zn = z * _rms_scale(z, QK_ROPE) * gkpe_ref[...]
    kpe = zn * c32_ref[...] + _swap_halves_64(zn) * s32_ref[...]
    kpe_ref[...] = kpe[:, :QK_ROPE]
    c64, s64 = c64_ref[...], s64_ref[...]
    for name, ref, scale in (('rq', rq_ref, 1.0), ('rk', rk_ref, RET_DK ** -0.5)):
        z = proj(name)
        for hd in range(RET_HEADS):
            zh = z[:, hd * RET_DK:(hd + 1) * RET_DK]
            r = zh * c64 + pltpu.roll(zh, RET_DK // 2, 1) * s64
            if scale != 1.0:
                r = r * scale
            ref[:, hd * RET_DK:(hd + 1) * RET_DK] = r.astype(BF16)
    rv_ref[...] = proj('rv').astype(BF16)
    rg_ref[...] = proj('rg')
    ga_ref[...] = proj('ga')
    gb_ref[...] = proj('gb')


def _in_proj(x, w_in_parts, gmix, gq, gkv, gkpe, tabs64, tabs32, *, seq_tiles, tm):
    n, d = x.shape
    w_head, w_tail = w_in_parts
    row = lambda w: pl.BlockSpec((tm, w), lambda i: (i, 0))
    tab = pl.BlockSpec((tm, LANES), lambda i: (i % seq_tiles, 0))
    out_shapes = (
        jax.ShapeDtypeStruct((n, 768), BF16), jax.ShapeDtypeStruct((n, 512), F32),
        jax.ShapeDtypeStruct((n, QK_ROPE), F32), jax.ShapeDtypeStruct((n, 512), BF16),
        jax.ShapeDtypeStruct((n, 512), BF16), jax.ShapeDtypeStruct((n, 1024), BF16),
        jax.ShapeDtypeStruct((n, 1024), F32), jax.ShapeDtypeStruct((n, 1024), F32),
        jax.ShapeDtypeStruct((n, 1024), F32))
    return pl.pallas_call(
        _inproj_kernel,
        grid=(n // tm,),
        in_specs=[row(d), _const_spec(gmix.shape), _const_spec(w_head.shape),
                  _const_spec(w_tail.shape), _const_spec(gq.shape), _const_spec(gkv.shape),
                  _const_spec(gkpe.shape), tab, tab, tab, tab],
        out_specs=[row(s.shape[1]) for s in out_shapes],
        out_shape=out_shapes,
        compiler_params=pltpu.CompilerParams(
            dimension_semantics=("arbitrary",), vmem_limit_bytes=_vmem_limit(56 << 20)),
        name="in_proj",
    )(x, gmix, w_head, w_tail, gq, gkv, gkpe, *tabs64, *tabs32)


def _qup_kernel(ql_ref, wt_ref, gn_ref, gr_ref, c_ref, s_ref, qt_ref):
    zt = lax.dot_general(wt_ref[...], ql_ref[...], (((1,), (1,)), ((), ())),
                         preferred_element_type=F32)
    gn, gr, cos, sin = gn_ref[...], gr_ref[...], c_ref[...], s_ref[...]
    half = QK_ROPE // 2
    for hd in range(MLA_HEADS):
        zn = zt[hd * QK_DIM:hd * QK_DIM + QK_NOPE, :]
        inv = lax.rsqrt(jnp.sum(zn * zn, axis=0, keepdims=True) * (1.0 / QK_NOPE) + EPS)
        qt_ref[hd, :QK_NOPE, :] = (zn * (inv * Q_PRESCALE) * gn).astype(BF16)
        zr = zt[hd * QK_DIM + QK_NOPE:(hd + 1) * QK_DIM, :]
        inv = lax.rsqrt(jnp.sum(zr * zr, axis=0, keepdims=True) * (1.0 / QK_ROPE) + EPS)
        z = zr * (inv * Q_PRESCALE) * gr
        swapped = jnp.concatenate([z[half:], z[:half]], axis=0)
        qt_ref[hd, QK_NOPE:, :] = (z * cos + swapped * sin).astype(BF16)


def _q_up(qlat, w_qt, gn, gr, tabs32_t, *, seq_tiles, tm):
    n = qlat.shape[0]
    tab = pl.BlockSpec((QK_ROPE, tm), lambda i: (0, i % seq_tiles))
    return pl.pallas_call(
        _qup_kernel,
        grid=(n // tm,),
        in_specs=[pl.BlockSpec((tm, qlat.shape[1]), lambda i: (i, 0)), _const_spec(w_qt.shape),
                  _const_spec(gn.shape), _const_spec(gr.shape), tab, tab],
        out_specs=pl.BlockSpec((MLA_HEADS, QK_DIM, tm), lambda i: (0, 0, i)),
        out_shape=jax.ShapeDtypeStruct((MLA_HEADS, QK_DIM, n), BF16),
        compiler_params=pltpu.CompilerParams(
            dimension_semantics=("arbitrary",), vmem_limit_bytes=_vmem_limit(40 << 20)),
        name="q_up",
    )(qlat, w_qt, gn, gr, *tabs32_t)


_ONES_ROWS = 16


def _kvup_kernel(c_ref, kpe_ref, w_ref, gk_ref, k_ref, vt_ref):
    c = c_ref[...].astype(BF16)
    kpe = kpe_ref[...].astype(BF16)
    gk = gk_ref[...]
    kw = MLA_HEADS * QK_NOPE
    for pair in range(MLA_HEADS // 2):
        zk = jnp.dot(c, w_ref[:, pair * 256:(pair + 1) * 256], preferred_element_type=F32)
        zv = jnp.dot(c, w_ref[:, kw + pair * 256:kw + (pair + 1) * 256],
                     preferred_element_type=F32)
        for sub in range(2):
            hd = 2 * pair + sub
            zh = zk[:, sub * QK_NOPE:(sub + 1) * QK_NOPE]
            k_ref[hd, :, :QK_NOPE] = (zh * _rms_scale(zh, QK_NOPE) * gk).astype(BF16)
            k_ref[hd, :, QK_NOPE:] = kpe
            vh = zv[:, sub * V_HEAD:(sub + 1) * V_HEAD]
            vt_ref[hd, :V_HEAD, :] = vh.T.astype(BF16)
            vt_ref[hd, V_HEAD:, :] = jnp.ones((_ONES_ROWS, vh.shape[0]), BF16)


def _kv_up(ckv, kpe, w_kv, gk, *, tm):
    n = ckv.shape[0]
    v_spec = pl.BlockSpec((MLA_HEADS, V_HEAD + _ONES_ROWS, tm), lambda i: (0, 0, i))
    v_shape = (MLA_HEADS, V_HEAD + _ONES_ROWS, n)
    return pl.pallas_call(
        _kvup_kernel,
        grid=(n // tm,),
        in_specs=[pl.BlockSpec((tm, ckv.shape[1]), lambda i: (i, 0)),
                  pl.BlockSpec((tm, QK_ROPE), lambda i: (i, 0)),
                  _const_spec(w_kv.shape), _const_spec(gk.shape)],
        out_specs=[pl.BlockSpec((MLA_HEADS, tm, QK_DIM), lambda i: (0, i, 0)), v_spec],
        out_shape=(jax.ShapeDtypeStruct((MLA_HEADS, n, QK_DIM), BF16),
                   jax.ShapeDtypeStruct(v_shape, BF16)),
        compiler_params=pltpu.CompilerParams(
            dimension_semantics=("arbitrary",), vmem_limit_bytes=_vmem_limit(40 << 20)),
        name="kv_up",
    )(ckv, kpe, w_kv, gk)


def _qk(q, k):
    return lax.dot_general(q, k, (((1,), (1,)), ((), ())), preferred_element_type=F32)


_STRIP = 256
_ATTN_KEY_BLOCK = 512


def _prompt_attn_kernel(qt_ref, k_ref, vt_ref, o_ref, s_ref, smax_ref, p_ref, a_ref, m_ref,
                        acc_ref, *, tq, tk):
    n_tiles = qt_ref.shape[2] // tq
    n_strips = tq // _STRIP
    n_diag = tq // tk
    assert n_diag % 2 == 0 and tk % _STRIP == 0 and _STRIP % CHUNK == 0
    all_strips = tuple(range(n_strips))
    cols = lambda c: slice(c * _STRIP, (c + 1) * _STRIP)
    n_seen = lambda d, c: max(0, min(tk, (c + 1) * _STRIP - d * tk))
    seen = lambda d: tuple(c for c in all_strips if n_seen(d, c) > 0)
    unmasked = lambda d, c: c * _STRIP >= (d + 1) * tk

    def scores(i, b, slot, c, nk=tk):
        start = pl.multiple_of(b * tk, tk)
        q_start = pl.multiple_of(i * tq + c * _STRIP, _STRIP)
        s = jnp.dot(k_ref[0, pl.ds(start, nk), :], qt_ref[0, :, pl.ds(q_start, _STRIP)],
                    preferred_element_type=F32)
        s_ref[slot, :nk, cols(c)] = s
        smax_ref[slot, :, cols(c)] = jnp.max(s, axis=0, keepdims=True)

    def softmax(slot, c, diag_block=None):
        nk = tk if diag_block is None else n_seen(diag_block, c)
        s = s_ref[slot, :nk, cols(c)]
        if diag_block is None or unmasked(diag_block, c):
            s_max = smax_ref[slot, :, cols(c)]
        else:
            kc = (lax.broadcasted_iota(jnp.int32, s.shape, 0) + diag_block * tk) // CHUNK
            qc = (lax.broadcasted_iota(jnp.int32, s.shape, 1) + c * _STRIP) // CHUNK
            s = jnp.where(kc <= qc, s, NEG_INF)
            s_max = jnp.max(s, axis=0, keepdims=True)
        m_prev = m_ref[:, cols(c)]
        m_new = jnp.maximum(m_prev, s_max)
        alpha = jnp.exp2(m_prev - m_new)
        p = jnp.exp2(s - m_new)
        m_ref[:, cols(c)] = m_new
        a_ref[slot, :, cols(c)] = alpha
        p_ref[slot, :nk, cols(c)] = p.astype(BF16)

    def values(b, slot, c, nk=tk):
        start = pl.multiple_of(jnp.maximum(b, 0) * tk, tk)
        pv = jnp.dot(vt_ref[0, :, pl.ds(start, nk)], p_ref[slot, :nk, cols(c)],
                     preferred_element_type=F32)
        acc_ref[:, cols(c)] = acc_ref[:, cols(c)] * a_ref[slot, :, cols(c)] + pv

    def first_scores(i):
        for c in all_strips:
            scores(i, 0, 0, c)

    def reset_state():
        m_ref[...] = jnp.full(m_ref.shape, NEG_INF, F32)
        acc_ref[...] = jnp.zeros(acc_ref.shape, F32)
        p_ref[1] = jnp.zeros(p_ref.shape[1:], BF16)
        a_ref[1] = jnp.ones(a_ref.shape[1:], F32)

    def tile(i, carry):
        def pair(jj, carry):
            b = 2 * jj
            for c in all_strips:
                scores(i, b + 1, 1, c)
            for c in all_strips:
                values(b - 1, 1, c)
            for c in all_strips:
                softmax(0, c)
            for c in all_strips:
                scores(i, b + 2, 0, c)
            for c in all_strips:
                values(b, 0, c)
            for c in all_strips:
                softmax(1, c)
            return carry

        lax.fori_loop(0, i * (n_diag // 2), pair, 0)
        b0 = n_diag * i
        for t in range(n_diag + 1):
            if 1 <= t + 1 < n_diag:
                for c in seen(t + 1):
                    scores(i, b0 + t + 1, (t + 1) % 2, c, n_seen(t + 1, c))
            if t == n_diag - 1:
                first_scores(jnp.minimum(i + 1, n_tiles - 1))
            if t == 0:
                for c in all_strips:
                    values(b0 - 1, 1, c)
            else:
                for c in seen(t - 1):
                    values(b0 + t - 1, (t - 1) % 2, c, n_seen(t - 1, c))
            if t < n_diag:
                for c in seen(t):
                    softmax(t % 2, c, diag_block=t)
        o = acc_ref[:V_HEAD, :] / acc_ref[V_HEAD:V_HEAD + 1, :]
        o_ref[pl.ds(pl.multiple_of(i * tq, tq), tq), :] = o.T.astype(o_ref.dtype)
        reset_state()
        return carry

    first_scores(0)
    reset_state()
    lax.fori_loop(0, n_tiles, tile, 0)


def _prompt_attention(qt, k, vt, *, batch, seq, tq):
    heads, _, n = qt.shape
    tk = _ATTN_KEY_BLOCK
    return pl.pallas_call(
        functools.partial(_prompt_attn_kernel, tq=tq, tk=tk),
        grid=(batch, heads),
        in_specs=[pl.BlockSpec((1, QK_DIM, seq), lambda b, h: (h, 0, b)),
                  pl.BlockSpec((1, seq, QK_DIM), lambda b, h: (h, b, 0)),
                  pl.BlockSpec((1, vt.shape[1], seq), lambda b, h: (h, 0, b))],
        out_specs=pl.BlockSpec((seq, V_HEAD), lambda b, h: (b, h)),
        out_shape=jax.ShapeDtypeStruct((n, heads * V_HEAD), BF16),
        scratch_shapes=[pltpu.VMEM((2, tk, tq), F32), pltpu.VMEM((2, 1, tq), F32),
                        pltpu.VMEM((2, tk, tq), BF16),
                        pltpu.VMEM((2, 1, tq), F32), pltpu.VMEM((1, tq), F32),
                        pltpu.VMEM((vt.shape[1], tq), F32)],
        compiler_params=pltpu.CompilerParams(
            dimension_semantics=("arbitrary", "arbitrary"),
            vmem_limit_bytes=_vmem_limit(48 << 20)),
        name="prompt_attention",
    )(qt, k, vt)


def _sample_attn_kernel(qbd_ref, qr_ref, latc_ref, kpec_ref, latn_ref, kpen_ref, w_ref, gk_ref,
                        o_ref, *, new):
    gk = gk_ref[...]
    kw = MLA_HEADS * QK_NOPE

    def scores(lat_ref, kpe_ref):
        c = lat_ref[0].astype(BF16)
        k_heads = []
        for pair in range(MLA_HEADS // 2):
            zk = jnp.dot(c, w_ref[:, pair * 256:(pair + 1) * 256], preferred_element_type=F32)
            for sub in range(2):
                zh = zk[:, sub * QK_NOPE:(sub + 1) * QK_NOPE]
                k_heads.append((zh * _rms_scale(zh, QK_NOPE) * gk).astype(BF16))
        k_all = jnp.concatenate(k_heads, axis=1)
        s = (jnp.dot(k_all, qbd_ref[0], preferred_element_type=F32)
             + jnp.dot(kpe_ref[0].astype(BF16), qr_ref[0], preferred_element_type=F32))
        return s, c

    s_c, c_c = scores(latc_ref, kpec_ref)
    s_n, c_n = scores(latn_ref, kpen_ref)
    m = jnp.maximum(jnp.max(s_c, axis=0, keepdims=True), jnp.max(s_n, axis=0, keepdims=True))
    p_c = jnp.exp2(s_c - m).astype(BF16)
    p_n = jnp.exp2(s_n - m).astype(BF16)
    tn = (((0,), (0,)), ((), ()))
    ctx = (lax.dot_general(p_c, c_c, tn, preferred_element_type=F32)
           + lax.dot_general(p_n, c_n, tn, preferred_element_type=F32))
    den = (lax.dot_general(p_c, jnp.ones((p_c.shape[0], V_HEAD), BF16), tn,
                           preferred_element_type=F32)
           + lax.dot_general(p_n, jnp.ones((p_n.shape[0], V_HEAD), BF16), tn,
                             preferred_element_type=F32))
    ctx = ctx.astype(BF16)
    for hd in range(MLA_HEADS):
        rows = slice(hd * new, (hd + 1) * new)
        o = jnp.dot(ctx[rows, :], w_ref[:, kw + hd * V_HEAD:kw + (hd + 1) * V_HEAD],
                    preferred_element_type=F32)
        o_ref[0, :, hd * V_HEAD:(hd + 1) * V_HEAD] = (o / den[rows, :]).astype(o_ref.dtype)


def _sample_attention(qt, lat_c, kpe_c, lat_n, kpe_n, w_kv, gk, *, batch, new):
    heads = qt.shape[0]
    q4 = qt.reshape(heads, QK_DIM, batch, new)
    qn = jnp.transpose(q4[:, :QK_NOPE], (2, 0, 1, 3))
    qbd = jnp.einsum('bhdj,hg->bhdgj', qn, jnp.eye(heads, dtype=qn.dtype))
    qbd = qbd.reshape(batch, heads * QK_NOPE, heads * new)
    qr = jnp.transpose(q4[:, QK_NOPE:], (2, 1, 0, 3)).reshape(batch, QK_ROPE, heads * new)
    per_batch = lambda a: pl.BlockSpec((1,) + a.shape[1:], lambda b: (b, 0, 0))
    return pl.pallas_call(
        functools.partial(_sample_attn_kernel, new=new),
        grid=(batch,),
        in_specs=[per_batch(qbd), per_batch(qr), per_batch(lat_c), per_batch(kpe_c),
                  per_batch(lat_n), per_batch(kpe_n), _const_spec(w_kv.shape),
                  _const_spec(gk.shape)],
        out_specs=pl.BlockSpec((1, new, heads * V_HEAD), lambda b: (b, 0, 0)),
        out_shape=jax.ShapeDtypeStruct((batch, new, heads * V_HEAD), BF16),
        compiler_params=pltpu.CompilerParams(
            dimension_semantics=("arbitrary",), vmem_limit_bytes=_vmem_limit(48 << 20)),
        name="sample_attention",
    )(qbd, qr, lat_c, kpe_c, lat_n, kpe_n, w_kv, gk)


def _retention_kernel(q_ref, k_ref, v_ref, rg_ref, dec_ref, qd_ref, kd_ref, g_ref, s0_ref,
                      o_ref, sfin_ref, state_ref, *, chunk, n_chunks):
    c = pl.program_id(1)

    @pl.when(c == 0)
    def _():
        state_ref[...] = s0_ref[0]

    heads = range(RET_HEADS)
    kcols = [slice(hd * RET_DK, (hd + 1) * RET_DK) for hd in heads]
    vcols = [slice(hd * RET_DV, (hd + 1) * RET_DV) for hd in heads]
    q = [q_ref[:, kcols[hd]] for hd in heads]
    k = [k_ref[:, kcols[hd]] for hd in heads]
    v = [v_ref[:, vcols[hd]] for hd in heads]
    state = [state_ref[hd] for hd in heads]
    qd = [qd_ref[hd] for hd in heads]
    s = [_qk(q[hd], k[hd]) for hd in heads]
    cross = [jnp.dot(q[hd], state[hd].astype(BF16), preferred_element_type=F32) for hd in heads]
    s = [(s[hd] * dec_ref[hd]).astype(BF16) for hd in heads]
    k_dec = [(k[hd].astype(F32) * kd_ref[hd]).astype(BF16) for hd in heads]
    inner = [jnp.dot(s[hd], v[hd], preferred_element_type=F32) for hd in heads]
    kv = [lax.dot_general(k_dec[hd], v[hd], (((0,), (0,)), ((), ())),
                          preferred_element_type=F32) for hd in heads]
    for hd in heads:
        state_ref[hd] = state[hd] * qd[hd][chunk - 1:chunk, :] + kv[hd]
    for hd in heads:
        o = inner[hd] + cross[hd] * qd[hd]
        mu = jnp.mean(o, axis=-1, keepdims=True)
        oc = o - mu
        var = jnp.mean(oc * oc, axis=-1, keepdims=True)
        y = oc * lax.rsqrt(var + EPS) * g_ref[hd]
        rg = rg_ref[:, vcols[hd]]
        o_ref[:, vcols[hd]] = (rg * jax.nn.sigmoid(rg) * y).astype(o_ref.dtype)

    @pl.when(c == n_chunks - 1)
    def _():
        sfin_ref[0] = state_ref[...]


def _retention_tables(chunk):
    lg = jnp.log1p(-(2.0 ** (-5.0 - jnp.arange(RET_HEADS, dtype=F32))))
    idx = jnp.arange(chunk, dtype=F32)
    diff = idx[:, None] - idx[None, :]
    decay = jnp.where(diff[None] >= 0,
                      jnp.exp(jnp.maximum(diff, 0.0)[None] * lg[:, None, None]), 0.0)
    q_dec = jnp.exp((idx + 1.0)[None, :] * lg[:, None])[..., None]
    k_dec = jnp.exp((chunk - 1.0 - idx)[None, :] * lg[:, None])[..., None]
    return decay, q_dec, k_dec


def _retention(rq, rk, rv, rg, g_ret, s0, *, batch, seq, chunk):
    n = rq.shape[0]
    nc = seq // chunk
    decay, q_dec, k_dec = _retention_tables(chunk)
    tok = lambda w: pl.BlockSpec((chunk, w), lambda b, c: (b * nc + c, 0))
    state_spec = pl.BlockSpec((1, RET_HEADS, RET_DK, RET_DV), lambda b, c: (b, 0, 0, 0))
    g_ret = g_ret.reshape(RET_HEADS, 1, RET_DV)
    return pl.pallas_call(
        functools.partial(_retention_kernel, chunk=chunk, n_chunks=nc),
        grid=(batch, nc),
        in_specs=[tok(RET_HEADS * RET_DK), tok(RET_HEADS * RET_DK), tok(RET_HEADS * RET_DV),
                  tok(RET_HEADS * RET_DV), _const_spec(decay.shape), _const_spec(q_dec.shape),
                  _const_spec(k_dec.shape), _const_spec(g_ret.shape), state_spec],
        out_specs=[tok(RET_HEADS * RET_DV), state_spec],
        out_shape=(jax.ShapeDtypeStruct((n, RET_HEADS * RET_DV), BF16),
                   jax.ShapeDtypeStruct((batch, RET_HEADS, RET_DK, RET_DV), F32)),
        scratch_shapes=[pltpu.VMEM((RET_HEADS, RET_DK, RET_DV), F32)],
        compiler_params=pltpu.CompilerParams(dimension_semantics=("arbitrary", "arbitrary")),
        name="retention",
    )(rq, rk, rv, rg, decay, q_dec, k_dec, g_ret, s0)


def _merge_kernel(x_ref, a_ref, r_ref, ga_ref, gb_ref, wa_ref, wb_ref, wo_ref, o_ref):
    a_d = jnp.dot(a_ref[...], wa_ref[...], preferred_element_type=F32)
    r_d = jnp.dot(r_ref[...], wb_ref[...], preferred_element_type=F32)
    merged = jax.nn.sigmoid(ga_ref[...]) * a_d + jax.nn.sigmoid(gb_ref[...]) * r_d
    o_ref[...] = x_ref[...] + jnp.dot(merged.astype(BF16), wo_ref[...],
                                      preferred_element_type=F32)


def _merge(x, attn, ret, ga, gb, wa, wb, wo, *, tm):
    n, d = x.shape
    row = pl.BlockSpec((tm, d), lambda i: (i, 0))
    return pl.pallas_call(
        _merge_kernel,
        grid=(n // tm,),
        in_specs=[row, row, row, row, row,
                  _const_spec(wa.shape), _const_spec(wb.shape), _const_spec(wo.shape)],
        out_specs=row,
        out_shape=jax.ShapeDtypeStruct((n, d), F32),
        compiler_params=pltpu.CompilerParams(
            dimension_semantics=("arbitrary",), vmem_limit_bytes=_vmem_limit(40 << 20)),
        name="merge",
    )(x, attn, ret, ga, gb, wa, wb, wo)


_PAD_ROWS = 8


def _ffn_kernel(x_ref, g_ref, wup_ref, cw_ref, cb_ref, wdn_ref, prev_ref,
                y_ref, st_ref, ubuf_ref, *, rows, n_seqs, d_ff, n_tiles):
    j = pl.program_id(1)
    lo = _PAD_ROWS - (CONV_W - 1)
    stride = _PAD_ROWS + rows
    assert n_seqs == 1 or n_tiles == 1

    @pl.when(j == 0)
    def _():
        for b in range(n_seqs):
            ubuf_ref[b * stride + lo:b * stride + _PAD_ROWS, :] = prev_ref[b]

    if n_tiles > 1:
        @pl.when(j > 0)
        def _():
            ubuf_ref[0:_PAD_ROWS, :] = ubuf_ref[rows:rows + _PAD_ROWS, :]

    xf = x_ref[...]
    hn = (xf * _rms_scale(xf, xf.shape[-1]) * g_ref[...]).astype(BF16)
    u = jnp.dot(hn, wup_ref[...], preferred_element_type=F32)
    for b in range(n_seqs):
        ubuf_ref[b * stride + _PAD_ROWS:(b + 1) * stride, :] = u[b * rows:(b + 1) * rows, :]
    runs = []
    for b in range(n_seqs):
        c = cb_ref[...]
        for tap in range(CONV_W):
            first = b * stride + lo + tap
            c = c + ubuf_ref[first:first + rows, :] * cw_ref[tap:tap + 1, :]
        runs.append(c)
    c = runs[0] if n_seqs == 1 else jnp.concatenate(runs, axis=0)
    gate, val = c[:, :d_ff], c[:, d_ff:]
    act = (gate * jax.nn.sigmoid(gate) * val).astype(BF16)
    y_ref[...] = xf + jnp.dot(act, wdn_ref[...], preferred_element_type=F32)

    @pl.when(j == n_tiles - 1)
    def _():
        for b in range(n_seqs):
            st_ref[b] = ubuf_ref[(b + 1) * stride - (CONV_W - 1):(b + 1) * stride, :]


def _conv_ffn(x, g, w_up, cw, cb, w_dn, prev, *, batch, seq, tm):
    n, d = x.shape
    d_ff = w_dn.shape[0]
    n_seqs = max(tm // seq, 1)
    rows = tm // n_seqs
    nt = seq // rows
    row = pl.BlockSpec((tm, d), lambda b, j: (b * nt + j, 0))
    st = pl.BlockSpec((n_seqs, CONV_W - 1, 2 * d_ff), lambda b, j: (b, 0, 0))
    return pl.pallas_call(
        functools.partial(_ffn_kernel, rows=rows, n_seqs=n_seqs, d_ff=d_ff, n_tiles=nt),
        grid=(batch // n_seqs, nt),
        in_specs=[row, _const_spec(g.shape), _const_spec(w_up.shape), _const_spec(cw.shape),
                  _const_spec(cb.shape), _const_spec(w_dn.shape), st],
        out_specs=[row, st],
        out_shape=(jax.ShapeDtypeStruct((n, d), F32),
                   jax.ShapeDtypeStruct((batch, CONV_W - 1, 2 * d_ff), F32)),
        scratch_shapes=[pltpu.VMEM((n_seqs * (_PAD_ROWS + rows), 2 * d_ff), F32)],
        compiler_params=pltpu.CompilerParams(
            dimension_semantics=("arbitrary", "arbitrary"),
            vmem_limit_bytes=_vmem_limit(56 << 20)),
        name="conv_ffn",
    )(x, g, w_up, cw, cb, w_dn, prev)


def _pack_weights(w_in, w_q_up, w_kv_up, w_o_branch, w_out, w_ffn_up, w_ffn_down):
    head_width = sum(width for _, width in _IN_HEAD)
    w_head = jnp.pad(w_in[:, :_IN_HEAD_SRC_WIDTH].astype(BF16),
                     ((0, 0), (0, head_width - _IN_HEAD_SRC_WIDTH)))
    w_in_p = (w_head, w_in[:, _IN_HEAD_SRC_WIDTH:].astype(BF16))
    w_q_p = w_q_up.T.astype(BF16)
    wkv = w_kv_up.reshape(w_kv_up.shape[0], MLA_HEADS, QK_NOPE + V_HEAD)
    w_kv_p = jnp.concatenate([wkv[:, :, :QK_NOPE].reshape(wkv.shape[0], -1),
                              wkv[:, :, QK_NOPE:].reshape(wkv.shape[0], -1)], axis=1).astype(BF16)
    mla_width = MLA_HEADS * V_HEAD
    return (w_in_p, w_q_p, w_kv_p, w_o_branch[:mla_width].astype(BF16),
            w_o_branch[mla_width:].astype(BF16), w_out.astype(BF16),
            w_ffn_up.astype(BF16), w_ffn_down.astype(BF16))


def _layer(x, pos, past, w, *, batch, seq, tm, tq, ret_chunk, ffn_tm):
    n = x.shape[0]
    tabs64 = _rope_tables(pos, RET_DK // 2)
    tabs32 = _rope_tables(pos, QK_ROPE // 2)
    seq_tiles = max(seq // tm, 1)
    if seq < tm:
        tabs64 = tuple(np.tile(t, (tm // seq, 1)) for t in tabs64)
        tabs32 = tuple(np.tile(t, (tm // seq, 1)) for t in tabs32)
    qlat, ckv, kpe, rq, rk, rv, rg, ga, gb = _in_proj(
        x, w['w_in'], w['g_norm_mix'], w['g_q_lat'], w['g_kv_lat'], w['g_k_rope'],
        tabs64, tabs32, seq_tiles=seq_tiles, tm=tm)
    tabs32_t = tuple(np.ascontiguousarray(t[:, :QK_ROPE].T) for t in tabs32)
    bcast = lambda g: jnp.broadcast_to(g.reshape(-1, 1), (g.size, tm))
    qt = _q_up(qlat, w['w_q_up'], bcast(w['g_q_nope']), bcast(w['g_q_rope']), tabs32_t,
               seq_tiles=seq_tiles, tm=tm)
    if past is None:
        k_new, vt_new = _kv_up(ckv, kpe, w['w_kv_up'], w['g_k_nope'], tm=tm)
        attn = _prompt_attention(qt, k_new, vt_new, batch=batch, seq=seq, tq=tq)
        s0 = jnp.zeros((batch, RET_HEADS, RET_DK, RET_DV), F32)
        prev = jnp.zeros((batch, CONV_W - 1, w['w_ffn_up'].shape[1]), F32)
    else:
        lat_c, pe_c, s0, prev = past
        attn = _sample_attention(qt, lat_c, pe_c, ckv.reshape(batch, seq, -1),
                                 kpe.reshape(batch, seq, -1), w['w_kv_up'], w['g_k_nope'],
                                 batch=batch, new=seq).reshape(n, -1)
    ret, s_fin = _retention(rq, rk, rv, rg, w['g_ret_out'], s0,
                            batch=batch, seq=seq, chunk=ret_chunk)
    x1 = _merge(x, attn, ret, ga, gb, w['w_o_a'], w['w_o_b'], w['w_out'], tm=tm)
    y, conv_state = _conv_ffn(x1, w['g_norm_ffn'], w['w_ffn_up'], w['ffn_conv_w'],
                              w['ffn_conv_b'], w['w_ffn_down'], prev,
                              batch=batch, seq=seq, tm=ffn_tm)
    return y, (ckv, kpe, s_fin, conv_state)


def kernel(x_prompt, x_sample, cache_mla_latent, cache_mla_rope_key, state_retention, state_ffn_conv, g_norm_mix, w_in, g_q_lat, w_q_up, g_q_nope, g_q_rope, g_kv_lat, w_kv_up, g_k_nope, g_k_rope, g_ret_out, w_o_branch, w_out, g_norm_ffn, w_ffn_up, ffn_conv_w, ffn_conv_b, w_ffn_down):
    depth = w_in.shape[0]
    assert depth == 1, "single-layer trunk"
    bp, tp, d = x_prompt.shape
    bs, ts, _ = x_sample.shape
    past_len = cache_mla_latent.shape[2]
    (w_in_p, w_q_p, w_kv_p, w_o_a, w_o_b, w_out_p, w_up_p, w_dn_p) = _pack_weights(
        w_in[0], w_q_up[0], w_kv_up[0], w_o_branch[0], w_out[0], w_ffn_up[0], w_ffn_down[0])
    row = lambda g: g.reshape(1, -1).astype(F32)
    w = dict(
        w_in=w_in_p, w_q_up=w_q_p, w_kv_up=w_kv_p, w_o_a=w_o_a, w_o_b=w_o_b, w_out=w_out_p,
        w_ffn_up=w_up_p, w_ffn_down=w_dn_p,
        g_norm_mix=row(g_norm_mix[0]), g_q_lat=row(g_q_lat[0]), g_kv_lat=row(g_kv_lat[0]),
        g_k_rope=jnp.pad(row(g_k_rope[0]), ((0, 0), (0, LANES - QK_ROPE))),
        g_q_nope=g_q_nope[0].astype(F32), g_q_rope=g_q_rope[0].astype(F32),
        g_k_nope=row(g_k_nope[0]), g_ret_out=g_ret_out[0], g_norm_ffn=row(g_norm_ffn[0]),
        ffn_conv_w=ffn_conv_w[0], ffn_conv_b=row(ffn_conv_b[0]))

    yp, sp = _layer(x_prompt.reshape(bp * tp, d), np.arange(tp), None, w,
                    batch=bp, seq=tp, tm=512, tq=2048, ret_chunk=256, ffn_tm=256)
    past = (cache_mla_latent[0], cache_mla_rope_key[0], state_retention[0], state_ffn_conv[0])
    ys, ss = _layer(x_sample.reshape(bs * ts, d), past_len + np.arange(ts), past, w,
                    batch=bs, seq=ts, tm=bs * ts, tq=None, ret_chunk=ts, ffn_tm=bs * ts)

    def states(st, b, t):
        ckv, kpe, s_fin, conv = st
        return (ckv.reshape(1, b, t, -1), kpe.reshape(1, b, t, -1), s_fin[None], conv[None])

    return (yp.reshape(bp, tp, d), ys.reshape(bs, ts, d)) + states(sp, bp, tp) + states(ss, bs, ts)
```

```python
import functools
import math

import jax
import jax.numpy as jnp
import numpy as np
from jax import lax
from jax.experimental import pallas as pl
from jax.experimental.pallas import tpu as pltpu

F32 = jnp.float32
BF16 = jnp.bfloat16

EPS = 1e-6
NEG_INF = -1e30
ROPE_THETA = 10000.0
CHUNK = 64
LANES = 128
V7X_VMEM_BYTES = 64 * 1024 * 1024

MLA_HEADS = 8
QK_NOPE = 128
QK_ROPE = 64
QK_DIM = QK_NOPE + QK_ROPE
V_HEAD = 128
MLA_SCALE = QK_DIM ** -0.5
Q_PRESCALE = MLA_SCALE * math.log2(math.e)
RET_HEADS = 4
RET_DK = 128
RET_DV = 256
CONV_W = 3


def _vmem_limit(nbytes):
    return int(min(V7X_VMEM_BYTES - (4 << 20), max(32 << 20, nbytes)))


def _const_spec(shape):
    nd = len(shape)
    return pl.BlockSpec(shape, lambda *_: (0,) * nd, pipeline_mode=pl.Buffered(1))


def _rms_scale(z, width):
    return lax.rsqrt(jnp.sum(z * z, axis=-1, keepdims=True) * (1.0 / width) + EPS)


def _swap_halves_64(z):
    lane = lax.broadcasted_iota(jnp.int32, z.shape, 1)
    return jnp.where((lane % 64) < 32, pltpu.roll(z, 96, 1), pltpu.roll(z, 32, 1))


def _rope_tables(pos, half):
    freqs = ROPE_THETA ** (-np.arange(half, dtype=np.float64) / half)
    ang = pos.astype(np.float64)[:, None] * freqs[None, :]
    cos, sin = np.cos(ang), np.sin(ang)
    reps = LANES // (2 * half)
    return (np.tile(np.concatenate([cos, cos], -1), (1, reps)).astype(np.float32),
            np.tile(np.concatenate([-sin, sin], -1), (1, reps)).astype(np.float32))


_IN_HEAD = (('qlat', 768), ('kvl', 512), ('kpe', LANES))
_IN_TAIL = (('rq', 512), ('rk', 512), ('rv', 1024), ('rg', 1024), ('ga', 1024), ('gb', 1024))
_IN_HEAD_SRC_WIDTH = 768 + 512 + QK_ROPE


def _group_offsets(groups):
    offs, o = {}, 0
    for name, width in groups:
        offs[name] = (o, o + width)
        o += width
    return offs


_IN_OFFS = {**{n: (0,) + ab for n, ab in _group_offsets(_IN_HEAD).items()},
            **{n: (1,) + ab for n, ab in _group_offsets(_IN_TAIL).items()}}


def _inproj_kernel(x_ref, gmix_ref, wh_ref, wt_ref, gq_ref, gkv_ref, gkpe_ref,
                   c64_ref, s64_ref, c32_ref, s32_ref,
                   qlat_ref, ckv_ref, kpe_ref, rq_ref, rk_ref, rv_ref, rg_ref, ga_ref, gb_ref):
    xf = x_ref[...]
    h = (xf * _rms_scale(xf, xf.shape[-1]) * gmix_ref[...]).astype(BF16)

    def proj(name):
        part, a, b = _IN_OFFS[name]
        w_ref = (wh_ref, wt_ref)[part]
        return jnp.dot(h, w_ref[:, a:b], preferred_element_type=F32)

    z = proj('qlat')
    qlat_ref[...] = (z * _rms_scale(z, z.shape[-1]) * gq_ref[...]).astype(BF16)
    z = proj('kvl')
    ckv_ref[...] = z * _rms_scale(z, z.shape[-1]) * gkv_ref[...]
    z = proj('kpe')
    zn = z * _rms_scale(z, QK_ROPE) * gkpe_ref[...]
    kpe = zn * c32_ref[...] + _swap_halves_64(zn) * s32_ref[...]
    kpe_ref[...] = kpe[:, :QK_ROPE]
    c64, s64 = c64_ref[...], s64_ref[...]
    for name, ref, scale in (('rq', rq_ref, 1.0), ('rk', rk_ref, RET_DK ** -0.5)):
        z = proj(name)
        for hd in range(RET_HEADS):
            zh = z[:, hd * RET_DK:(hd + 1) * RET_DK]
            r = zh * c64 + pltpu.roll(zh, RET_DK // 2, 1) * s64
            if scale != 1.0:
                r = r * scale
            ref[:, hd * RET_DK:(hd + 1) * RET_DK] = r.astype(BF16)
    rv_ref[...] = proj('rv').astype(BF16)
    rg_ref[...] = proj('rg')
    ga_ref[...] = proj('ga')
    gb_ref[...] = proj('gb')


def _in_proj(x, w_in_parts, gmix, gq, gkv, gkpe, tabs64, tabs32, *, seq_tiles, tm):
    n, d = x.shape
    w_head, w_tail = w_in_parts
    row = lambda w: pl.BlockSpec((tm, w), lambda i: (i, 0))
    tab = pl.BlockSpec((tm, LANES), lambda i: (i % seq_tiles, 0))
    out_shapes = (
        jax.ShapeDtypeStruct((n, 768), BF16), jax.ShapeDtypeStruct((n, 512), F32),
        jax.ShapeDtypeStruct((n, QK_ROPE), F32), jax.ShapeDtypeStruct((n, 512), BF16),
        jax.ShapeDtypeStruct((n, 512), BF16), jax.ShapeDtypeStruct((n, 1024), BF16),
        jax.ShapeDtypeStruct((n, 1024), F32), jax.ShapeDtypeStruct((n, 1024), F32),
        jax.ShapeDtypeStruct((n, 1024), F32))
    return pl.pallas_call(
        _inproj_kernel,
        grid=(n // tm,),
        in_specs=[row(d), _const_spec(gmix.shape), _const_spec(w_head.shape),
                  _const_spec(w_tail.shape), _const_spec(gq.shape), _const_spec(gkv.shape),
                  _const_spec(gkpe.shape), tab, tab, tab, tab],
        out_specs=[row(s.shape[1]) for s in out_shapes],
        out_shape=out_shapes,
        compiler_params=pltpu.CompilerParams(
            dimension_semantics=("arbitrary",), vmem_limit_bytes=_vmem_limit(56 << 20)),
        name="in_proj",
    )(x, gmix, w_head, w_tail, gq, gkv, gkpe, *tabs64, *tabs32)


_QUP_GROUPS = 2


def _qup_kernel(ql_ref, wt_ref, gn_ref, gr_ref, c_ref, s_ref, qt_ref):
    gn, gr, cos, sin = gn_ref[...], gr_ref[...], c_ref[...], s_ref[...]
    half = QK_ROPE // 2
    ql = ql_ref[...]

    group = MLA_HEADS // _QUP_GROUPS

    def project(g):
        return lax.dot_general(wt_ref[g * group * QK_DIM:(g + 1) * group * QK_DIM, :], ql,
                               (((1,), (1,)), ((), ())), preferred_element_type=F32)

    def finish(g, zt):
        for j in range(group):
            hd = g * group + j
            zn = zt[j * QK_DIM:j * QK_DIM + QK_NOPE, :]
            inv = lax.rsqrt(jnp.sum(zn * zn, axis=0, keepdims=True) * (1.0 / QK_NOPE) + EPS)
            qt_ref[hd, :QK_NOPE, :] = (zn * (inv * Q_PRESCALE) * gn).astype(BF16)
            zr = zt[j * QK_DIM + QK_NOPE:(j + 1) * QK_DIM, :]
            inv = lax.rsqrt(jnp.sum(zr * zr, axis=0, keepdims=True) * (1.0 / QK_ROPE) + EPS)
            z = zr * (inv * Q_PRESCALE) * gr
            swapped = jnp.concatenate([z[half:], z[:half]], axis=0)
            qt_ref[hd, QK_NOPE:, :] = (z * cos + swapped * sin).astype(BF16)

    zt = project(0)
    for g in range(_QUP_GROUPS):
        zt_next = project(g + 1) if g + 1 < _QUP_GROUPS else None
        finish(g, zt)
        zt = zt_next


def _q_up(qlat, w_qt, gn, gr, tabs32_t, *, seq_tiles, tm):
    n = qlat.shape[0]
    tab = pl.BlockSpec((QK_ROPE, tm), lambda i: (0, i % seq_tiles))
    return pl.pallas_call(
        _qup_kernel,
        grid=(n // tm,),
        in_specs=[pl.BlockSpec((tm, qlat.shape[1]), lambda i: (i, 0)), _const_spec(w_qt.shape),
                  _const_spec(gn.shape), _const_spec(gr.shape), tab, tab],
        out_specs=pl.BlockSpec((MLA_HEADS, QK_DIM, tm), lambda i: (0, 0, i)),
        out_shape=jax.ShapeDtypeStruct((MLA_HEADS, QK_DIM, n), BF16),
        compiler_params=pltpu.CompilerParams(
            dimension_semantics=("arbitrary",), vmem_limit_bytes=_vmem_limit(40 << 20)),
        name="q_up",
    )(qlat, w_qt, gn, gr, *tabs32_t)


_ONES_ROWS = 16


def _kvup_kernel(c_ref, kpe_ref, w_ref, gk_ref, k_ref, vt_ref):
    c = c_ref[...].astype(BF16)
    kpe = kpe_ref[...].astype(BF16)
    gk = gk_ref[...]
    kw = MLA_HEADS * QK_NOPE
    n_pairs = MLA_HEADS // 2

    def project(pair):
        zk = jnp.dot(c, w_ref[:, pair * 256:(pair + 1) * 256], preferred_element_type=F32)
        zv = jnp.dot(c, w_ref[:, kw + pair * 256:kw + (pair + 1) * 256],
                     preferred_element_type=F32)
        return zk, zv

    def finish(pair, zk, zv):
        for sub in range(2):
            hd = 2 * pair + sub
            zh = zk[:, sub * QK_NOPE:(sub + 1) * QK_NOPE]
            k_ref[hd, :, :QK_NOPE] = (zh * _rms_scale(zh, QK_NOPE) * gk).astype(BF16)
            k_ref[hd, :, QK_NOPE:] = kpe
            vh = zv[:, sub * V_HEAD:(sub + 1) * V_HEAD]
            vt_ref[hd, :V_HEAD, :] = vh.T.astype(BF16)
            vt_ref[hd, V_HEAD:, :] = jnp.ones((_ONES_ROWS, vh.shape[0]), BF16)

    z = project(0)
    for pair in range(n_pairs):
        z_next = project(pair + 1) if pair + 1 < n_pairs else None
        finish(pair, *z)
        z = z_next


def _kv_up(ckv, kpe, w_kv, gk, *, tm):
    n = ckv.shape[0]
    v_spec = pl.BlockSpec((MLA_HEADS, V_HEAD + _ONES_ROWS, tm), lambda i: (0, 0, i))
    v_shape = (MLA_HEADS, V_HEAD + _ONES_ROWS, n)
    return pl.pallas_call(
        _kvup_kernel,
        grid=(n // tm,),
        in_specs=[pl.BlockSpec((tm, ckv.shape[1]), lambda i: (i, 0)),
                  pl.BlockSpec((tm, QK_ROPE), lambda i: (i, 0)),
                  _const_spec(w_kv.shape), _const_spec(gk.shape)],
        out_specs=[pl.BlockSpec((MLA_HEADS, tm, QK_DIM), lambda i: (0, i, 0)), v_spec],
        out_shape=(jax.ShapeDtypeStruct((MLA_HEADS, n, QK_DIM), BF16),
                   jax.ShapeDtypeStruct(v_shape, BF16)),
        compiler_params=pltpu.CompilerParams(
            dimension_semantics=("arbitrary",), vmem_limit_bytes=_vmem_limit(40 << 20)),
        name="kv_up",
    )(ckv, kpe, w_kv, gk)


def _qk(q, k):
    return lax.dot_general(q, k, (((1,), (1,)), ((), ())), preferred_element_type=F32)


_STRIP = 256
_ATTN_KEY_BLOCK = 512


def _prompt_attn_kernel(qt_ref, k_ref, vt_ref, o_ref, s_ref, smax_ref, p_ref, a_ref, m_ref,
                        acc_ref, *, tq, tk):
    n_tiles = qt_ref.shape[2] // tq
    n_strips = tq // _STRIP
    n_diag = tq // tk
    assert n_diag % 2 == 0 and tk % _STRIP == 0 and _STRIP % CHUNK == 0
    all_strips = tuple(range(n_strips))
    cols = lambda c: slice(c * _STRIP, (c + 1) * _STRIP)
    n_seen = lambda d, c: max(0, min(tk, (c + 1) * _STRIP - d * tk))
    seen = lambda d: tuple(c for c in all_strips if n_seen(d, c) > 0)
    unmasked = lambda d, c: c * _STRIP >= (d + 1) * tk

    def scores(i, b, slot, c, nk=tk):
        start = pl.multiple_of(b * tk, tk)
        q_start = pl.multiple_of(i * tq + c * _STRIP, _STRIP)
        s = jnp.dot(k_ref[0, pl.ds(start, nk), :], qt_ref[0, :, pl.ds(q_start, _STRIP)],
                    preferred_element_type=F32)
        s_ref[slot, :nk, cols(c)] = s
        smax_ref[slot, :, cols(c)] = jnp.max(s, axis=0, keepdims=True)

    def softmax(slot, c, diag_block=None):
        nk = tk if diag_block is None else n_seen(diag_block, c)
        s = s_ref[slot, :nk, cols(c)]
        if diag_block is None or unmasked(diag_block, c):
            s_max = smax_ref[slot, :, cols(c)]
        else:
            kc = (lax.broadcasted_iota(jnp.int32, s.shape, 0) + diag_block * tk) // CHUNK
            qc = (lax.broadcasted_iota(jnp.int32, s.shape, 1) + c * _STRIP) // CHUNK
            s = jnp.where(kc <= qc, s, NEG_INF)
            s_max = jnp.max(s, axis=0, keepdims=True)
        m_prev = m_ref[:, cols(c)]
        m_new = jnp.maximum(m_prev, s_max)
        alpha = jnp.exp2(m_prev - m_new)
        p = jnp.exp2(s - m_new)
        m_ref[:, cols(c)] = m_new
        a_ref[slot, :, cols(c)] = alpha
        p_ref[slot, :nk, cols(c)] = p.astype(BF16)

    def values(b, slot, c, nk=tk):
        start = pl.multiple_of(jnp.maximum(b, 0) * tk, tk)
        pv = jnp.dot(vt_ref[0, :, pl.ds(start, nk)], p_ref[slot, :nk, cols(c)],
                     preferred_element_type=F32)
        acc_ref[:, cols(c)] = acc_ref[:, cols(c)] * a_ref[slot, :, cols(c)] + pv

    def first_scores(i):
        for c in all_strips:
            scores(i, 0, 0, c)

    def reset_state():
        m_ref[...] = jnp.full(m_ref.shape, NEG_INF, F32)
        acc_ref[...] = jnp.zeros(acc_ref.shape, F32)
        p_ref[1] = jnp.zeros(p_ref.shape[1:], BF16)
        a_ref[1] = jnp.ones(a_ref.shape[1:], F32)

    def tile(i, carry):
        def pair(jj, carry):
            b = 2 * jj
            for c in all_strips:
                scores(i, b + 1, 1, c)
            for c in all_strips:
                values(b - 1, 1, c)
            for c in all_strips:
                softmax(0, c)
            for c in all_strips:
                scores(i, b + 2, 0, c)
            for c in all_strips:
                values(b, 0, c)
            for c in all_strips:
                softmax(1, c)
            return carry

        lax.fori_loop(0, i * (n_diag // 2), pair, 0)
        b0 = n_diag * i
        for t in range(n_diag + 1):
            if 1 <= t + 1 < n_diag:
                for c in seen(t + 1):
                    scores(i, b0 + t + 1, (t + 1) % 2, c, n_seen(t + 1, c))
            if t == n_diag - 1:
                first_scores(jnp.minimum(i + 1, n_tiles - 1))
            if t == 0:
                for c in all_strips:
                    values(b0 - 1, 1, c)
            else:
                for c in seen(t - 1):
                    values(b0 + t - 1, (t - 1) % 2, c, n_seen(t - 1, c))
            if t < n_diag:
                for c in seen(t):
                    softmax(t % 2, c, diag_block=t)
        o = acc_ref[:V_HEAD, :] / acc_ref[V_HEAD:V_HEAD + 1, :]
        o_ref[pl.ds(pl.multiple_of(i * tq, tq), tq), :] = o.T.astype(o_ref.dtype)
        reset_state()
        return carry

    first_scores(0)
    reset_state()
    lax.fori_loop(0, n_tiles, tile, 0)


def _prompt_attention(qt, k, vt, *, batch, seq, tq):
    heads, _, n = qt.shape
    tk = _ATTN_KEY_BLOCK
    return pl.pallas_call(
        functools.partial(_prompt_attn_kernel, tq=tq, tk=tk),
        grid=(batch, heads),
        in_specs=[pl.BlockSpec((1, QK_DIM, seq), lambda b, h: (h, 0, b)),
                  pl.BlockSpec((1, seq, QK_DIM), lambda b, h: (h, b, 0)),
                  pl.BlockSpec((1, vt.shape[1], seq), lambda b, h: (h, 0, b))],
        out_specs=pl.BlockSpec((seq, V_HEAD), lambda b, h: (b, h)),
        out_shape=jax.ShapeDtypeStruct((n, heads * V_HEAD), BF16),
        scratch_shapes=[pltpu.VMEM((2, tk, tq), F32), pltpu.VMEM((2, 1, tq), F32),
                        pltpu.VMEM((2, tk, tq), BF16),
                        pltpu.VMEM((2, 1, tq), F32), pltpu.VMEM((1, tq), F32),
                        pltpu.VMEM((vt.shape[1], tq), F32)],
        compiler_params=pltpu.CompilerParams(
            dimension_semantics=("arbitrary", "arbitrary"),
            vmem_limit_bytes=_vmem_limit(48 << 20)),
        name="prompt_attention",
    )(qt, k, vt)


def _sample_attn_kernel(qbd_ref, qr_ref, latc_ref, kpec_ref, latn_ref, kpen_ref, w_ref, gk_ref,
                        o_ref, *, new):
    gk = gk_ref[...]
    kw = MLA_HEADS * QK_NOPE

    def scores(lat_ref, kpe_ref):
        c = lat_ref[0].astype(BF16)
        k_heads = []
        for pair in range(MLA_HEADS // 2):
            zk = jnp.dot(c, w_ref[:, pair * 256:(pair + 1) * 256], preferred_element_type=F32)
            for sub in range(2):
                zh = zk[:, sub * QK_NOPE:(sub + 1) * QK_NOPE]
                k_heads.append((zh * _rms_scale(zh, QK_NOPE) * gk).astype(BF16))
        k_all = jnp.concatenate(k_heads, axis=1)
        s = (jnp.dot(k_all, qbd_ref[0], preferred_element_type=F32)
             + jnp.dot(kpe_ref[0].astype(BF16), qr_ref[0], preferred_element_type=F32))
        return s, c

    s_c, c_c = scores(latc_ref, kpec_ref)
    s_n, c_n = scores(latn_ref, kpen_ref)
    m = jnp.maximum(jnp.max(s_c, axis=0, keepdims=True), jnp.max(s_n, axis=0, keepdims=True))
    p_c = jnp.exp2(s_c - m).astype(BF16)
    p_n = jnp.exp2(s_n - m).astype(BF16)
    tn = (((0,), (0,)), ((), ()))
    ctx = (lax.dot_general(p_c, c_c, tn, preferred_element_type=F32)
           + lax.dot_general(p_n, c_n, tn, preferred_element_type=F32))
    den = (lax.dot_general(p_c, jnp.ones((p_c.shape[0], V_HEAD), BF16), tn,
                           preferred_element_type=F32)
           + lax.dot_general(p_n, jnp.ones((p_n.shape[0], V_HEAD), BF16), tn,
                             preferred_element_type=F32))
    ctx = ctx.astype(BF16)
    for hd in range(MLA_HEADS):
        rows = slice(hd * new, (hd + 1) * new)
        o = jnp.dot(ctx[rows, :], w_ref[:, kw + hd * V_HEAD:kw + (hd + 1) * V_HEAD],
                    preferred_element_type=F32)
        o_ref[0, :, hd * V_HEAD:(hd + 1) * V_HEAD] = (o / den[rows, :]).astype(o_ref.dtype)


def _sample_attention(qt, lat_c, kpe_c, lat_n, kpe_n, w_kv, gk, *, batch, new):
    heads = qt.shape[0]
    q4 = qt.reshape(heads, QK_DIM, batch, new)
    qn = jnp.transpose(q4[:, :QK_NOPE], (2, 0, 1, 3))
    qbd = jnp.einsum('bhdj,hg->bhdgj', qn, jnp.eye(heads, dtype=qn.dtype))
    qbd = qbd.reshape(batch, heads * QK_NOPE, heads * new)
    qr = jnp.transpose(q4[:, QK_NOPE:], (2, 1, 0, 3)).reshape(batch, QK_ROPE, heads * new)
    per_batch = lambda a: pl.BlockSpec((1,) + a.shape[1:], lambda b: (b, 0, 0))
    return pl.pallas_call(
        functools.partial(_sample_attn_kernel, new=new),
        grid=(batch,),
        in_specs=[per_batch(qbd), per_batch(qr), per_batch(lat_c), per_batch(kpe_c),
                  per_batch(lat_n), per_batch(kpe_n), _const_spec(w_kv.shape),
                  _const_spec(gk.shape)],
        out_specs=pl.BlockSpec((1, new, heads * V_HEAD), lambda b: (b, 0, 0)),
        out_shape=jax.ShapeDtypeStruct((batch, new, heads * V_HEAD), BF16),
        compiler_params=pltpu.CompilerParams(
            dimension_semantics=("arbitrary",), vmem_limit_bytes=_vmem_limit(48 << 20)),
        name="sample_attention",
    )(qbd, qr, lat_c, kpe_c, lat_n, kpe_n, w_kv, gk)


_RET_SUB_CHUNKS = 4


def _retention_kernel(q_ref, k_ref, v_ref, rg_ref, dec_ref, qd_ref, kd_ref, g_ref, s0_ref,
                      o_ref, sfin_ref, state_ref, *, chunk, n_sub, n_steps):
    c = pl.program_id(1)

    @pl.when(c == 0)
    def _():
        state_ref[...] = s0_ref[0]

    heads = range(RET_HEADS)
    kcols = [slice(hd * RET_DK, (hd + 1) * RET_DK) for hd in heads]
    vcols = [slice(hd * RET_DV, (hd + 1) * RET_DV) for hd in heads]
    qd = [qd_ref[hd] for hd in heads]

    def recur(j, state):
        rows = slice(j * chunk, (j + 1) * chunk)
        q = [q_ref[rows, kcols[hd]] for hd in heads]
        k = [k_ref[rows, kcols[hd]] for hd in heads]
        v = [v_ref[rows, vcols[hd]] for hd in heads]
        s = [_qk(q[hd], k[hd]) for hd in heads]
        cross = [jnp.dot(q[hd], state[hd].astype(BF16), preferred_element_type=F32)
                 for hd in heads]
        s = [(s[hd] * dec_ref[hd]).astype(BF16) for hd in heads]
        k_dec = [(k[hd].astype(F32) * kd_ref[hd]).astype(BF16) for hd in heads]
        inner = [jnp.dot(s[hd], v[hd], preferred_element_type=F32) for hd in heads]
        kv = [lax.dot_general(k_dec[hd], v[hd], (((0,), (0,)), ((), ())),
                              preferred_element_type=F32) for hd in heads]
        new_state = [state[hd] * qd[hd][chunk - 1:chunk, :] + kv[hd] for hd in heads]
        return [inner[hd] + cross[hd] * qd[hd] for hd in heads], new_state

    def finish(j, o):
        rows = slice(j * chunk, (j + 1) * chunk)
        for hd in heads:
            mu = jnp.mean(o[hd], axis=-1, keepdims=True)
            oc = o[hd] - mu
            var = jnp.mean(oc * oc, axis=-1, keepdims=True)
            y = oc * lax.rsqrt(var + EPS) * g_ref[hd]
            rg = rg_ref[rows, vcols[hd]]
            o_ref[rows, vcols[hd]] = (rg * jax.nn.sigmoid(rg) * y).astype(o_ref.dtype)

    state = [state_ref[hd] for hd in heads]
    o_prev = None
    for j in range(n_sub):
        o, state = recur(j, state)
        if o_prev is not None:
            finish(j - 1, o_prev)
        o_prev = o
    finish(n_sub - 1, o_prev)
    for hd in heads:
        state_ref[hd] = state[hd]

    @pl.when(c == n_steps - 1)
    def _():
        sfin_ref[0] = state_ref[...]


def _retention_tables(chunk):
    lg = jnp.log1p(-(2.0 ** (-5.0 - jnp.arange(RET_HEADS, dtype=F32))))
    idx = jnp.arange(chunk, dtype=F32)
    diff = idx[:, None] - idx[None, :]
    decay = jnp.where(diff[None] >= 0,
                      jnp.exp(jnp.maximum(diff, 0.0)[None] * lg[:, None, None]), 0.0)
    q_dec = jnp.exp((idx + 1.0)[None, :] * lg[:, None])[..., None]
    k_dec = jnp.exp((chunk - 1.0 - idx)[None, :] * lg[:, None])[..., None]
    return decay, q_dec, k_dec


def _retention(rq, rk, rv, rg, g_ret, s0, *, batch, seq, chunk):
    n = rq.shape[0]
    n_sub = min(_RET_SUB_CHUNKS, seq // chunk)
    rows = n_sub * chunk
    ns = seq // rows
    decay, q_dec, k_dec = _retention_tables(chunk)
    tok = lambda w: pl.BlockSpec((rows, w), lambda b, c: (b * ns + c, 0))
    state_spec = pl.BlockSpec((1, RET_HEADS, RET_DK, RET_DV), lambda b, c: (b, 0, 0, 0))
    g_ret = g_ret.reshape(RET_HEADS, 1, RET_DV)
    return pl.pallas_call(
        functools.partial(_retention_kernel, chunk=chunk, n_sub=n_sub, n_steps=ns),
        grid=(batch, ns),
        in_specs=[tok(RET_HEADS * RET_DK), tok(RET_HEADS * RET_DK), tok(RET_HEADS * RET_DV),
                  tok(RET_HEADS * RET_DV), _const_spec(decay.shape), _const_spec(q_dec.shape),
                  _const_spec(k_dec.shape), _const_spec(g_ret.shape), state_spec],
        out_specs=[tok(RET_HEADS * RET_DV), state_spec],
        out_shape=(jax.ShapeDtypeStruct((n, RET_HEADS * RET_DV), BF16),
                   jax.ShapeDtypeStruct((batch, RET_HEADS, RET_DK, RET_DV), F32)),
        scratch_shapes=[pltpu.VMEM((RET_HEADS, RET_DK, RET_DV), F32)],
        compiler_params=pltpu.CompilerParams(dimension_semantics=("arbitrary", "arbitrary")),
        name="retention",
    )(rq, rk, rv, rg, decay, q_dec, k_dec, g_ret, s0)


def _merge_kernel(x_ref, a_ref, r_ref, ga_ref, gb_ref, wa_ref, wb_ref, wo_ref, o_ref):
    a_d = jnp.dot(a_ref[...], wa_ref[...], preferred_element_type=F32)
    r_d = jnp.dot(r_ref[...], wb_ref[...], preferred_element_type=F32)
    merged = jax.nn.sigmoid(ga_ref[...]) * a_d + jax.nn.sigmoid(gb_ref[...]) * r_d
    o_ref[...] = x_ref[...] + jnp.dot(merged.astype(BF16), wo_ref[...],
                                      preferred_element_type=F32)


def _merge(x, attn, ret, ga, gb, wa, wb, wo, *, tm):
    n, d = x.shape
    row = pl.BlockSpec((tm, d), lambda i: (i, 0))
    return pl.pallas_call(
        _merge_kernel,
        grid=(n // tm,),
        in_specs=[row, row, row, row, row,
                  _const_spec(wa.shape), _const_spec(wb.shape), _const_spec(wo.shape)],
        out_specs=row,
        out_shape=jax.ShapeDtypeStruct((n, d), F32),
        compiler_params=pltpu.CompilerParams(
            dimension_semantics=("arbitrary",), vmem_limit_bytes=_vmem_limit(40 << 20)),
        name="merge",
    )(x, attn, ret, ga, gb, wa, wb, wo)


_PAD_ROWS = 8


def _ffn_kernel(x_ref, g_ref, wup_ref, cw_ref, cb_ref, wdn_ref, prev_ref,
                y_ref, st_ref, ubuf_ref, *, rows, n_seqs, d_ff, n_tiles):
    j = pl.program_id(1)
    lo = _PAD_ROWS - (CONV_W - 1)
    stride = _PAD_ROWS + rows
    assert n_seqs == 1 or n_tiles == 1

    @pl.when(j == 0)
    def _():
        for b in range(n_seqs):
            ubuf_ref[b * stride + lo:b * stride + _PAD_ROWS, :] = prev_ref[b]

    if n_tiles > 1:
        @pl.when(j > 0)
        def _():
            ubuf_ref[0:_PAD_ROWS, :] = ubuf_ref[rows:rows + _PAD_ROWS, :]

    xf = x_ref[...]
    hn = (xf * _rms_scale(xf, xf.shape[-1]) * g_ref[...]).astype(BF16)
    u = jnp.dot(hn, wup_ref[...], preferred_element_type=F32)
    for b in range(n_seqs):
        ubuf_ref[b * stride + _PAD_ROWS:(b + 1) * stride, :] = u[b * rows:(b + 1) * rows, :]
    runs = []
    for b in range(n_seqs):
        c = cb_ref[...]
        for tap in range(CONV_W):
            first = b * stride + lo + tap
            c = c + ubuf_ref[first:first + rows, :] * cw_ref[tap:tap + 1, :]
        runs.append(c)
    c = runs[0] if n_seqs == 1 else jnp.concatenate(runs, axis=0)
    gate, val = c[:, :d_ff], c[:, d_ff:]
    act = (gate * jax.nn.sigmoid(gate) * val).astype(BF16)
    y_ref[...] = xf + jnp.dot(act, wdn_ref[...], preferred_element_type=F32)

    @pl.when(j == n_tiles - 1)
    def _():
        for b in range(n_seqs):
            st_ref[b] = ubuf_ref[(b + 1) * stride - (CONV_W - 1):(b + 1) * stride, :]


def _conv_ffn(x, g, w_up, cw, cb, w_dn, prev, *, batch, seq, tm):
    n, d = x.shape
    d_ff = w_dn.shape[0]
    n_seqs = max(tm // seq, 1)
    rows = tm // n_seqs
    nt = seq // rows
    row = pl.BlockSpec((tm, d), lambda b, j: (b * nt + j, 0))
    st = pl.BlockSpec((n_seqs, CONV_W - 1, 2 * d_ff), lambda b, j: (b, 0, 0))
    return pl.pallas_call(
        functools.partial(_ffn_kernel, rows=rows, n_seqs=n_seqs, d_ff=d_ff, n_tiles=nt),
        grid=(batch // n_seqs, nt),
        in_specs=[row, _const_spec(g.shape), _const_spec(w_up.shape), _const_spec(cw.shape),
                  _const_spec(cb.shape), _const_spec(w_dn.shape), st],
        out_specs=[row, st],
        out_shape=(jax.ShapeDtypeStruct((n, d), F32),
                   jax.ShapeDtypeStruct((batch, CONV_W - 1, 2 * d_ff), F32)),
        scratch_shapes=[pltpu.VMEM((n_seqs * (_PAD_ROWS + rows), 2 * d_ff), F32)],
        compiler_params=pltpu.CompilerParams(
            dimension_semantics=("arbitrary", "arbitrary"),
            vmem_limit_bytes=_vmem_limit(56 << 20)),
        name="conv_ffn",
    )(x, g, w_up, cw, cb, w_dn, prev)


def _pack_weights(w_in, w_q_up, w_kv_up, w_o_branch, w_out, w_ffn_up, w_ffn_down):
    head_width = sum(width for _, width in _IN_HEAD)
    w_head = jnp.pad(w_in[:, :_IN_HEAD_SRC_WIDTH].astype(BF16),
                     ((0, 0), (0, head_width - _IN_HEAD_SRC_WIDTH)))
    w_in_p = (w_head, w_in[:, _IN_HEAD_SRC_WIDTH:].astype(BF16))
    w_q_p = w_q_up.T.astype(BF16)
    wkv = w_kv_up.reshape(w_kv_up.shape[0], MLA_HEADS, QK_NOPE + V_HEAD)
    w_kv_p = jnp.concatenate([wkv[:, :, :QK_NOPE].reshape(wkv.shape[0], -1),
                              wkv[:, :, QK_NOPE:].reshape(wkv.shape[0], -1)], axis=1).astype(BF16)
    mla_width = MLA_HEADS * V_HEAD
    return (w_in_p, w_q_p, w_kv_p, w_o_branch[:mla_width].astype(BF16),
            w_o_branch[mla_width:].astype(BF16), w_out.astype(BF16),
            w_ffn_up.astype(BF16), w_ffn_down.astype(BF16))


def _layer(x, pos, past, w, *, batch, seq, tm, tq, ret_chunk, ffn_tm):
    n = x.shape[0]
    tabs64 = _rope_tables(pos, RET_DK // 2)
    tabs32 = _rope_tables(pos, QK_ROPE // 2)
    seq_tiles = max(seq // tm, 1)
    if seq < tm:
        tabs64 = tuple(np.tile(t, (tm // seq, 1)) for t in tabs64)
        tabs32 = tuple(np.tile(t, (tm // seq, 1)) for t in tabs32)
    qlat, ckv, kpe, rq, rk, rv, rg, ga, gb = _in_proj(
        x, w['w_in'], w['g_norm_mix'], w['g_q_lat'], w['g_kv_lat'], w['g_k_rope'],
        tabs64, tabs32, seq_tiles=seq_tiles, tm=tm)
    tabs32_t = tuple(np.ascontiguousarray(t[:, :QK_ROPE].T) for t in tabs32)
    bcast = lambda g: jnp.broadcast_to(g.reshape(-1, 1), (g.size, tm))
    qt = _q_up(qlat, w['w_q_up'], bcast(w['g_q_nope']), bcast(w['g_q_rope']), tabs32_t,
               seq_tiles=seq_tiles, tm=tm)
    if past is None:
        k_new, vt_new = _kv_up(ckv, kpe, w['w_kv_up'], w['g_k_nope'], tm=2 * tm)
        attn = _prompt_attention(qt, k_new, vt_new, batch=batch, seq=seq, tq=tq)
        s0 = jnp.zeros((batch, RET_HEADS, RET_DK, RET_DV), F32)
        prev = jnp.zeros((batch, CONV_W - 1, w['w_ffn_up'].shape[1]), F32)
    else:
        lat_c, pe_c, s0, prev = past
        attn = _sample_attention(qt, lat_c, pe_c, ckv.reshape(batch, seq, -1),
                                 kpe.reshape(batch, seq, -1), w['w_kv_up'], w['g_k_nope'],
                                 batch=batch, new=seq).reshape(n, -1)
    ret, s_fin = _retention(rq, rk, rv, rg, w['g_ret_out'], s0,
                            batch=batch, seq=seq, chunk=ret_chunk)
    x1 = _merge(x, attn, ret, ga, gb, w['w_o_a'], w['w_o_b'], w['w_out'], tm=tm)
    y, conv_state = _conv_ffn(x1, w['g_norm_ffn'], w['w_ffn_up'], w['ffn_conv_w'],
                              w['ffn_conv_b'], w['w_ffn_down'], prev,
                              batch=batch, seq=seq, tm=ffn_tm)
    return y, (ckv, kpe, s_fin, conv_state)


def kernel(x_prompt, x_sample, cache_mla_latent, cache_mla_rope_key, state_retention, state_ffn_conv, g_norm_mix, w_in, g_q_lat, w_q_up, g_q_nope, g_q_rope, g_kv_lat, w_kv_up, g_k_nope, g_k_rope, g_ret_out, w_o_branch, w_out, g_norm_ffn, w_ffn_up, ffn_conv_w, ffn_conv_b, w_ffn_down):
    depth = w_in.shape[0]
    assert depth == 1, "single-layer trunk"
    bp, tp, d = x_prompt.shape
    bs, ts, _ = x_sample.shape
    past_len = cache_mla_latent.shape[2]
    (w_in_p, w_q_p, w_kv_p, w_o_a, w_o_b, w_out_p, w_up_p, w_dn_p) = _pack_weights(
        w_in[0], w_q_up[0], w_kv_up[0], w_o_branch[0], w_out[0], w_ffn_up[0], w_ffn_down[0])
    row = lambda g: g.reshape(1, -1).astype(F32)
    w = dict(
        w_in=w_in_p, w_q_up=w_q_p, w_kv_up=w_kv_p, w_o_a=w_o_a, w_o_b=w_o_b, w_out=w_out_p,
        w_ffn_up=w_up_p, w_ffn_down=w_dn_p,
        g_norm_mix=row(g_norm_mix[0]), g_q_lat=row(g_q_lat[0]), g_kv_lat=row(g_kv_lat[0]),
        g_k_rope=jnp.pad(row(g_k_rope[0]), ((0, 0), (0, LANES - QK_ROPE))),
        g_q_nope=g_q_nope[0].astype(F32), g_q_rope=g_q_rope[0].astype(F32),
        g_k_nope=row(g_k_nope[0]), g_ret_out=g_ret_out[0], g_norm_ffn=row(g_norm_ffn[0]),
        ffn_conv_w=ffn_conv_w[0], ffn_conv_b=row(ffn_conv_b[0]))

    yp, sp = _layer(x_prompt.reshape(bp * tp, d), np.arange(tp), None, w,
                    batch=bp, seq=tp, tm=512, tq=2048, ret_chunk=256, ffn_tm=512)
    past = (cache_mla_latent[0], cache_mla_rope_key[0], state_retention[0], state_ffn_conv[0])
    ys, ss = _layer(x_sample.reshape(bs * ts, d), past_len + np.arange(ts), past, w,
                    batch=bs, seq=ts, tm=bs * ts, tq=None, ret_chunk=ts, ffn_tm=bs * ts)

    def states(st, b, t):
        ckv, kpe, s_fin, conv = st
        return (ckv.reshape(1, b, t, -1), kpe.reshape(1, b, t, -1), s_fin[None], conv[None])

    return (yp.reshape(bp, tp, d), ys.reshape(bs, ts, d)) + states(sp, bp, tp) + states(ss, bs, ts)
```

```python
import functools
import math

import jax
import jax.numpy as jnp
import numpy as np
from jax import lax
from jax.experimental import pallas as pl
from jax.experimental.pallas import tpu as pltpu

F32 = jnp.float32
BF16 = jnp.bfloat16

EPS = 1e-6
NEG_INF = -1e30
ROPE_THETA = 10000.0
CHUNK = 64
LANES = 128
V7X_VMEM_BYTES = 64 * 1024 * 1024

MLA_HEADS = 8
QK_NOPE = 128
QK_ROPE = 64
QK_DIM = QK_NOPE + QK_ROPE
V_HEAD = 128
MLA_SCALE = QK_DIM ** -0.5
Q_PRESCALE = MLA_SCALE * math.log2(math.e)
RET_HEADS = 4
RET_DK = 128
RET_DV = 256
CONV_W = 3


def _vmem_limit(nbytes):
    return int(min(V7X_VMEM_BYTES - (4 << 20), max(32 << 20, nbytes)))


def _const_spec(shape):
    nd = len(shape)
    return pl.BlockSpec(shape, lambda *_: (0,) * nd, pipeline_mode=pl.Buffered(1))


def _rms_scale(z, width):
    return lax.rsqrt(jnp.sum(z * z, axis=-1, keepdims=True) * (1.0 / width) + EPS)


def _swap_halves_64(z):
    lane = lax.broadcasted_iota(jnp.int32, z.shape, 1)
    return jnp.where((lane % 64) < 32, pltpu.roll(z, 96, 1), pltpu.roll(z, 32, 1))


def _rope_tables(pos, half):
    freqs = ROPE_THETA ** (-np.arange(half, dtype=np.float64) / half)
    ang = pos.astype(np.float64)[:, None] * freqs[None, :]
    cos, sin = np.cos(ang), np.sin(ang)
    reps = LANES // (2 * half)
    return (np.tile(np.concatenate([cos, cos], -1), (1, reps)).astype(np.float32),
            np.tile(np.concatenate([-sin, sin], -1), (1, reps)).astype(np.float32))


_IN_HEAD = (('qlat', 768), ('kvl', 512), ('kpe', LANES))
_IN_TAIL = (('rq', 512), ('rk', 512), ('rv', 1024), ('rg', 1024), ('ga', 1024), ('gb', 1024))
_IN_HEAD_SRC_WIDTH = 768 + 512 + QK_ROPE


def _group_offsets(groups):
    offs, o = {}, 0
    for name, width in groups:
        offs[name] = (o, o + width)
        o += width
    return offs


_IN_OFFS = {**{n: (0,) + ab for n, ab in _group_offsets(_IN_HEAD).items()},
            **{n: (1,) + ab for n, ab in _group_offsets(_IN_TAIL).items()}}


def _inproj_kernel(x_ref, gmix_ref, wh_ref, wt_ref, gq_ref, gkv_ref, gkpe_ref,
                   c64_ref, s64_ref, c32_ref, s32_ref,
                   qlat_ref, ckv_ref, kpe_ref, rq_ref, rk_ref, rv_ref, rg_ref, ga_ref, gb_ref):
    xf = x_ref[...]
    h = (xf * _rms_scale(xf, xf.shape[-1]) * gmix_ref[...]).astype(BF16)

    def proj(name):
        part, a, b = _IN_OFFS[name]
        w_ref = (wh_ref, wt_ref)[part]
        return jnp.dot(h, w_ref[:, a:b], preferred_element_type=F32)

    z = proj('qlat')
    qlat_ref[...] = (z * _rms_scale(z, z.shape[-1]) * gq_ref[...]).astype(BF16)
    z = proj('kvl')
    ckv_ref[...] = z * _rms_scale(z, z.shape[-1]) * gkv_ref[...]
    z = proj('kpe')
    zn = z * _rms_scale(z, QK_ROPE) * gkpe_ref[...]
    kpe = zn * c32_ref[...] + _swap_halves_64(zn) * s32_ref[...]
    kpe_ref[...] = kpe[:, :QK_ROPE]
    c64, s64 = c64_ref[...], s64_ref[...]
    for name, ref, scale in (('rq', rq_ref, 1.0), ('rk', rk_ref, RET_DK ** -0.5)):
        z = proj(name)
        for hd in range(RET_HEADS):
            zh = z[:, hd * RET_DK:(hd + 1) * RET_DK]
            r = zh * c64 + pltpu.roll(zh, RET_DK // 2, 1) * s64
            if scale != 1.0:
                r = r * scale
            ref[:, hd * RET_DK:(hd + 1) * RET_DK] = r.astype(BF16)
    rv_ref[...] = proj('rv').astype(BF16)
    rg_ref[...] = proj('rg').astype(BF16)
    ga_ref[...] = proj('ga').astype(BF16)
    gb_ref[...] = proj('gb').astype(BF16)


def _in_proj(x, w_in_parts, gmix, gq, gkv, gkpe, tabs64, tabs32, *, seq_tiles, tm):
    n, d = x.shape
    w_head, w_tail = w_in_parts
    row = lambda w: pl.BlockSpec((tm, w), lambda i: (i, 0))
    tab = pl.BlockSpec((tm, LANES), lambda i: (i % seq_tiles, 0))
    out_shapes = (
        jax.ShapeDtypeStruct((n, 768), BF16), jax.ShapeDtypeStruct((n, 512), F32),
        jax.ShapeDtypeStruct((n, QK_ROPE), F32), jax.ShapeDtypeStruct((n, 512), BF16),
        jax.ShapeDtypeStruct((n, 512), BF16), jax.ShapeDtypeStruct((n, 1024), BF16),
        jax.ShapeDtypeStruct((n, 1024), BF16), jax.ShapeDtypeStruct((n, 1024), BF16),
        jax.ShapeDtypeStruct((n, 1024), BF16))
    return pl.pallas_call(
        _inproj_kernel,
        grid=(n // tm,),
        in_specs=[row(d), _const_spec(gmix.shape), _const_spec(w_head.shape),
                  _const_spec(w_tail.shape), _const_spec(gq.shape), _const_spec(gkv.shape),
                  _const_spec(gkpe.shape), tab, tab, tab, tab],
        out_specs=[row(s.shape[1]) for s in out_shapes],
        out_shape=out_shapes,
        compiler_params=pltpu.CompilerParams(
            dimension_semantics=("arbitrary",), vmem_limit_bytes=_vmem_limit(56 << 20)),
        name="in_proj",
    )(x, gmix, w_head, w_tail, gq, gkv, gkpe, *tabs64, *tabs32)


_QUP_GROUPS = 2


def _qup_kernel(ql_ref, wt_ref, gn_ref, gr_ref, c_ref, s_ref, qt_ref):
    gn, gr, cos, sin = gn_ref[...], gr_ref[...], c_ref[...], s_ref[...]
    half = QK_ROPE // 2
    ql = ql_ref[...]

    group = MLA_HEADS // _QUP_GROUPS

    def project(g):
        return lax.dot_general(wt_ref[g * group * QK_DIM:(g + 1) * group * QK_DIM, :], ql,
                               (((1,), (1,)), ((), ())), preferred_element_type=F32)

    def finish(g, zt):
        for j in range(group):
            hd = g * group + j
            zn = zt[j * QK_DIM:j * QK_DIM + QK_NOPE, :]
            inv = lax.rsqrt(jnp.sum(zn * zn, axis=0, keepdims=True) * (1.0 / QK_NOPE) + EPS)
            qt_ref[hd, :QK_NOPE, :] = (zn * (inv * Q_PRESCALE) * gn).astype(BF16)
            zr = zt[j * QK_DIM + QK_NOPE:(j + 1) * QK_DIM, :]
            inv = lax.rsqrt(jnp.sum(zr * zr, axis=0, keepdims=True) * (1.0 / QK_ROPE) + EPS)
            z = zr * (inv * Q_PRESCALE) * gr
            swapped = jnp.concatenate([z[half:], z[:half]], axis=0)
            qt_ref[hd, QK_NOPE:, :] = (z * cos + swapped * sin).astype(BF16)

    zt = project(0)
    for g in range(_QUP_GROUPS):
        zt_next = project(g + 1) if g + 1 < _QUP_GROUPS else None
        finish(g, zt)
        zt = zt_next


def _q_up(qlat, w_qt, gn, gr, tabs32_t, *, seq_tiles, tm):
    n = qlat.shape[0]
    tab = pl.BlockSpec((QK_ROPE, tm), lambda i: (0, i % seq_tiles))
    return pl.pallas_call(
        _qup_kernel,
        grid=(n // tm,),
        in_specs=[pl.BlockSpec((tm, qlat.shape[1]), lambda i: (i, 0)), _const_spec(w_qt.shape),
                  _const_spec(gn.shape), _const_spec(gr.shape), tab, tab],
        out_specs=pl.BlockSpec((MLA_HEADS, QK_DIM, tm), lambda i: (0, 0, i)),
        out_shape=jax.ShapeDtypeStruct((MLA_HEADS, QK_DIM, n), BF16),
        compiler_params=pltpu.CompilerParams(
            dimension_semantics=("arbitrary",), vmem_limit_bytes=_vmem_limit(40 << 20)),
        name="q_up",
    )(qlat, w_qt, gn, gr, *tabs32_t)


_ONES_ROWS = 16


def _kvup_kernel(c_ref, kpe_ref, w_ref, gk_ref, k_ref, vt_ref):
    c = c_ref[...].astype(BF16)
    kpe = kpe_ref[...].astype(BF16)
    gk = gk_ref[...]
    kw = MLA_HEADS * QK_NOPE
    n_pairs = MLA_HEADS // 2

    def project(pair):
        zk = jnp.dot(c, w_ref[:, pair * 256:(pair + 1) * 256], preferred_element_type=F32)
        zv = jnp.dot(c, w_ref[:, kw + pair * 256:kw + (pair + 1) * 256],
                     preferred_element_type=F32)
        return zk, zv

    def finish(pair, zk, zv):
        for sub in range(2):
            hd = 2 * pair + sub
            zh = zk[:, sub * QK_NOPE:(sub + 1) * QK_NOPE]
            k_ref[hd, :, :QK_NOPE] = (zh * _rms_scale(zh, QK_NOPE) * gk).astype(BF16)
            k_ref[hd, :, QK_NOPE:] = kpe
            vh = zv[:, sub * V_HEAD:(sub + 1) * V_HEAD]
            vt_ref[hd, :V_HEAD, :] = vh.T.astype(BF16)
            vt_ref[hd, V_HEAD:, :] = jnp.ones((_ONES_ROWS, vh.shape[0]), BF16)

    z = project(0)
    for pair in range(n_pairs):
        z_next = project(pair + 1) if pair + 1 < n_pairs else None
        finish(pair, *z)
        z = z_next


def _kv_up(ckv, kpe, w_kv, gk, *, tm):
    n = ckv.shape[0]
    v_spec = pl.BlockSpec((MLA_HEADS, V_HEAD + _ONES_ROWS, tm), lambda i: (0, 0, i))
    v_shape = (MLA_HEADS, V_HEAD + _ONES_ROWS, n)
    return pl.pallas_call(
        _kvup_kernel,
        grid=(n // tm,),
        in_specs=[pl.BlockSpec((tm, ckv.shape[1]), lambda i: (i, 0)),
                  pl.BlockSpec((tm, QK_ROPE), lambda i: (i, 0)),
                  _const_spec(w_kv.shape), _const_spec(gk.shape)],
        out_specs=[pl.BlockSpec((MLA_HEADS, tm, QK_DIM), lambda i: (0, i, 0)), v_spec],
        out_shape=(jax.ShapeDtypeStruct((MLA_HEADS, n, QK_DIM), BF16),
                   jax.ShapeDtypeStruct(v_shape, BF16)),
        compiler_params=pltpu.CompilerParams(
            dimension_semantics=("arbitrary",), vmem_limit_bytes=_vmem_limit(40 << 20)),
        name="kv_up",
    )(ckv, kpe, w_kv, gk)


def _qk(q, k):
    return lax.dot_general(q, k, (((1,), (1,)), ((), ())), preferred_element_type=F32)


_STRIP = 256
_ATTN_KEY_BLOCK = 512


def _prompt_attn_kernel(qt_ref, k_ref, vt_ref, o_ref, s_ref, smax_ref, p_ref, a_ref, m_ref,
                        acc_ref, *, tq, tk):
    n_tiles = qt_ref.shape[2] // tq
    n_strips = tq // _STRIP
    n_diag = tq // tk
    assert n_diag % 2 == 0 and tk % _STRIP == 0 and _STRIP % CHUNK == 0
    all_strips = tuple(range(n_strips))
    cols = lambda c: slice(c * _STRIP, (c + 1) * _STRIP)
    n_seen = lambda d, c: max(0, min(tk, (c + 1) * _STRIP - d * tk))
    seen = lambda d: tuple(c for c in all_strips if n_seen(d, c) > 0)
    unmasked = lambda d, c: c * _STRIP >= (d + 1) * tk

    def scores(i, b, slot, c, nk=tk):
        start = pl.multiple_of(b * tk, tk)
        q_start = pl.multiple_of(i * tq + c * _STRIP, _STRIP)
        s = jnp.dot(k_ref[0, pl.ds(start, nk), :], qt_ref[0, :, pl.ds(q_start, _STRIP)],
                    preferred_element_type=F32)
        s_ref[slot, :nk, cols(c)] = s
        smax_ref[slot, :, cols(c)] = jnp.max(s, axis=0, keepdims=True)

    def softmax(slot, c, diag_block=None):
        nk = tk if diag_block is None else n_seen(diag_block, c)
        s = s_ref[slot, :nk, cols(c)]
        if diag_block is None or unmasked(diag_block, c):
            s_max = smax_ref[slot, :, cols(c)]
        else:
            kc = (lax.broadcasted_iota(jnp.int32, s.shape, 0) + diag_block * tk) // CHUNK
            qc = (lax.broadcasted_iota(jnp.int32, s.shape, 1) + c * _STRIP) // CHUNK
            s = jnp.where(kc <= qc, s, NEG_INF)
            s_max = jnp.max(s, axis=0, keepdims=True)
        m_prev = m_ref[:, cols(c)]
        m_new = jnp.maximum(m_prev, s_max)
        alpha = jnp.exp2(m_prev - m_new)
        p = jnp.exp2(s - m_new)
        m_ref[:, cols(c)] = m_new
        a_ref[slot, :, cols(c)] = alpha
        p_ref[slot, :nk, cols(c)] = p.astype(BF16)

    def values(b, slot, c, nk=tk):
        start = pl.multiple_of(jnp.maximum(b, 0) * tk, tk)
        pv = jnp.dot(vt_ref[0, :, pl.ds(start, nk)], p_ref[slot, :nk, cols(c)],
                     preferred_element_type=F32)
        acc_ref[:, cols(c)] = acc_ref[:, cols(c)] * a_ref[slot, :, cols(c)] + pv

    def first_scores(i):
        for c in all_strips:
            scores(i, 0, 0, c)

    def reset_state():
        m_ref[...] = jnp.full(m_ref.shape, NEG_INF, F32)
        acc_ref[...] = jnp.zeros(acc_ref.shape, F32)
        p_ref[1] = jnp.zeros(p_ref.shape[1:], BF16)
        a_ref[1] = jnp.ones(a_ref.shape[1:], F32)

    def tile(i, carry):
        def pair(jj, carry):
            b = 2 * jj
            for c in all_strips:
                scores(i, b + 1, 1, c)
            for c in all_strips:
                values(b - 1, 1, c)
            for c in all_strips:
                softmax(0, c)
            for c in all_strips:
                scores(i, b + 2, 0, c)
            for c in all_strips:
                values(b, 0, c)
            for c in all_strips:
                softmax(1, c)
            return carry

        lax.fori_loop(0, i * (n_diag // 2), pair, 0)
        b0 = n_diag * i
        for t in range(n_diag + 1):
            if 1 <= t + 1 < n_diag:
                for c in seen(t + 1):
                    scores(i, b0 + t + 1, (t + 1) % 2, c, n_seen(t + 1, c))
            if t == n_diag - 1:
                first_scores(jnp.minimum(i + 1, n_tiles - 1))
            if t == 0:
                for c in all_strips:
                    values(b0 - 1, 1, c)
            else:
                for c in seen(t - 1):
                    values(b0 + t - 1, (t - 1) % 2, c, n_seen(t - 1, c))
            if t < n_diag:
                for c in seen(t):
                    softmax(t % 2, c, diag_block=t)
        o = acc_ref[:V_HEAD, :] / acc_ref[V_HEAD:V_HEAD + 1, :]
        o_ref[pl.ds(pl.multiple_of(i * tq, tq), tq), :] = o.T.astype(o_ref.dtype)
        reset_state()
        return carry

    first_scores(0)
    reset_state()
    lax.fori_loop(0, n_tiles, tile, 0)


def _prompt_attention(qt, k, vt, *, batch, seq, tq):
    heads, _, n = qt.shape
    tk = _ATTN_KEY_BLOCK
    return pl.pallas_call(
        functools.partial(_prompt_attn_kernel, tq=tq, tk=tk),
        grid=(batch, heads),
        in_specs=[pl.BlockSpec((1, QK_DIM, seq), lambda b, h: (h, 0, b)),
                  pl.BlockSpec((1, seq, QK_DIM), lambda b, h: (h, b, 0)),
                  pl.BlockSpec((1, vt.shape[1], seq), lambda b, h: (h, 0, b))],
        out_specs=pl.BlockSpec((seq, V_HEAD), lambda b, h: (b, h)),
        out_shape=jax.ShapeDtypeStruct((n, heads * V_HEAD), BF16),
        scratch_shapes=[pltpu.VMEM((2, tk, tq), F32), pltpu.VMEM((2, 1, tq), F32),
                        pltpu.VMEM((2, tk, tq), BF16),
                        pltpu.VMEM((2, 1, tq), F32), pltpu.VMEM((1, tq), F32),
                        pltpu.VMEM((vt.shape[1], tq), F32)],
        compiler_params=pltpu.CompilerParams(
            dimension_semantics=("arbitrary", "arbitrary"),
            vmem_limit_bytes=_vmem_limit(48 << 20)),
        name="prompt_attention",
    )(qt, k, vt)


def _sample_attn_kernel(qbd_ref, qr_ref, latc_ref, kpec_ref, latn_ref, kpen_ref, w_ref, gk_ref,
                        o_ref, *, new):
    gk = gk_ref[...]
    kw = MLA_HEADS * QK_NOPE

    def scores(lat_ref, kpe_ref):
        c = lat_ref[0].astype(BF16)
        k_heads = []
        for pair in range(MLA_HEADS // 2):
            zk = jnp.dot(c, w_ref[:, pair * 256:(pair + 1) * 256], preferred_element_type=F32)
            for sub in range(2):
                zh = zk[:, sub * QK_NOPE:(sub + 1) * QK_NOPE]
                k_heads.append((zh * _rms_scale(zh, QK_NOPE) * gk).astype(BF16))
        k_all = jnp.concatenate(k_heads, axis=1)
        s = (jnp.dot(k_all, qbd_ref[0], preferred_element_type=F32)
             + jnp.dot(kpe_ref[0].astype(BF16), qr_ref[0], preferred_element_type=F32))
        return s, c

    s_c, c_c = scores(latc_ref, kpec_ref)
    s_n, c_n = scores(latn_ref, kpen_ref)
    m = jnp.maximum(jnp.max(s_c, axis=0, keepdims=True), jnp.max(s_n, axis=0, keepdims=True))
    p_c = jnp.exp2(s_c - m).astype(BF16)
    p_n = jnp.exp2(s_n - m).astype(BF16)
    tn = (((0,), (0,)), ((), ()))
    ctx = (lax.dot_general(p_c, c_c, tn, preferred_element_type=F32)
           + lax.dot_general(p_n, c_n, tn, preferred_element_type=F32))
    den = (lax.dot_general(p_c, jnp.ones((p_c.shape[0], V_HEAD), BF16), tn,
                           preferred_element_type=F32)
           + lax.dot_general(p_n, jnp.ones((p_n.shape[0], V_HEAD), BF16), tn,
                             preferred_element_type=F32))
    ctx = ctx.astype(BF16)
    for hd in range(MLA_HEADS):
        rows = slice(hd * new, (hd + 1) * new)
        o = jnp.dot(ctx[rows, :], w_ref[:, kw + hd * V_HEAD:kw + (hd + 1) * V_HEAD],
                    preferred_element_type=F32)
        o_ref[0, :, hd * V_HEAD:(hd + 1) * V_HEAD] = (o / den[rows, :]).astype(o_ref.dtype)


def _sample_attention(qt, lat_c, kpe_c, lat_n, kpe_n, w_kv, gk, *, batch, new):
    heads = qt.shape[0]
    q4 = qt.reshape(heads, QK_DIM, batch, new)
    qn = jnp.transpose(q4[:, :QK_NOPE], (2, 0, 1, 3))
    qbd = jnp.einsum('bhdj,hg->bhdgj', qn, jnp.eye(heads, dtype=qn.dtype))
    qbd = qbd.reshape(batch, heads * QK_NOPE, heads * new)
    qr = jnp.transpose(q4[:, QK_NOPE:], (2, 1, 0, 3)).reshape(batch, QK_ROPE, heads * new)
    per_batch = lambda a: pl.BlockSpec((1,) + a.shape[1:], lambda b: (b, 0, 0))
    return pl.pallas_call(
        functools.partial(_sample_attn_kernel, new=new),
        grid=(batch,),
        in_specs=[per_batch(qbd), per_batch(qr), per_batch(lat_c), per_batch(kpe_c),
                  per_batch(lat_n), per_batch(kpe_n), _const_spec(w_kv.shape),
                  _const_spec(gk.shape)],
        out_specs=pl.BlockSpec((1, new, heads * V_HEAD), lambda b: (b, 0, 0)),
        out_shape=jax.ShapeDtypeStruct((batch, new, heads * V_HEAD), BF16),
        compiler_params=pltpu.CompilerParams(
            dimension_semantics=("arbitrary",), vmem_limit_bytes=_vmem_limit(48 << 20)),
        name="sample_attention",
    )(qbd, qr, lat_c, kpe_c, lat_n, kpe_n, w_kv, gk)


_RET_SUB_CHUNKS = 4


def _retention_kernel(q_ref, k_ref, v_ref, rg_ref, dec_ref, qd_ref, kd_ref, g_ref, s0_ref,
                      o_ref, sfin_ref, state_ref, *, chunk, n_sub, n_steps):
    c = pl.program_id(1)

    @pl.when(c == 0)
    def _():
        state_ref[...] = s0_ref[0]

    heads = range(RET_HEADS)
    kcols = [slice(hd * RET_DK, (hd + 1) * RET_DK) for hd in heads]
    vcols = [slice(hd * RET_DV, (hd + 1) * RET_DV) for hd in heads]
    qd = [qd_ref[hd] for hd in heads]

    def recur(j, state):
        rows = slice(j * chunk, (j + 1) * chunk)
        q = [q_ref[rows, kcols[hd]] for hd in heads]
        k = [k_ref[rows, kcols[hd]] for hd in heads]
        v = [v_ref[rows, vcols[hd]] for hd in heads]
        s = [_qk(q[hd], k[hd]) for hd in heads]
        cross = [jnp.dot(q[hd], state[hd].astype(BF16), preferred_element_type=F32)
                 for hd in heads]
        s = [(s[hd] * dec_ref[hd]).astype(BF16) for hd in heads]
        k_dec = [(k[hd].astype(F32) * kd_ref[hd]).astype(BF16) for hd in heads]
        inner = [jnp.dot(s[hd], v[hd], preferred_element_type=F32) for hd in heads]
        kv = [lax.dot_general(k_dec[hd], v[hd], (((0,), (0,)), ((), ())),
                              preferred_element_type=F32) for hd in heads]
        new_state = [state[hd] * qd[hd][chunk - 1:chunk, :] + kv[hd] for hd in heads]
        return [inner[hd] + cross[hd] * qd[hd] for hd in heads], new_state

    def finish(j, o):
        rows = slice(j * chunk, (j + 1) * chunk)
        for hd in heads:
            mu = jnp.mean(o[hd], axis=-1, keepdims=True)
            oc = o[hd] - mu
            var = jnp.mean(oc * oc, axis=-1, keepdims=True)
            y = oc * lax.rsqrt(var + EPS) * g_ref[hd]
            rg = rg_ref[rows, vcols[hd]].astype(F32)
            o_ref[rows, vcols[hd]] = (rg * jax.nn.sigmoid(rg) * y).astype(o_ref.dtype)

    state = [state_ref[hd] for hd in heads]
    o_prev = None
    for j in range(n_sub):
        o, state = recur(j, state)
        if o_prev is not None:
            finish(j - 1, o_prev)
        o_prev = o
    finish(n_sub - 1, o_prev)
    for hd in heads:
        state_ref[hd] = state[hd]

    @pl.when(c == n_steps - 1)
    def _():
        sfin_ref[0] = state_ref[...]


def _retention_tables(chunk):
    lg = jnp.log1p(-(2.0 ** (-5.0 - jnp.arange(RET_HEADS, dtype=F32))))
    idx = jnp.arange(chunk, dtype=F32)
    diff = idx[:, None] - idx[None, :]
    decay = jnp.where(diff[None] >= 0,
                      jnp.exp(jnp.maximum(diff, 0.0)[None] * lg[:, None, None]), 0.0)
    q_dec = jnp.exp((idx + 1.0)[None, :] * lg[:, None])[..., None]
    k_dec = jnp.exp((chunk - 1.0 - idx)[None, :] * lg[:, None])[..., None]
    return decay, q_dec, k_dec


def _retention(rq, rk, rv, rg, g_ret, s0, *, batch, seq, chunk):
    n = rq.shape[0]
    n_sub = min(_RET_SUB_CHUNKS, seq // chunk)
    rows = n_sub * chunk
    ns = seq // rows
    decay, q_dec, k_dec = _retention_tables(chunk)
    tok = lambda w: pl.BlockSpec((rows, w), lambda b, c: (b * ns + c, 0))
    state_spec = pl.BlockSpec((1, RET_HEADS, RET_DK, RET_DV), lambda b, c: (b, 0, 0, 0))
    g_ret = g_ret.reshape(RET_HEADS, 1, RET_DV)
    return pl.pallas_call(
        functools.partial(_retention_kernel, chunk=chunk, n_sub=n_sub, n_steps=ns),
        grid=(batch, ns),
        in_specs=[tok(RET_HEADS * RET_DK), tok(RET_HEADS * RET_DK), tok(RET_HEADS * RET_DV),
                  tok(RET_HEADS * RET_DV), _const_spec(decay.shape), _const_spec(q_dec.shape),
                  _const_spec(k_dec.shape), _const_spec(g_ret.shape), state_spec],
        out_specs=[tok(RET_HEADS * RET_DV), state_spec],
        out_shape=(jax.ShapeDtypeStruct((n, RET_HEADS * RET_DV), BF16),
                   jax.ShapeDtypeStruct((batch, RET_HEADS, RET_DK, RET_DV), F32)),
        scratch_shapes=[pltpu.VMEM((RET_HEADS, RET_DK, RET_DV), F32)],
        compiler_params=pltpu.CompilerParams(dimension_semantics=("arbitrary", "arbitrary")),
        name="retention",
    )(rq, rk, rv, rg, decay, q_dec, k_dec, g_ret, s0)


def _merge_kernel(x_ref, a_ref, r_ref, ga_ref, gb_ref, wa_ref, wb_ref, wo_ref, o_ref):
    a_d = jnp.dot(a_ref[...], wa_ref[...], preferred_element_type=F32)
    r_d = jnp.dot(r_ref[...], wb_ref[...], preferred_element_type=F32)
    merged = (jax.nn.sigmoid(ga_ref[...].astype(F32)) * a_d
              + jax.nn.sigmoid(gb_ref[...].astype(F32)) * r_d)
    o_ref[...] = x_ref[...] + jnp.dot(merged.astype(BF16), wo_ref[...],
                                      preferred_element_type=F32)


def _merge(x, attn, ret, ga, gb, wa, wb, wo, *, tm):
    n, d = x.shape
    row = pl.BlockSpec((tm, d), lambda i: (i, 0))
    return pl.pallas_call(
        _merge_kernel,
        grid=(n // tm,),
        in_specs=[row, row, row, row, row,
                  _const_spec(wa.shape), _const_spec(wb.shape), _const_spec(wo.shape)],
        out_specs=row,
        out_shape=jax.ShapeDtypeStruct((n, d), F32),
        compiler_params=pltpu.CompilerParams(
            dimension_semantics=("arbitrary",), vmem_limit_bytes=_vmem_limit(40 << 20)),
        name="merge",
    )(x, attn, ret, ga, gb, wa, wb, wo)


_PAD_ROWS = 8


def _ffn_kernel(x_ref, g_ref, wup_ref, cw_ref, cb_ref, wdn_ref, prev_ref,
                y_ref, st_ref, ubuf_ref, *, rows, n_seqs, d_ff, n_tiles):
    j = pl.program_id(1)
    lo = _PAD_ROWS - (CONV_W - 1)
    stride = _PAD_ROWS + rows
    assert n_seqs == 1 or n_tiles == 1

    @pl.when(j == 0)
    def _():
        for b in range(n_seqs):
            ubuf_ref[b * stride + lo:b * stride + _PAD_ROWS, :] = prev_ref[b]

    if n_tiles > 1:
        @pl.when(j > 0)
        def _():
            ubuf_ref[0:_PAD_ROWS, :] = ubuf_ref[rows:rows + _PAD_ROWS, :]

    xf = x_ref[...]
    hn = (xf * _rms_scale(xf, xf.shape[-1]) * g_ref[...]).astype(BF16)
    u = jnp.dot(hn, wup_ref[...], preferred_element_type=F32)
    for b in range(n_seqs):
        ubuf_ref[b * stride + _PAD_ROWS:(b + 1) * stride, :] = u[b * rows:(b + 1) * rows, :]
    runs = []
    for b in range(n_seqs):
        c = cb_ref[...]
        for tap in range(CONV_W):
            first = b * stride + lo + tap
            c = c + ubuf_ref[first:first + rows, :] * cw_ref[tap:tap + 1, :]
        runs.append(c)
    c = runs[0] if n_seqs == 1 else jnp.concatenate(runs, axis=0)
    gate, val = c[:, :d_ff], c[:, d_ff:]
    act = (gate * jax.nn.sigmoid(gate) * val).astype(BF16)
    y_ref[...] = xf + jnp.dot(act, wdn_ref[...], preferred_element_type=F32)

    @pl.when(j == n_tiles - 1)
    def _():
        for b in range(n_seqs):
            st_ref[b] = ubuf_ref[(b + 1) * stride - (CONV_W - 1):(b + 1) * stride, :]


def _conv_ffn(x, g, w_up, cw, cb, w_dn, prev, *, batch, seq, tm):
    n, d = x.shape
    d_ff = w_dn.shape[0]
    n_seqs = max(tm // seq, 1)
    rows = tm // n_seqs
    nt = seq // rows
    row = pl.BlockSpec((tm, d), lambda b, j: (b * nt + j, 0))
    st = pl.BlockSpec((n_seqs, CONV_W - 1, 2 * d_ff), lambda b, j: (b, 0, 0))
    return pl.pallas_call(
        functools.partial(_ffn_kernel, rows=rows, n_seqs=n_seqs, d_ff=d_ff, n_tiles=nt),
        grid=(batch // n_seqs, nt),
        in_specs=[row, _const_spec(g.shape), _const_spec(w_up.shape), _const_spec(cw.shape),
                  _const_spec(cb.shape), _const_spec(w_dn.shape), st],
        out_specs=[row, st],
        out_shape=(jax.ShapeDtypeStruct((n, d), F32),
                   jax.ShapeDtypeStruct((batch, CONV_W - 1, 2 * d_ff), F32)),
        scratch_shapes=[pltpu.VMEM((n_seqs * (_PAD_ROWS + rows), 2 * d_ff), F32)],
        compiler_params=pltpu.CompilerParams(
            dimension_semantics=("arbitrary", "arbitrary"),
            vmem_limit_bytes=_vmem_limit(56 << 20)),
        name="conv_ffn",
    )(x, g, w_up, cw, cb, w_dn, prev)


def _pack_weights(w_in, w_q_up, w_kv_up, w_o_branch, w_out, w_ffn_up, w_ffn_down):
    head_width = sum(width for _, width in _IN_HEAD)
    w_head = jnp.pad(w_in[:, :_IN_HEAD_SRC_WIDTH].astype(BF16),
                     ((0, 0), (0, head_width - _IN_HEAD_SRC_WIDTH)))
    w_in_p = (w_head, w_in[:, _IN_HEAD_SRC_WIDTH:].astype(BF16))
    w_q_p = w_q_up.T.astype(BF16)
    wkv = w_kv_up.reshape(w_kv_up.shape[0], MLA_HEADS, QK_NOPE + V_HEAD)
    w_kv_p = jnp.concatenate([wkv[:, :, :QK_NOPE].reshape(wkv.shape[0], -1),
                              wkv[:, :, QK_NOPE:].reshape(wkv.shape[0], -1)], axis=1).astype(BF16)
    mla_width = MLA_HEADS * V_HEAD
    return (w_in_p, w_q_p, w_kv_p, w_o_branch[:mla_width].astype(BF16),
            w_o_branch[mla_width:].astype(BF16), w_out.astype(BF16),
            w_ffn_up.astype(BF16), w_ffn_down.astype(BF16))


def _layer(x, pos, past, w, *, batch, seq, tm, tq, ret_chunk, ffn_tm):
    n = x.shape[0]
    tabs64 = _rope_tables(pos, RET_DK // 2)
    tabs32 = _rope_tables(pos, QK_ROPE // 2)
    seq_tiles = max(seq // tm, 1)
    if seq < tm:
        tabs64 = tuple(np.tile(t, (tm // seq, 1)) for t in tabs64)
        tabs32 = tuple(np.tile(t, (tm // seq, 1)) for t in tabs32)
    qlat, ckv, kpe, rq, rk, rv, rg, ga, gb = _in_proj(
        x, w['w_in'], w['g_norm_mix'], w['g_q_lat'], w['g_kv_lat'], w['g_k_rope'],
        tabs64, tabs32, seq_tiles=seq_tiles, tm=tm)
    tabs32_t = tuple(np.ascontiguousarray(t[:, :QK_ROPE].T) for t in tabs32)
    bcast = lambda g: jnp.broadcast_to(g.reshape(-1, 1), (g.size, tm))
    qt = _q_up(qlat, w['w_q_up'], bcast(w['g_q_nope']), bcast(w['g_q_rope']), tabs32_t,
               seq_tiles=seq_tiles, tm=tm)
    if past is None:
        k_new, vt_new = _kv_up(ckv, kpe, w['w_kv_up'], w['g_k_nope'], tm=2 * tm)
        attn = _prompt_attention(qt, k_new, vt_new, batch=batch, seq=seq, tq=tq)
        s0 = jnp.zeros((batch, RET_HEADS, RET_DK, RET_DV), F32)
        prev = jnp.zeros((batch, CONV_W - 1, w['w_ffn_up'].shape[1]), F32)
    else:
        lat_c, pe_c, s0, prev = past
        attn = _sample_attention(qt, lat_c, pe_c, ckv.reshape(batch, seq, -1),
                                 kpe.reshape(batch, seq, -1), w['w_kv_up'], w['g_k_nope'],
                                 batch=batch, new=seq).reshape(n, -1)
    ret, s_fin = _retention(rq, rk, rv, rg, w['g_ret_out'], s0,
                            batch=batch, seq=seq, chunk=ret_chunk)
    x1 = _merge(x, attn, ret, ga, gb, w['w_o_a'], w['w_o_b'], w['w_out'], tm=tm)
    y, conv_state = _conv_ffn(x1, w['g_norm_ffn'], w['w_ffn_up'], w['ffn_conv_w'],
                              w['ffn_conv_b'], w['w_ffn_down'], prev,
                              batch=batch, seq=seq, tm=ffn_tm)
    return y, (ckv, kpe, s_fin, conv_state)


def kernel(x_prompt, x_sample, cache_mla_latent, cache_mla_rope_key, state_retention, state_ffn_conv, g_norm_mix, w_in, g_q_lat, w_q_up, g_q_nope, g_q_rope, g_kv_lat, w_kv_up, g_k_nope, g_k_rope, g_ret_out, w_o_branch, w_out, g_norm_ffn, w_ffn_up, ffn_conv_w, ffn_conv_b, w_ffn_down):
    depth = w_in.shape[0]
    assert depth == 1, "single-layer trunk"
    bp, tp, d = x_prompt.shape
    bs, ts, _ = x_sample.shape
    past_len = cache_mla_latent.shape[2]
    (w_in_p, w_q_p, w_kv_p, w_o_a, w_o_b, w_out_p, w_up_p, w_dn_p) = _pack_weights(
        w_in[0], w_q_up[0], w_kv_up[0], w_o_branch[0], w_out[0], w_ffn_up[0], w_ffn_down[0])
    row = lambda g: g.reshape(1, -1).astype(F32)
    w = dict(
        w_in=w_in_p, w_q_up=w_q_p, w_kv_up=w_kv_p, w_o_a=w_o_a, w_o_b=w_o_b, w_out=w_out_p,
        w_ffn_up=w_up_p, w_ffn_down=w_dn_p,
        g_norm_mix=row(g_norm_mix[0]), g_q_lat=row(g_q_lat[0]), g_kv_lat=row(g_kv_lat[0]),
        g_k_rope=jnp.pad(row(g_k_rope[0]), ((0, 0), (0, LANES - QK_ROPE))),
        g_q_nope=g_q_nope[0].astype(F32), g_q_rope=g_q_rope[0].astype(F32),
        g_k_nope=row(g_k_nope[0]), g_ret_out=g_ret_out[0], g_norm_ffn=row(g_norm_ffn[0]),
        ffn_conv_w=ffn_conv_w[0], ffn_conv_b=row(ffn_conv_b[0]))

    yp, sp = _layer(x_prompt.reshape(bp * tp, d), np.arange(tp), None, w,
                    batch=bp, seq=tp, tm=512, tq=2048, ret_chunk=256, ffn_tm=512)
    past = (cache_mla_latent[0], cache_mla_rope_key[0], state_retention[0], state_ffn_conv[0])
    ys, ss = _layer(x_sample.reshape(bs * ts, d), past_len + np.arange(ts), past, w,
                    batch=bs, seq=ts, tm=bs * ts, tq=None, ret_chunk=ts, ffn_tm=bs * ts)

    def states(st, b, t):
        ckv, kpe, s_fin, conv = st
        return (ckv.reshape(1, b, t, -1), kpe.reshape(1, b, t, -1), s_fin[None], conv[None])

    return (yp.reshape(bp, tp, d), ys.reshape(bs, ts, d)) + states(sp, bp, tp) + states(ss, bs, ts)
```

```python
import functools
import math

import jax
import jax.numpy as jnp
import numpy as np
from jax import lax
from jax.experimental import pallas as pl
from jax.experimental.pallas import tpu as pltpu

F32 = jnp.float32
BF16 = jnp.bfloat16

EPS = 1e-6
NEG_INF = -1e30
ROPE_THETA = 10000.0
CHUNK = 64
LANES = 128
V7X_VMEM_BYTES = 64 * 1024 * 1024

MLA_HEADS = 8
QK_NOPE = 128
QK_ROPE = 64
QK_DIM = QK_NOPE + QK_ROPE
V_HEAD = 128
MLA_SCALE = QK_DIM ** -0.5
Q_PRESCALE = MLA_SCALE * math.log2(math.e)
RET_HEADS = 4
RET_DK = 128
RET_DV = 256
CONV_W = 3


def _vmem_limit(nbytes):
    return int(min(V7X_VMEM_BYTES - (4 << 20), max(32 << 20, nbytes)))


def _const_spec(shape):
    nd = len(shape)
    return pl.BlockSpec(shape, lambda *_: (0,) * nd, pipeline_mode=pl.Buffered(1))


def _rms_scale(z, width):
    return lax.rsqrt(jnp.sum(z * z, axis=-1, keepdims=True) * (1.0 / width) + EPS)


def _swap_halves_64(z):
    lane = lax.broadcasted_iota(jnp.int32, z.shape, 1)
    return jnp.where((lane % 64) < 32, pltpu.roll(z, 96, 1), pltpu.roll(z, 32, 1))


def _rope_tables(pos, half):
    freqs = ROPE_THETA ** (-np.arange(half, dtype=np.float64) / half)
    ang = pos.astype(np.float64)[:, None] * freqs[None, :]
    cos, sin = np.cos(ang), np.sin(ang)
    reps = LANES // (2 * half)
    return (np.tile(np.concatenate([cos, cos], -1), (1, reps)).astype(np.float32),
            np.tile(np.concatenate([-sin, sin], -1), (1, reps)).astype(np.float32))


_IN_HEAD = (('qlat', 768), ('kvl', 512), ('kpe', LANES))
_IN_TAIL = (('rq', 512), ('rk', 512), ('rv', 1024), ('rg', 1024), ('ga', 1024), ('gb', 1024))
_IN_HEAD_SRC_WIDTH = 768 + 512 + QK_ROPE


def _group_offsets(groups):
    offs, o = {}, 0
    for name, width in groups:
        offs[name] = (o, o + width)
        o += width
    return offs


_IN_OFFS = {**{n: (0,) + ab for n, ab in _group_offsets(_IN_HEAD).items()},
            **{n: (1,) + ab for n, ab in _group_offsets(_IN_TAIL).items()}}


def _inproj_kernel(x_ref, gmix_ref, wh_ref, wt_ref, gq_ref, gkv_ref, gkpe_ref,
                   c64_ref, s64_ref, c32_ref, s32_ref,
                   qlat_ref, ckv_ref, kpe_ref, rq_ref, rk_ref, rv_ref, rg_ref, ga_ref, gb_ref):
    xf = x_ref[...]
    h = (xf * _rms_scale(xf, xf.shape[-1]) * gmix_ref[...]).astype(BF16)

    def proj(name):
        part, a, b = _IN_OFFS[name]
        w_ref = (wh_ref, wt_ref)[part]
        return jnp.dot(h, w_ref[:, a:b], preferred_element_type=F32)

    z = proj('qlat')
    qlat_ref[...] = (z * _rms_scale(z, z.shape[-1]) * gq_ref[...]).astype(BF16)
    z = proj('kvl')
    ckv_ref[...] = z * _rms_scale(z, z.shape[-1]) * gkv_ref[...]
    z = proj('kpe')
    zn = z * _rms_scale(z, QK_ROPE) * gkpe_ref[...]
    kpe = zn * c32_ref[...] + _swap_halves_64(zn) * s32_ref[...]
    kpe_ref[...] = kpe[:, :QK_ROPE]
    c64, s64 = c64_ref[...], s64_ref[...]
    for name, ref, scale in (('rq', rq_ref, 1.0), ('rk', rk_ref, RET_DK ** -0.5)):
        z = proj(name)
        for hd in range(RET_HEADS):
            zh = z[:, hd * RET_DK:(hd + 1) * RET_DK]
            r = zh * c64 + pltpu.roll(zh, RET_DK // 2, 1) * s64
            if scale != 1.0:
                r = r * scale
            ref[:, hd * RET_DK:(hd + 1) * RET_DK] = r.astype(BF16)
    rv_ref[...] = proj('rv').astype(BF16)
    rg_ref[...] = proj('rg').astype(BF16)
    ga_ref[...] = proj('ga').astype(BF16)
    gb_ref[...] = proj('gb').astype(BF16)


def _in_proj(x, w_in_parts, gmix, gq, gkv, gkpe, tabs64, tabs32, *, seq_tiles, tm):
    n, d = x.shape
    w_head, w_tail = w_in_parts
    row = lambda w: pl.BlockSpec((tm, w), lambda i: (i, 0))
    tab = pl.BlockSpec((tm, LANES), lambda i: (i % seq_tiles, 0))
    out_shapes = (
        jax.ShapeDtypeStruct((n, 768), BF16), jax.ShapeDtypeStruct((n, 512), F32),
        jax.ShapeDtypeStruct((n, QK_ROPE), F32), jax.ShapeDtypeStruct((n, 512), BF16),
        jax.ShapeDtypeStruct((n, 512), BF16), jax.ShapeDtypeStruct((n, 1024), BF16),
        jax.ShapeDtypeStruct((n, 1024), BF16), jax.ShapeDtypeStruct((n, 1024), BF16),
        jax.ShapeDtypeStruct((n, 1024), BF16))
    return pl.pallas_call(
        _inproj_kernel,
        grid=(n // tm,),
        in_specs=[row(d), _const_spec(gmix.shape), _const_spec(w_head.shape),
                  _const_spec(w_tail.shape), _const_spec(gq.shape), _const_spec(gkv.shape),
                  _const_spec(gkpe.shape), tab, tab, tab, tab],
        out_specs=[row(s.shape[1]) for s in out_shapes],
        out_shape=out_shapes,
        compiler_params=pltpu.CompilerParams(
            dimension_semantics=("arbitrary",), vmem_limit_bytes=_vmem_limit(56 << 20)),
        name="in_proj",
    )(x, gmix, w_head, w_tail, gq, gkv, gkpe, *tabs64, *tabs32)


_QUP_GROUPS = 2


def _qup_kernel(ql_ref, wt_ref, gn_ref, gr_ref, c_ref, s_ref, qt_ref):
    gn, gr, cos, sin = gn_ref[...], gr_ref[...], c_ref[...], s_ref[...]
    half = QK_ROPE // 2
    ql = ql_ref[...]

    group = MLA_HEADS // _QUP_GROUPS

    def project(g):
        return lax.dot_general(wt_ref[g * group * QK_DIM:(g + 1) * group * QK_DIM, :], ql,
                               (((1,), (1,)), ((), ())), preferred_element_type=F32)

    def finish(g, zt):
        for j in range(group):
            hd = g * group + j
            zn = zt[j * QK_DIM:j * QK_DIM + QK_NOPE, :]
            inv = lax.rsqrt(jnp.sum(zn * zn, axis=0, keepdims=True) * (1.0 / QK_NOPE) + EPS)
            qt_ref[hd, :QK_NOPE, :] = (zn * (inv * Q_PRESCALE) * gn).astype(BF16)
            zr = zt[j * QK_DIM + QK_NOPE:(j + 1) * QK_DIM, :]
            inv = lax.rsqrt(jnp.sum(zr * zr, axis=0, keepdims=True) * (1.0 / QK_ROPE) + EPS)
            z = zr * (inv * Q_PRESCALE) * gr
            swapped = jnp.concatenate([z[half:], z[:half]], axis=0)
            qt_ref[hd, QK_NOPE:, :] = (z * cos + swapped * sin).astype(BF16)

    zt = project(0)
    for g in range(_QUP_GROUPS):
        zt_next = project(g + 1) if g + 1 < _QUP_GROUPS else None
        finish(g, zt)
        zt = zt_next


def _q_up(qlat, w_qt, gn, gr, tabs32_t, *, seq_tiles, tm):
    n = qlat.shape[0]
    tab = pl.BlockSpec((QK_ROPE, tm), lambda i: (0, i % seq_tiles))
    return pl.pallas_call(
        _qup_kernel,
        grid=(n // tm,),
        in_specs=[pl.BlockSpec((tm, qlat.shape[1]), lambda i: (i, 0)), _const_spec(w_qt.shape),
                  _const_spec(gn.shape), _const_spec(gr.shape), tab, tab],
        out_specs=pl.BlockSpec((MLA_HEADS, QK_DIM, tm), lambda i: (0, 0, i)),
        out_shape=jax.ShapeDtypeStruct((MLA_HEADS, QK_DIM, n), BF16),
        compiler_params=pltpu.CompilerParams(
            dimension_semantics=("arbitrary",), vmem_limit_bytes=_vmem_limit(40 << 20)),
        name="q_up",
    )(qlat, w_qt, gn, gr, *tabs32_t)


_ONES_ROWS = 16


def _kvup_kernel(c_ref, kpe_ref, w_ref, gk_ref, k_ref, vt_ref):
    c = c_ref[...].astype(BF16)
    kpe = kpe_ref[...].astype(BF16)
    gk = gk_ref[...]
    kw = MLA_HEADS * QK_NOPE
    n_pairs = MLA_HEADS // 2

    def project(pair):
        zk = jnp.dot(c, w_ref[:, pair * 256:(pair + 1) * 256], preferred_element_type=F32)
        zv = jnp.dot(c, w_ref[:, kw + pair * 256:kw + (pair + 1) * 256],
                     preferred_element_type=F32)
        return zk, zv

    def finish(pair, zk, zv):
        for sub in range(2):
            hd = 2 * pair + sub
            zh = zk[:, sub * QK_NOPE:(sub + 1) * QK_NOPE]
            k_ref[hd, :, :QK_NOPE] = (zh * _rms_scale(zh, QK_NOPE) * gk).astype(BF16)
            k_ref[hd, :, QK_NOPE:] = kpe
            vh = zv[:, sub * V_HEAD:(sub + 1) * V_HEAD]
            vt_ref[hd, :V_HEAD, :] = vh.T.astype(BF16)
            vt_ref[hd, V_HEAD:, :] = jnp.ones((_ONES_ROWS, vh.shape[0]), BF16)

    z = project(0)
    for pair in range(n_pairs):
        z_next = project(pair + 1) if pair + 1 < n_pairs else None
        finish(pair, *z)
        z = z_next


def _kv_up(ckv, kpe, w_kv, gk, *, tm):
    n = ckv.shape[0]
    v_spec = pl.BlockSpec((MLA_HEADS, V_HEAD + _ONES_ROWS, tm), lambda i: (0, 0, i))
    v_shape = (MLA_HEADS, V_HEAD + _ONES_ROWS, n)
    return pl.pallas_call(
        _kvup_kernel,
        grid=(n // tm,),
        in_specs=[pl.BlockSpec((tm, ckv.shape[1]), lambda i: (i, 0)),
                  pl.BlockSpec((tm, QK_ROPE), lambda i: (i, 0)),
                  _const_spec(w_kv.shape), _const_spec(gk.shape)],
        out_specs=[pl.BlockSpec((MLA_HEADS, tm, QK_DIM), lambda i: (0, i, 0)), v_spec],
        out_shape=(jax.ShapeDtypeStruct((MLA_HEADS, n, QK_DIM), BF16),
                   jax.ShapeDtypeStruct(v_shape, BF16)),
        compiler_params=pltpu.CompilerParams(
            dimension_semantics=("arbitrary",), vmem_limit_bytes=_vmem_limit(40 << 20)),
        name="kv_up",
    )(ckv, kpe, w_kv, gk)


def _qk(q, k):
    return lax.dot_general(q, k, (((1,), (1,)), ((), ())), preferred_element_type=F32)


_STRIP = 256
_ATTN_KEY_BLOCK = 512


def _prompt_attn_kernel(qt_ref, k_ref, vt_ref, o_ref, s_ref, smax_ref, p_ref, a_ref, m_ref,
                        acc_ref, *, tq, tk):
    n_tiles = qt_ref.shape[2] // tq
    n_strips = tq // _STRIP
    n_diag = tq // tk
    assert n_diag % 2 == 0 and tk % _STRIP == 0 and _STRIP % CHUNK == 0
    all_strips = tuple(range(n_strips))
    cols = lambda c: slice(c * _STRIP, (c + 1) * _STRIP)
    n_seen = lambda d, c: max(0, min(tk, (c + 1) * _STRIP - d * tk))
    seen = lambda d: tuple(c for c in all_strips if n_seen(d, c) > 0)
    unmasked = lambda d, c: c * _STRIP >= (d + 1) * tk

    def scores(i, b, slot, c, nk=tk):
        start = pl.multiple_of(b * tk, tk)
        q_start = pl.multiple_of(i * tq + c * _STRIP, _STRIP)
        s = jnp.dot(k_ref[0, pl.ds(start, nk), :], qt_ref[0, :, pl.ds(q_start, _STRIP)],
                    preferred_element_type=F32)
        s_ref[slot, :nk, cols(c)] = s
        smax_ref[slot, :, cols(c)] = jnp.max(s, axis=0, keepdims=True)

    def softmax(slot, c, diag_block=None):
        nk = tk if diag_block is None else n_seen(diag_block, c)
        s = s_ref[slot, :nk, cols(c)]
        if diag_block is None or unmasked(diag_block, c):
            s_max = smax_ref[slot, :, cols(c)]
        else:
            kc = (lax.broadcasted_iota(jnp.int32, s.shape, 0) + diag_block * tk) // CHUNK
            qc = (lax.broadcasted_iota(jnp.int32, s.shape, 1) + c * _STRIP) // CHUNK
            s = jnp.where(kc <= qc, s, NEG_INF)
            s_max = jnp.max(s, axis=0, keepdims=True)
        m_prev = m_ref[:, cols(c)]
        m_new = jnp.maximum(m_prev, s_max)
        alpha = jnp.exp2(m_prev - m_new)
        p = jnp.exp2(s - m_new)
        m_ref[:, cols(c)] = m_new
        a_ref[slot, :, cols(c)] = alpha
        p_ref[slot, :nk, cols(c)] = p.astype(BF16)

    def values(b, slot, c, nk=tk):
        start = pl.multiple_of(jnp.maximum(b, 0) * tk, tk)
        pv = jnp.dot(vt_ref[0, :, pl.ds(start, nk)], p_ref[slot, :nk, cols(c)],
                     preferred_element_type=F32)
        acc_ref[:, cols(c)] = acc_ref[:, cols(c)] * a_ref[slot, :, cols(c)] + pv

    def first_scores(i):
        for c in all_strips:
            scores(i, 0, 0, c)

    def reset_state():
        m_ref[...] = jnp.full(m_ref.shape, NEG_INF, F32)
        acc_ref[...] = jnp.zeros(acc_ref.shape, F32)
        p_ref[1] = jnp.zeros(p_ref.shape[1:], BF16)
        a_ref[1] = jnp.ones(a_ref.shape[1:], F32)

    def tile(i, carry):
        def pair(jj, carry):
            b = 2 * jj
            for c in all_strips:
                scores(i, b + 1, 1, c)
            for c in all_strips:
                values(b - 1, 1, c)
            for c in all_strips:
                softmax(0, c)
            for c in all_strips:
                scores(i, b + 2, 0, c)
            for c in all_strips:
                values(b, 0, c)
            for c in all_strips:
                softmax(1, c)
            return carry

        lax.fori_loop(0, i * (n_diag // 2), pair, 0)
        b0 = n_diag * i
        for t in range(n_diag + 1):
            if 1 <= t + 1 < n_diag:
                for c in seen(t + 1):
                    scores(i, b0 + t + 1, (t + 1) % 2, c, n_seen(t + 1, c))
            if t == n_diag - 1:
                first_scores(jnp.minimum(i + 1, n_tiles - 1))
            if t == 0:
                for c in all_strips:
                    values(b0 - 1, 1, c)
            else:
                for c in seen(t - 1):
                    values(b0 + t - 1, (t - 1) % 2, c, n_seen(t - 1, c))
            if t < n_diag:
                for c in seen(t):
                    softmax(t % 2, c, diag_block=t)
        o = acc_ref[:V_HEAD, :] / acc_ref[V_HEAD:V_HEAD + 1, :]
        o_ref[pl.ds(pl.multiple_of(i * tq, tq), tq), :] = o.T.astype(o_ref.dtype)
        reset_state()
        return carry

    first_scores(0)
    reset_state()
    lax.fori_loop(0, n_tiles, tile, 0)


def _prompt_attention(qt, k, vt, *, batch, seq, tq):
    heads, _, n = qt.shape
    tk = _ATTN_KEY_BLOCK
    return pl.pallas_call(
        functools.partial(_prompt_attn_kernel, tq=tq, tk=tk),
        grid=(batch, heads),
        in_specs=[pl.BlockSpec((1, QK_DIM, seq), lambda b, h: (h, 0, b)),
                  pl.BlockSpec((1, seq, QK_DIM), lambda b, h: (h, b, 0)),
                  pl.BlockSpec((1, vt.shape[1], seq), lambda b, h: (h, 0, b))],
        out_specs=pl.BlockSpec((seq, V_HEAD), lambda b, h: (b, h)),
        out_shape=jax.ShapeDtypeStruct((n, heads * V_HEAD), BF16),
        scratch_shapes=[pltpu.VMEM((2, tk, tq + LANES), F32), pltpu.VMEM((2, 1, tq), F32),
                        pltpu.VMEM((2, tk, tq), BF16),
                        pltpu.VMEM((2, 1, tq), F32), pltpu.VMEM((1, tq), F32),
                        pltpu.VMEM((vt.shape[1], tq), F32)],
        compiler_params=pltpu.CompilerParams(
            dimension_semantics=("arbitrary", "arbitrary"),
            vmem_limit_bytes=_vmem_limit(48 << 20)),
        name="prompt_attention",
    )(qt, k, vt)


def _sample_attn_kernel(qbd_ref, qr_ref, latc_ref, kpec_ref, latn_ref, kpen_ref, w_ref, gk_ref,
                        o_ref, *, new):
    gk = gk_ref[...]
    kw = MLA_HEADS * QK_NOPE

    def scores(lat_ref, kpe_ref):
        c = lat_ref[0].astype(BF16)
        k_heads = []
        for pair in range(MLA_HEADS // 2):
            zk = jnp.dot(c, w_ref[:, pair * 256:(pair + 1) * 256], preferred_element_type=F32)
            for sub in range(2):
                zh = zk[:, sub * QK_NOPE:(sub + 1) * QK_NOPE]
                k_heads.append((zh * _rms_scale(zh, QK_NOPE) * gk).astype(BF16))
        k_all = jnp.concatenate(k_heads, axis=1)
        s = (jnp.dot(k_all, qbd_ref[0], preferred_element_type=F32)
             + jnp.dot(kpe_ref[0].astype(BF16), qr_ref[0], preferred_element_type=F32))
        return s, c

    s_c, c_c = scores(latc_ref, kpec_ref)
    s_n, c_n = scores(latn_ref, kpen_ref)
    m = jnp.maximum(jnp.max(s_c, axis=0, keepdims=True), jnp.max(s_n, axis=0, keepdims=True))
    p_c = jnp.exp2(s_c - m).astype(BF16)
    p_n = jnp.exp2(s_n - m).astype(BF16)
    tn = (((0,), (0,)), ((), ()))
    ctx = (lax.dot_general(p_c, c_c, tn, preferred_element_type=F32)
           + lax.dot_general(p_n, c_n, tn, preferred_element_type=F32))
    den = (lax.dot_general(p_c, jnp.ones((p_c.shape[0], V_HEAD), BF16), tn,
                           preferred_element_type=F32)
           + lax.dot_general(p_n, jnp.ones((p_n.shape[0], V_HEAD), BF16), tn,
                             preferred_element_type=F32))
    ctx = ctx.astype(BF16)
    for hd in range(MLA_HEADS):
        rows = slice(hd * new, (hd + 1) * new)
        o = jnp.dot(ctx[rows, :], w_ref[:, kw + hd * V_HEAD:kw + (hd + 1) * V_HEAD],
                    preferred_element_type=F32)
        o_ref[0, :, hd * V_HEAD:(hd + 1) * V_HEAD] = (o / den[rows, :]).astype(o_ref.dtype)


def _sample_attention(qt, lat_c, kpe_c, lat_n, kpe_n, w_kv, gk, *, batch, new):
    heads = qt.shape[0]
    q4 = qt.reshape(heads, QK_DIM, batch, new)
    qn = jnp.transpose(q4[:, :QK_NOPE], (2, 0, 1, 3))
    qbd = jnp.einsum('bhdj,hg->bhdgj', qn, jnp.eye(heads, dtype=qn.dtype))
    qbd = qbd.reshape(batch, heads * QK_NOPE, heads * new)
    qr = jnp.transpose(q4[:, QK_NOPE:], (2, 1, 0, 3)).reshape(batch, QK_ROPE, heads * new)
    per_batch = lambda a: pl.BlockSpec((1,) + a.shape[1:], lambda b: (b, 0, 0))
    return pl.pallas_call(
        functools.partial(_sample_attn_kernel, new=new),
        grid=(batch,),
        in_specs=[per_batch(qbd), per_batch(qr), per_batch(lat_c), per_batch(kpe_c),
                  per_batch(lat_n), per_batch(kpe_n), _const_spec(w_kv.shape),
                  _const_spec(gk.shape)],
        out_specs=pl.BlockSpec((1, new, heads * V_HEAD), lambda b: (b, 0, 0)),
        out_shape=jax.ShapeDtypeStruct((batch, new, heads * V_HEAD), BF16),
        compiler_params=pltpu.CompilerParams(
            dimension_semantics=("arbitrary",), vmem_limit_bytes=_vmem_limit(48 << 20)),
        name="sample_attention",
    )(qbd, qr, lat_c, kpe_c, lat_n, kpe_n, w_kv, gk)


_RET_SUB_CHUNKS = 4


def _retention_kernel(q_ref, k_ref, v_ref, rg_ref, dec_ref, qd_ref, kd_ref, g_ref, s0_ref,
                      o_ref, sfin_ref, state_ref, *, chunk, n_sub, n_steps):
    c = pl.program_id(1)

    @pl.when(c == 0)
    def _():
        state_ref[...] = s0_ref[0]

    heads = range(RET_HEADS)
    kcols = [slice(hd * RET_DK, (hd + 1) * RET_DK) for hd in heads]
    vcols = [slice(hd * RET_DV, (hd + 1) * RET_DV) for hd in heads]
    qd = [qd_ref[hd] for hd in heads]

    def recur(j, state):
        rows = slice(j * chunk, (j + 1) * chunk)
        q = [q_ref[rows, kcols[hd]] for hd in heads]
        k = [k_ref[rows, kcols[hd]] for hd in heads]
        v = [v_ref[rows, vcols[hd]] for hd in heads]
        s = [_qk(q[hd], k[hd]) for hd in heads]
        cross = [jnp.dot(q[hd], state[hd].astype(BF16), preferred_element_type=F32)
                 for hd in heads]
        s = [(s[hd] * dec_ref[hd]).astype(BF16) for hd in heads]
        k_dec = [(k[hd].astype(F32) * kd_ref[hd]).astype(BF16) for hd in heads]
        inner = [jnp.dot(s[hd], v[hd], preferred_element_type=F32) for hd in heads]
        kv = [lax.dot_general(k_dec[hd], v[hd], (((0,), (0,)), ((), ())),
                              preferred_element_type=F32) for hd in heads]
        new_state = [state[hd] * qd[hd][chunk - 1:chunk, :] + kv[hd] for hd in heads]
        return [inner[hd] + cross[hd] * qd[hd] for hd in heads], new_state

    def finish(j, o):
        rows = slice(j * chunk, (j + 1) * chunk)
        for hd in heads:
            mu = jnp.mean(o[hd], axis=-1, keepdims=True)
            oc = o[hd] - mu
            var = jnp.mean(oc * oc, axis=-1, keepdims=True)
            y = oc * lax.rsqrt(var + EPS) * g_ref[hd]
            rg = rg_ref[rows, vcols[hd]].astype(F32)
            o_ref[rows, vcols[hd]] = (rg * jax.nn.sigmoid(rg) * y).astype(o_ref.dtype)

    state = [state_ref[hd] for hd in heads]
    o_prev = None
    for j in range(n_sub):
        o, state = recur(j, state)
        if o_prev is not None:
            finish(j - 1, o_prev)
        o_prev = o
    finish(n_sub - 1, o_prev)
    for hd in heads:
        state_ref[hd] = state[hd]

    @pl.when(c == n_steps - 1)
    def _():
        sfin_ref[0] = state_ref[...]


def _retention_tables(chunk):
    lg = jnp.log1p(-(2.0 ** (-5.0 - jnp.arange(RET_HEADS, dtype=F32))))
    idx = jnp.arange(chunk, dtype=F32)
    diff = idx[:, None] - idx[None, :]
    decay = jnp.where(diff[None] >= 0,
                      jnp.exp(jnp.maximum(diff, 0.0)[None] * lg[:, None, None]), 0.0)
    q_dec = jnp.exp((idx + 1.0)[None, :] * lg[:, None])[..., None]
    k_dec = jnp.exp((chunk - 1.0 - idx)[None, :] * lg[:, None])[..., None]
    return decay, q_dec, k_dec


def _retention(rq, rk, rv, rg, g_ret, s0, *, batch, seq, chunk):
    n = rq.shape[0]
    n_sub = min(_RET_SUB_CHUNKS, seq // chunk)
    rows = n_sub * chunk
    ns = seq // rows
    decay, q_dec, k_dec = _retention_tables(chunk)
    tok = lambda w: pl.BlockSpec((rows, w), lambda b, c: (b * ns + c, 0))
    state_spec = pl.BlockSpec((1, RET_HEADS, RET_DK, RET_DV), lambda b, c: (b, 0, 0, 0))
    g_ret = g_ret.reshape(RET_HEADS, 1, RET_DV)
    return pl.pallas_call(
        functools.partial(_retention_kernel, chunk=chunk, n_sub=n_sub, n_steps=ns),
        grid=(batch, ns),
        in_specs=[tok(RET_HEADS * RET_DK), tok(RET_HEADS * RET_DK), tok(RET_HEADS * RET_DV),
                  tok(RET_HEADS * RET_DV), _const_spec(decay.shape), _const_spec(q_dec.shape),
                  _const_spec(k_dec.shape), _const_spec(g_ret.shape), state_spec],
        out_specs=[tok(RET_HEADS * RET_DV), state_spec],
        out_shape=(jax.ShapeDtypeStruct((n, RET_HEADS * RET_DV), BF16),
                   jax.ShapeDtypeStruct((batch, RET_HEADS, RET_DK, RET_DV), F32)),
        scratch_shapes=[pltpu.VMEM((RET_HEADS, RET_DK, RET_DV), F32)],
        compiler_params=pltpu.CompilerParams(dimension_semantics=("arbitrary", "arbitrary")),
        name="retention",
    )(rq, rk, rv, rg, decay, q_dec, k_dec, g_ret, s0)


def _merge_kernel(x_ref, a_ref, r_ref, ga_ref, gb_ref, wa_ref, wb_ref, wo_ref, o_ref):
    a_d = jnp.dot(a_ref[...], wa_ref[...], preferred_element_type=F32)
    r_d = jnp.dot(r_ref[...], wb_ref[...], preferred_element_type=F32)
    merged = (jax.nn.sigmoid(ga_ref[...].astype(F32)) * a_d
              + jax.nn.sigmoid(gb_ref[...].astype(F32)) * r_d)
    o_ref[...] = x_ref[...] + jnp.dot(merged.astype(BF16), wo_ref[...],
                                      preferred_element_type=F32)


def _merge(x, attn, ret, ga, gb, wa, wb, wo, *, tm):
    n, d = x.shape
    row = pl.BlockSpec((tm, d), lambda i: (i, 0))
    return pl.pallas_call(
        _merge_kernel,
        grid=(n // tm,),
        in_specs=[row, row, row, row, row,
                  _const_spec(wa.shape), _const_spec(wb.shape), _const_spec(wo.shape)],
        out_specs=row,
        out_shape=jax.ShapeDtypeStruct((n, d), F32),
        compiler_params=pltpu.CompilerParams(
            dimension_semantics=("arbitrary",), vmem_limit_bytes=_vmem_limit(40 << 20)),
        name="merge",
    )(x, attn, ret, ga, gb, wa, wb, wo)


_PAD_ROWS = 8


def _ffn_kernel(x_ref, g_ref, wup_ref, cw_ref, cb_ref, wdn_ref, prev_ref,
                y_ref, st_ref, ubuf_ref, *, rows, n_seqs, d_ff, n_tiles):
    j = pl.program_id(1)
    lo = _PAD_ROWS - (CONV_W - 1)
    stride = _PAD_ROWS + rows
    assert n_seqs == 1 or n_tiles == 1

    @pl.when(j == 0)
    def _():
        for b in range(n_seqs):
            ubuf_ref[b * stride + lo:b * stride + _PAD_ROWS, :] = prev_ref[b]

    if n_tiles > 1:
        @pl.when(j > 0)
        def _():
            ubuf_ref[0:_PAD_ROWS, :] = ubuf_ref[rows:rows + _PAD_ROWS, :]

    xf = x_ref[...]
    hn = (xf * _rms_scale(xf, xf.shape[-1]) * g_ref[...]).astype(BF16)
    u = jnp.dot(hn, wup_ref[...], preferred_element_type=F32)
    for b in range(n_seqs):
        ubuf_ref[b * stride + _PAD_ROWS:(b + 1) * stride, :] = u[b * rows:(b + 1) * rows, :]
    runs = []
    for b in range(n_seqs):
        c = cb_ref[...]
        for tap in range(CONV_W):
            first = b * stride + lo + tap
            c = c + ubuf_ref[first:first + rows, :] * cw_ref[tap:tap + 1, :]
        runs.append(c)
    c = runs[0] if n_seqs == 1 else jnp.concatenate(runs, axis=0)
    gate, val = c[:, :d_ff], c[:, d_ff:]
    act = (gate * jax.nn.sigmoid(gate) * val).astype(BF16)
    y_ref[...] = xf + jnp.dot(act, wdn_ref[...], preferred_element_type=F32)

    @pl.when(j == n_tiles - 1)
    def _():
        for b in range(n_seqs):
            st_ref[b] = ubuf_ref[(b + 1) * stride - (CONV_W - 1):(b + 1) * stride, :]


def _conv_ffn(x, g, w_up, cw, cb, w_dn, prev, *, batch, seq, tm):
    n, d = x.shape
    d_ff = w_dn.shape[0]
    n_seqs = max(tm // seq, 1)
    rows = tm // n_seqs
    nt = seq // rows
    row = pl.BlockSpec((tm, d), lambda b, j: (b * nt + j, 0))
    st = pl.BlockSpec((n_seqs, CONV_W - 1, 2 * d_ff), lambda b, j: (b, 0, 0))
    return pl.pallas_call(
        functools.partial(_ffn_kernel, rows=rows, n_seqs=n_seqs, d_ff=d_ff, n_tiles=nt),
        grid=(batch // n_seqs, nt),
        in_specs=[row, _const_spec(g.shape), _const_spec(w_up.shape), _const_spec(cw.shape),
                  _const_spec(cb.shape), _const_spec(w_dn.shape), st],
        out_specs=[row, st],
        out_shape=(jax.ShapeDtypeStruct((n, d), F32),
                   jax.ShapeDtypeStruct((batch, CONV_W - 1, 2 * d_ff), F32)),
        scratch_shapes=[pltpu.VMEM((n_seqs * (_PAD_ROWS + rows), 2 * d_ff), F32)],
        compiler_params=pltpu.CompilerParams(
            dimension_semantics=("arbitrary", "arbitrary"),
            vmem_limit_bytes=_vmem_limit(56 << 20)),
        name="conv_ffn",
    )(x, g, w_up, cw, cb, w_dn, prev)


def _pack_weights(w_in, w_q_up, w_kv_up, w_o_branch, w_out, w_ffn_up, w_ffn_down):
    head_width = sum(width for _, width in _IN_HEAD)
    w_head = jnp.pad(w_in[:, :_IN_HEAD_SRC_WIDTH].astype(BF16),
                     ((0, 0), (0, head_width - _IN_HEAD_SRC_WIDTH)))
    w_in_p = (w_head, w_in[:, _IN_HEAD_SRC_WIDTH:].astype(BF16))
    w_q_p = w_q_up.T.astype(BF16)
    wkv = w_kv_up.reshape(w_kv_up.shape[0], MLA_HEADS, QK_NOPE + V_HEAD)
    w_kv_p = jnp.concatenate([wkv[:, :, :QK_NOPE].reshape(wkv.shape[0], -1),
                              wkv[:, :, QK_NOPE:].reshape(wkv.shape[0], -1)], axis=1).astype(BF16)
    mla_width = MLA_HEADS * V_HEAD
    return (w_in_p, w_q_p, w_kv_p, w_o_branch[:mla_width].astype(BF16),
            w_o_branch[mla_width:].astype(BF16), w_out.astype(BF16),
            w_ffn_up.astype(BF16), w_ffn_down.astype(BF16))


def _layer(x, pos, past, w, *, batch, seq, tm, tq, ret_chunk, ffn_tm):
    n = x.shape[0]
    tabs64 = _rope_tables(pos, RET_DK // 2)
    tabs32 = _rope_tables(pos, QK_ROPE // 2)
    seq_tiles = max(seq // tm, 1)
    if seq < tm:
        tabs64 = tuple(np.tile(t, (tm // seq, 1)) for t in tabs64)
        tabs32 = tuple(np.tile(t, (tm // seq, 1)) for t in tabs32)
    qlat, ckv, kpe, rq, rk, rv, rg, ga, gb = _in_proj(
        x, w['w_in'], w['g_norm_mix'], w['g_q_lat'], w['g_kv_lat'], w['g_k_rope'],
        tabs64, tabs32, seq_tiles=seq_tiles, tm=tm)
    tabs32_t = tuple(np.ascontiguousarray(t[:, :QK_ROPE].T) for t in tabs32)
    bcast = lambda g: jnp.broadcast_to(g.reshape(-1, 1), (g.size, tm))
    qt = _q_up(qlat, w['w_q_up'], bcast(w['g_q_nope']), bcast(w['g_q_rope']), tabs32_t,
               seq_tiles=seq_tiles, tm=tm)
    if past is None:
        k_new, vt_new = _kv_up(ckv, kpe, w['w_kv_up'], w['g_k_nope'], tm=2 * tm)
        attn = _prompt_attention(qt, k_new, vt_new, batch=batch, seq=seq, tq=tq)
        s0 = jnp.zeros((batch, RET_HEADS, RET_DK, RET_DV), F32)
        prev = jnp.zeros((batch, CONV_W - 1, w['w_ffn_up'].shape[1]), F32)
    else:
        lat_c, pe_c, s0, prev = past
        attn = _sample_attention(qt, lat_c, pe_c, ckv.reshape(batch, seq, -1),
                                 kpe.reshape(batch, seq, -1), w['w_kv_up'], w['g_k_nope'],
                                 batch=batch, new=seq).reshape(n, -1)
    ret, s_fin = _retention(rq, rk, rv, rg, w['g_ret_out'], s0,
                            batch=batch, seq=seq, chunk=ret_chunk)
    x1 = _merge(x, attn, ret, ga, gb, w['w_o_a'], w['w_o_b'], w['w_out'], tm=tm)
    y, conv_state = _conv_ffn(x1, w['g_norm_ffn'], w['w_ffn_up'], w['ffn_conv_w'],
                              w['ffn_conv_b'], w['w_ffn_down'], prev,
                              batch=batch, seq=seq, tm=ffn_tm)
    return y, (ckv, kpe, s_fin, conv_state)


def kernel(x_prompt, x_sample, cache_mla_latent, cache_mla_rope_key, state_retention, state_ffn_conv, g_norm_mix, w_in, g_q_lat, w_q_up, g_q_nope, g_q_rope, g_kv_lat, w_kv_up, g_k_nope, g_k_rope, g_ret_out, w_o_branch, w_out, g_norm_ffn, w_ffn_up, ffn_conv_w, ffn_conv_b, w_ffn_down):
    depth = w_in.shape[0]
    assert depth == 1, "single-layer trunk"
    bp, tp, d = x_prompt.shape
    bs, ts, _ = x_sample.shape
    past_len = cache_mla_latent.shape[2]
    (w_in_p, w_q_p, w_kv_p, w_o_a, w_o_b, w_out_p, w_up_p, w_dn_p) = _pack_weights(
        w_in[0], w_q_up[0], w_kv_up[0], w_o_branch[0], w_out[0], w_ffn_up[0], w_ffn_down[0])
    row = lambda g: g.reshape(1, -1).astype(F32)
    w = dict(
        w_in=w_in_p, w_q_up=w_q_p, w_kv_up=w_kv_p, w_o_a=w_o_a, w_o_b=w_o_b, w_out=w_out_p,
        w_ffn_up=w_up_p, w_ffn_down=w_dn_p,
        g_norm_mix=row(g_norm_mix[0]), g_q_lat=row(g_q_lat[0]), g_kv_lat=row(g_kv_lat[0]),
        g_k_rope=jnp.pad(row(g_k_rope[0]), ((0, 0), (0, LANES - QK_ROPE))),
        g_q_nope=g_q_nope[0].astype(F32), g_q_rope=g_q_rope[0].astype(F32),
        g_k_nope=row(g_k_nope[0]), g_ret_out=g_ret_out[0], g_norm_ffn=row(g_norm_ffn[0]),
        ffn_conv_w=ffn_conv_w[0], ffn_conv_b=row(ffn_conv_b[0]))

    yp, sp = _layer(x_prompt.reshape(bp * tp, d), np.arange(tp), None, w,
                    batch=bp, seq=tp, tm=512, tq=2048, ret_chunk=256, ffn_tm=512)
    past = (cache_mla_latent[0], cache_mla_rope_key[0], state_retention[0], state_ffn_conv[0])
    ys, ss = _layer(x_sample.reshape(bs * ts, d), past_len + np.arange(ts), past, w,
                    batch=bs, seq=ts, tm=bs * ts, tq=None, ret_chunk=ts, ffn_tm=bs * ts)

    def states(st, b, t):
        ckv, kpe, s_fin, conv = st
        return (ckv.reshape(1, b, t, -1), kpe.reshape(1, b, t, -1), s_fin[None], conv[None])

    return (yp.reshape(bp, tp, d), ys.reshape(bs, ts, d)) + states(sp, bp, tp) + states(ss, bs, ts)
```

```python
import functools
import math
from typing import NamedTuple

import jax
import jax.numpy as jnp
import numpy as np
from jax import lax
from jax.experimental import pallas as pl
from jax.experimental.pallas import tpu as pltpu

F32 = jnp.float32
BF16 = jnp.bfloat16

EPS = 1e-6
NEG_INF = -1e30
ROPE_THETA = 10000.0
CHUNK = 64
LANES = 128
MXU_TILE = 256
V7X_VMEM_BYTES = 64 * 1024 * 1024

D_MODEL = 1024
Q_LORA = 768
KV_LORA = 512
MLA_HEADS = 8
QK_NOPE = 128
QK_ROPE = 64
QK_DIM = QK_NOPE + QK_ROPE
V_HEAD = 128
MLA_SCALE = QK_DIM ** -0.5
Q_PRESCALE = MLA_SCALE * math.log2(math.e)
RET_HEADS = 4
RET_DK = 128
RET_DV = 256
CONV_W = 3


def _vmem_limit(nbytes):
    return int(min(V7X_VMEM_BYTES - (4 << 20), max(32 << 20, nbytes)))


def _const_spec(shape):
    nd = len(shape)
    return pl.BlockSpec(shape, lambda *_: (0,) * nd, pipeline_mode=pl.Buffered(1))


def _rms_scale(z, width):
    return lax.rsqrt(jnp.sum(z * z, axis=-1, keepdims=True) * (1.0 / width) + EPS)


def _swap_halves_64(z):
    lane = lax.broadcasted_iota(jnp.int32, z.shape, 1)
    return jnp.where((lane % 64) < 32, pltpu.roll(z, 96, 1), pltpu.roll(z, 32, 1))


def _rope_tables(pos, half):
    freqs = ROPE_THETA ** (-np.arange(half, dtype=np.float64) / half)
    ang = pos.astype(np.float64)[:, None] * freqs[None, :]
    cos, sin = np.cos(ang), np.sin(ang)
    reps = LANES // (2 * half)
    return (np.tile(np.concatenate([cos, cos], -1), (1, reps)).astype(np.float32),
            np.tile(np.concatenate([-sin, sin], -1), (1, reps)).astype(np.float32))


_IN_HEAD = (('qlat', Q_LORA), ('kvl', KV_LORA), ('kpe', LANES))
_IN_TAIL = (('rq', RET_HEADS * RET_DK), ('rk', RET_HEADS * RET_DK), ('rv', RET_HEADS * RET_DV),
            ('rg', RET_HEADS * RET_DV), ('ga', D_MODEL), ('gb', D_MODEL))
_IN_HEAD_SRC_WIDTH = Q_LORA + KV_LORA + QK_ROPE


def _group_offsets(groups):
    offs, o = {}, 0
    for name, width in groups:
        offs[name] = (o, o + width)
        o += width
    return offs


_IN_OFFS = {**{n: (0,) + ab for n, ab in _group_offsets(_IN_HEAD).items()},
            **{n: (1,) + ab for n, ab in _group_offsets(_IN_TAIL).items()}}


def _inproj_kernel(x_ref, gmix_ref, wh_ref, wt_ref, gq_ref, gkv_ref, gkpe_ref,
                   c64_ref, s64_ref, c32_ref, s32_ref,
                   qlat_ref, ckv_ref, kpe_ref, rq_ref, rk_ref, rv_ref, rg_ref, ga_ref, gb_ref):
    xf = x_ref[...]
    h = (xf * _rms_scale(xf, xf.shape[-1]) * gmix_ref[...]).astype(BF16)

    def proj(name):
        part, a, b = _IN_OFFS[name]
        w_ref = (wh_ref, wt_ref)[part]
        return jnp.dot(h, w_ref[:, a:b], preferred_element_type=F32)

    z = proj('qlat')
    qlat_ref[...] = (z * _rms_scale(z, z.shape[-1]) * gq_ref[...]).astype(BF16)
    z = proj('kvl')
    ckv_ref[...] = z * _rms_scale(z, z.shape[-1]) * gkv_ref[...]
    z = proj('kpe')
    zn = z * _rms_scale(z, QK_ROPE) * gkpe_ref[...]
    kpe = zn * c32_ref[...] + _swap_halves_64(zn) * s32_ref[...]
    kpe_ref[...] = kpe[:, :QK_ROPE]
    c64, s64 = c64_ref[...], s64_ref[...]
    for name, ref, scale in (('rq', rq_ref, 1.0), ('rk', rk_ref, RET_DK ** -0.5)):
        z = proj(name)
        for hd in range(RET_HEADS):
            zh = z[:, hd * RET_DK:(hd + 1) * RET_DK]
            r = zh * c64 + pltpu.roll(zh, RET_DK // 2, 1) * s64
            if scale != 1.0:
                r = r * scale
            ref[:, hd * RET_DK:(hd + 1) * RET_DK] = r.astype(BF16)
    rv_ref[...] = proj('rv').astype(BF16)
    rg_ref[...] = proj('rg').astype(BF16)
    ga_ref[...] = proj('ga').astype(BF16)
    gb_ref[...] = proj('gb').astype(BF16)


def _in_proj(x, w_in_parts, gmix, gq, gkv, gkpe, tabs64, tabs32, *, seq_tiles, tm):
    n, d = x.shape
    w_head, w_tail = w_in_parts
    row = lambda w: pl.BlockSpec((tm, w), lambda i: (i, 0))
    tab = pl.BlockSpec((tm, LANES), lambda i: (i % seq_tiles, 0))
    width = dict(_IN_HEAD + _IN_TAIL)
    out_dtypes = dict(qlat=BF16, kvl=F32, kpe=F32, rq=BF16, rk=BF16, rv=BF16, rg=BF16,
                      ga=BF16, gb=BF16)
    out_shapes = tuple(
        jax.ShapeDtypeStruct((n, QK_ROPE if name == 'kpe' else width[name]), dtype)
        for name, dtype in out_dtypes.items())
    return pl.pallas_call(
        _inproj_kernel,
        grid=(n // tm,),
        in_specs=[row(d), _const_spec(gmix.shape), _const_spec(w_head.shape),
                  _const_spec(w_tail.shape), _const_spec(gq.shape), _const_spec(gkv.shape),
                  _const_spec(gkpe.shape), tab, tab, tab, tab],
        out_specs=[row(s.shape[1]) for s in out_shapes],
        out_shape=out_shapes,
        compiler_params=pltpu.CompilerParams(
            dimension_semantics=("arbitrary",), vmem_limit_bytes=_vmem_limit(56 << 20)),
        name="in_proj",
    )(x, gmix, w_head, w_tail, gq, gkv, gkpe, *tabs64, *tabs32)


_QUP_GROUPS = 2


def _qup_kernel(ql_ref, wt_ref, gn_ref, gr_ref, c_ref, s_ref, qt_ref):
    gn, gr, cos, sin = gn_ref[...], gr_ref[...], c_ref[...], s_ref[...]
    half = QK_ROPE // 2
    ql = ql_ref[...]

    group = MLA_HEADS // _QUP_GROUPS

    def project(g):
        return lax.dot_general(wt_ref[g * group * QK_DIM:(g + 1) * group * QK_DIM, :], ql,
                               (((1,), (1,)), ((), ())), preferred_element_type=F32)

    def finish(g, zt):
        for j in range(group):
            hd = g * group + j
            zn = zt[j * QK_DIM:j * QK_DIM + QK_NOPE, :]
            inv = lax.rsqrt(jnp.sum(zn * zn, axis=0, keepdims=True) * (1.0 / QK_NOPE) + EPS)
            qt_ref[hd, :QK_NOPE, :] = (zn * (inv * Q_PRESCALE) * gn).astype(BF16)
            zr = zt[j * QK_DIM + QK_NOPE:(j + 1) * QK_DIM, :]
            inv = lax.rsqrt(jnp.sum(zr * zr, axis=0, keepdims=True) * (1.0 / QK_ROPE) + EPS)
            z = zr * (inv * Q_PRESCALE) * gr
            swapped = jnp.concatenate([z[half:], z[:half]], axis=0)
            qt_ref[hd, QK_NOPE:, :] = (z * cos + swapped * sin).astype(BF16)

    zt = project(0)
    for g in range(_QUP_GROUPS):
        zt_next = project(g + 1) if g + 1 < _QUP_GROUPS else None
        finish(g, zt)
        zt = zt_next


def _q_up(qlat, w_qt, gn, gr, tabs32_t, *, seq_tiles, tm):
    n = qlat.shape[0]
    tab = pl.BlockSpec((QK_ROPE, tm), lambda i: (0, i % seq_tiles))
    return pl.pallas_call(
        _qup_kernel,
        grid=(n // tm,),
        in_specs=[pl.BlockSpec((tm, qlat.shape[1]), lambda i: (i, 0)), _const_spec(w_qt.shape),
                  _const_spec(gn.shape), _const_spec(gr.shape), tab, tab],
        out_specs=pl.BlockSpec((MLA_HEADS, QK_DIM, tm), lambda i: (0, 0, i)),
        out_shape=jax.ShapeDtypeStruct((MLA_HEADS, QK_DIM, n), BF16),
        compiler_params=pltpu.CompilerParams(
            dimension_semantics=("arbitrary",), vmem_limit_bytes=_vmem_limit(40 << 20)),
        name="q_up",
    )(qlat, w_qt, gn, gr, *tabs32_t)


_ONES_ROWS = 16


def _kvup_kernel(c_ref, kpe_ref, w_ref, gk_ref, k_ref, vt_ref):
    c = c_ref[...].astype(BF16)
    kpe = kpe_ref[...].astype(BF16)
    gk = gk_ref[...]
    kw = MLA_HEADS * QK_NOPE
    n_pairs = MLA_HEADS // 2

    def project(pair):
        cols = slice(pair * MXU_TILE, (pair + 1) * MXU_TILE)
        zk = jnp.dot(c, w_ref[:, cols], preferred_element_type=F32)
        zv = jnp.dot(c, w_ref[:, kw + cols.start:kw + cols.stop], preferred_element_type=F32)
        return zk, zv

    def finish(pair, zk, zv):
        for sub in range(2):
            hd = 2 * pair + sub
            zh = zk[:, sub * QK_NOPE:(sub + 1) * QK_NOPE]
            k_ref[hd, :, :QK_NOPE] = (zh * _rms_scale(zh, QK_NOPE) * gk).astype(BF16)
            k_ref[hd, :, QK_NOPE:] = kpe
            vh = zv[:, sub * V_HEAD:(sub + 1) * V_HEAD]
            vt_ref[hd, :V_HEAD, :] = vh.T.astype(BF16)
            vt_ref[hd, V_HEAD:, :] = jnp.ones((_ONES_ROWS, vh.shape[0]), BF16)

    z = project(0)
    for pair in range(n_pairs):
        z_next = project(pair + 1) if pair + 1 < n_pairs else None
        finish(pair, *z)
        z = z_next


def _kv_up(ckv, kpe, w_kv, gk, *, tm):
    n = ckv.shape[0]
    v_spec = pl.BlockSpec((MLA_HEADS, V_HEAD + _ONES_ROWS, tm), lambda i: (0, 0, i))
    v_shape = (MLA_HEADS, V_HEAD + _ONES_ROWS, n)
    return pl.pallas_call(
        _kvup_kernel,
        grid=(n // tm,),
        in_specs=[pl.BlockSpec((tm, ckv.shape[1]), lambda i: (i, 0)),
                  pl.BlockSpec((tm, QK_ROPE), lambda i: (i, 0)),
                  _const_spec(w_kv.shape), _const_spec(gk.shape)],
        out_specs=[pl.BlockSpec((MLA_HEADS, tm, QK_DIM), lambda i: (0, i, 0)), v_spec],
        out_shape=(jax.ShapeDtypeStruct((MLA_HEADS, n, QK_DIM), BF16),
                   jax.ShapeDtypeStruct(v_shape, BF16)),
        compiler_params=pltpu.CompilerParams(
            dimension_semantics=("arbitrary",), vmem_limit_bytes=_vmem_limit(40 << 20)),
        name="kv_up",
    )(ckv, kpe, w_kv, gk)


def _qk(q, k):
    return lax.dot_general(q, k, (((1,), (1,)), ((), ())), preferred_element_type=F32)


_STRIP = MXU_TILE
_ATTN_KEY_BLOCK = 512


def _prompt_attn_kernel(qt_ref, k_ref, vt_ref, o_ref, s_ref, smax_ref, p_ref, a_ref, m_ref,
                        acc_ref, *, tq, tk):
    n_tiles = qt_ref.shape[2] // tq
    n_strips = tq // _STRIP
    n_diag = tq // tk
    assert n_diag % 2 == 0 and tk % _STRIP == 0 and _STRIP % CHUNK == 0
    all_strips = tuple(range(n_strips))
    cols = lambda c: slice(c * _STRIP, (c + 1) * _STRIP)
    n_seen = lambda d, c: max(0, min(tk, (c + 1) * _STRIP - d * tk))
    seen = lambda d: tuple(c for c in all_strips if n_seen(d, c) > 0)
    unmasked = lambda d, c: c * _STRIP >= (d + 1) * tk

    def scores(i, b, slot, c, nk=tk):
        start = pl.multiple_of(b * tk, tk)
        q_start = pl.multiple_of(i * tq + c * _STRIP, _STRIP)
        s = jnp.dot(k_ref[0, pl.ds(start, nk), :], qt_ref[0, :, pl.ds(q_start, _STRIP)],
                    preferred_element_type=F32)
        s_ref[slot, :nk, cols(c)] = s
        smax_ref[slot, :, cols(c)] = jnp.max(s, axis=0, keepdims=True)

    def softmax(slot, c, diag_block=None):
        nk = tk if diag_block is None else n_seen(diag_block, c)
        s = s_ref[slot, :nk, cols(c)]
        if diag_block is None or unmasked(diag_block, c):
            s_max = smax_ref[slot, :, cols(c)]
        else:
            kc = (lax.broadcasted_iota(jnp.int32, s.shape, 0) + diag_block * tk) // CHUNK
            qc = (lax.broadcasted_iota(jnp.int32, s.shape, 1) + c * _STRIP) // CHUNK
            s = jnp.where(kc <= qc, s, NEG_INF)
            s_max = jnp.max(s, axis=0, keepdims=True)
        m_prev = m_ref[:, cols(c)]
        m_new = jnp.maximum(m_prev, s_max)
        alpha = jnp.exp2(m_prev - m_new)
        p = jnp.exp2(s - m_new)
        m_ref[:, cols(c)] = m_new
        a_ref[slot, :, cols(c)] = alpha
        p_ref[slot, :nk, cols(c)] = p.astype(BF16)

    def values(b, slot, c, nk=tk):
        start = pl.multiple_of(jnp.maximum(b, 0) * tk, tk)
        pv = jnp.dot(vt_ref[0, :, pl.ds(start, nk)], p_ref[slot, :nk, cols(c)],
                     preferred_element_type=F32)
        acc_ref[:, cols(c)] = acc_ref[:, cols(c)] * a_ref[slot, :, cols(c)] + pv

    def first_scores(i):
        for c in all_strips:
            scores(i, 0, 0, c)

    def reset_state():
        m_ref[...] = jnp.full(m_ref.shape, NEG_INF, F32)
        acc_ref[...] = jnp.zeros(acc_ref.shape, F32)
        p_ref[1] = jnp.zeros(p_ref.shape[1:], BF16)
        a_ref[1] = jnp.ones(a_ref.shape[1:], F32)

    def tile(i, carry):
        def pair(jj, carry):
            b = 2 * jj
            for c in all_strips:
                scores(i, b + 1, 1, c)
            for c in all_strips:
                values(b - 1, 1, c)
            for c in all_strips:
                softmax(0, c)
            for c in all_strips:
                scores(i, b + 2, 0, c)
            for c in all_strips:
                values(b, 0, c)
            for c in all_strips:
                softmax(1, c)
            return carry

        lax.fori_loop(0, i * (n_diag // 2), pair, 0)
        b0 = n_diag * i
        for t in range(n_diag + 1):
            if 1 <= t + 1 < n_diag:
                for c in seen(t + 1):
                    scores(i, b0 + t + 1, (t + 1) % 2, c, n_seen(t + 1, c))
            if t == n_diag - 1:
                first_scores(jnp.minimum(i + 1, n_tiles - 1))
            if t == 0:
                for c in all_strips:
                    values(b0 - 1, 1, c)
            else:
                for c in seen(t - 1):
                    values(b0 + t - 1, (t - 1) % 2, c, n_seen(t - 1, c))
            if t < n_diag:
                for c in seen(t):
                    softmax(t % 2, c, diag_block=t)
        o = acc_ref[:V_HEAD, :] / acc_ref[V_HEAD:V_HEAD + 1, :]
        o_ref[pl.ds(pl.multiple_of(i * tq, tq), tq), :] = o.T.astype(o_ref.dtype)
        reset_state()
        return carry

    first_scores(0)
    reset_state()
    lax.fori_loop(0, n_tiles, tile, 0)


def _prompt_attention(qt, k, vt, *, batch, seq, tq):
    heads, _, n = qt.shape
    tk = _ATTN_KEY_BLOCK
    return pl.pallas_call(
        functools.partial(_prompt_attn_kernel, tq=tq, tk=tk),
        grid=(batch, heads),
        in_specs=[pl.BlockSpec((1, QK_DIM, seq), lambda b, h: (h, 0, b)),
                  pl.BlockSpec((1, seq, QK_DIM), lambda b, h: (h, b, 0)),
                  pl.BlockSpec((1, vt.shape[1], seq), lambda b, h: (h, 0, b))],
        out_specs=pl.BlockSpec((seq, V_HEAD), lambda b, h: (b, h)),
        out_shape=jax.ShapeDtypeStruct((n, heads * V_HEAD), BF16),
        scratch_shapes=[pltpu.VMEM((2, tk, tq), F32), pltpu.VMEM((2, 1, tq), F32),
                        pltpu.VMEM((2, tk, tq), BF16),
                        pltpu.VMEM((2, 1, tq), F32), pltpu.VMEM((1, tq), F32),
                        pltpu.VMEM((vt.shape[1], tq), F32)],
        compiler_params=pltpu.CompilerParams(
            dimension_semantics=("arbitrary", "arbitrary"),
            vmem_limit_bytes=_vmem_limit(48 << 20)),
        name="prompt_attention",
    )(qt, k, vt)


def _sample_attn_kernel(qbd_ref, qr_ref, latc_ref, kpec_ref, latn_ref, kpen_ref, w_ref, gk_ref,
                        o_ref, *, new):
    gk = gk_ref[...]
    kw = MLA_HEADS * QK_NOPE

    def scores(lat_ref, kpe_ref):
        c = lat_ref[0].astype(BF16)
        k_heads = []
        for pair in range(MLA_HEADS // 2):
            zk = jnp.dot(c, w_ref[:, pair * MXU_TILE:(pair + 1) * MXU_TILE],
                         preferred_element_type=F32)
            for sub in range(2):
                zh = zk[:, sub * QK_NOPE:(sub + 1) * QK_NOPE]
                k_heads.append((zh * _rms_scale(zh, QK_NOPE) * gk).astype(BF16))
        k_all = jnp.concatenate(k_heads, axis=1)
        s = (jnp.dot(k_all, qbd_ref[0], preferred_element_type=F32)
             + jnp.dot(kpe_ref[0].astype(BF16), qr_ref[0], preferred_element_type=F32))
        return s, c

    s_c, c_c = scores(latc_ref, kpec_ref)
    s_n, c_n = scores(latn_ref, kpen_ref)
    m = jnp.maximum(jnp.max(s_c, axis=0, keepdims=True), jnp.max(s_n, axis=0, keepdims=True))
    p_c = jnp.exp2(s_c - m).astype(BF16)
    p_n = jnp.exp2(s_n - m).astype(BF16)
    tn = (((0,), (0,)), ((), ()))
    ctx = (lax.dot_general(p_c, c_c, tn, preferred_element_type=F32)
           + lax.dot_general(p_n, c_n, tn, preferred_element_type=F32))
    den = (lax.dot_general(p_c, jnp.ones((p_c.shape[0], V_HEAD), BF16), tn,
                           preferred_element_type=F32)
           + lax.dot_general(p_n, jnp.ones((p_n.shape[0], V_HEAD), BF16), tn,
                             preferred_element_type=F32))
    ctx = ctx.astype(BF16)
    for hd in range(MLA_HEADS):
        rows = slice(hd * new, (hd + 1) * new)
        o = jnp.dot(ctx[rows, :], w_ref[:, kw + hd * V_HEAD:kw + (hd + 1) * V_HEAD],
                    preferred_element_type=F32)
        o_ref[0, :, hd * V_HEAD:(hd + 1) * V_HEAD] = (o / den[rows, :]).astype(o_ref.dtype)


def _sample_attention(qt, lat_c, kpe_c, lat_n, kpe_n, w_kv, gk, *, batch, new):
    heads = qt.shape[0]
    q4 = qt.reshape(heads, QK_DIM, batch, new)
    qn = jnp.transpose(q4[:, :QK_NOPE], (2, 0, 1, 3))
    qbd = jnp.einsum('bhdj,hg->bhdgj', qn, jnp.eye(heads, dtype=qn.dtype))
    qbd = qbd.reshape(batch, heads * QK_NOPE, heads * new)
    qr = jnp.transpose(q4[:, QK_NOPE:], (2, 1, 0, 3)).reshape(batch, QK_ROPE, heads * new)
    per_batch = lambda a: pl.BlockSpec((1,) + a.shape[1:], lambda b: (b, 0, 0))
    return pl.pallas_call(
        functools.partial(_sample_attn_kernel, new=new),
        grid=(batch,),
        in_specs=[per_batch(qbd), per_batch(qr), per_batch(lat_c), per_batch(kpe_c),
                  per_batch(lat_n), per_batch(kpe_n), _const_spec(w_kv.shape),
                  _const_spec(gk.shape)],
        out_specs=pl.BlockSpec((1, new, heads * V_HEAD), lambda b: (b, 0, 0)),
        out_shape=jax.ShapeDtypeStruct((batch, new, heads * V_HEAD), BF16),
        compiler_params=pltpu.CompilerParams(
            dimension_semantics=("arbitrary",), vmem_limit_bytes=_vmem_limit(48 << 20)),
        name="sample_attention",
    )(qbd, qr, lat_c, kpe_c, lat_n, kpe_n, w_kv, gk)


_RET_SUB_CHUNKS = 4


def _retention_kernel(q_ref, k_ref, v_ref, rg_ref, dec_ref, qd_ref, kd_ref, g_ref, s0_ref,
                      o_ref, sfin_ref, state_ref, *, chunk, n_sub, n_steps):
    c = pl.program_id(1)

    @pl.when(c == 0)
    def _():
        state_ref[...] = s0_ref[0]

    heads = range(RET_HEADS)
    kcols = [slice(hd * RET_DK, (hd + 1) * RET_DK) for hd in heads]
    vcols = [slice(hd * RET_DV, (hd + 1) * RET_DV) for hd in heads]
    qd = [qd_ref[hd] for hd in heads]

    def recur(j, state):
        rows = slice(j * chunk, (j + 1) * chunk)
        q = [q_ref[rows, kcols[hd]] for hd in heads]
        k = [k_ref[rows, kcols[hd]] for hd in heads]
        v = [v_ref[rows, vcols[hd]] for hd in heads]
        s = [_qk(q[hd], k[hd]) for hd in heads]
        cross = [jnp.dot(q[hd], state[hd].astype(BF16), preferred_element_type=F32)
                 for hd in heads]
        s = [(s[hd] * dec_ref[hd]).astype(BF16) for hd in heads]
        k_dec = [(k[hd].astype(F32) * kd_ref[hd]).astype(BF16) for hd in heads]
        inner = [jnp.dot(s[hd], v[hd], preferred_element_type=F32) for hd in heads]
        kv = [lax.dot_general(k_dec[hd], v[hd], (((0,), (0,)), ((), ())),
                              preferred_element_type=F32) for hd in heads]
        new_state = [state[hd] * qd[hd][chunk - 1:chunk, :] + kv[hd] for hd in heads]
        return [inner[hd] + cross[hd] * qd[hd] for hd in heads], new_state

    def finish(j, o):
        rows = slice(j * chunk, (j + 1) * chunk)
        for hd in heads:
            mu = jnp.mean(o[hd], axis=-1, keepdims=True)
            oc = o[hd] - mu
            var = jnp.mean(oc * oc, axis=-1, keepdims=True)
            y = oc * lax.rsqrt(var + EPS) * g_ref[hd]
            rg = rg_ref[rows, vcols[hd]].astype(F32)
            o_ref[rows, vcols[hd]] = (rg * jax.nn.sigmoid(rg) * y).astype(o_ref.dtype)

    state = [state_ref[hd] for hd in heads]
    o_prev = None
    for j in range(n_sub):
        o, state = recur(j, state)
        if o_prev is not None:
            finish(j - 1, o_prev)
        o_prev = o
    finish(n_sub - 1, o_prev)
    for hd in heads:
        state_ref[hd] = state[hd]

    @pl.when(c == n_steps - 1)
    def _():
        sfin_ref[0] = state_ref[...]


def _retention_tables(chunk):
    lg = jnp.log1p(-(2.0 ** (-5.0 - jnp.arange(RET_HEADS, dtype=F32))))
    idx = jnp.arange(chunk, dtype=F32)
    diff = idx[:, None] - idx[None, :]
    decay = jnp.where(diff[None] >= 0,
                      jnp.exp(jnp.maximum(diff, 0.0)[None] * lg[:, None, None]), 0.0)
    q_dec = jnp.exp((idx + 1.0)[None, :] * lg[:, None])[..., None]
    k_dec = jnp.exp((chunk - 1.0 - idx)[None, :] * lg[:, None])[..., None]
    return decay, q_dec, k_dec


def _retention(rq, rk, rv, rg, g_ret, s0, *, batch, seq, chunk):
    n = rq.shape[0]
    n_sub = min(_RET_SUB_CHUNKS, seq // chunk)
    rows = n_sub * chunk
    ns = seq // rows
    decay, q_dec, k_dec = _retention_tables(chunk)
    tok = lambda w: pl.BlockSpec((rows, w), lambda b, c: (b * ns + c, 0))
    state_spec = pl.BlockSpec((1, RET_HEADS, RET_DK, RET_DV), lambda b, c: (b, 0, 0, 0))
    g_ret = g_ret.reshape(RET_HEADS, 1, RET_DV)
    return pl.pallas_call(
        functools.partial(_retention_kernel, chunk=chunk, n_sub=n_sub, n_steps=ns),
        grid=(batch, ns),
        in_specs=[tok(RET_HEADS * RET_DK), tok(RET_HEADS * RET_DK), tok(RET_HEADS * RET_DV),
                  tok(RET_HEADS * RET_DV), _const_spec(decay.shape), _const_spec(q_dec.shape),
                  _const_spec(k_dec.shape), _const_spec(g_ret.shape), state_spec],
        out_specs=[tok(RET_HEADS * RET_DV), state_spec],
        out_shape=(jax.ShapeDtypeStruct((n, RET_HEADS * RET_DV), BF16),
                   jax.ShapeDtypeStruct((batch, RET_HEADS, RET_DK, RET_DV), F32)),
        scratch_shapes=[pltpu.VMEM((RET_HEADS, RET_DK, RET_DV), F32)],
        compiler_params=pltpu.CompilerParams(dimension_semantics=("arbitrary", "arbitrary")),
        name="retention",
    )(rq, rk, rv, rg, decay, q_dec, k_dec, g_ret, s0)


def _merge_kernel(x_ref, a_ref, r_ref, ga_ref, gb_ref, wa_ref, wb_ref, wo_ref, o_ref):
    a_d = jnp.dot(a_ref[...], wa_ref[...], preferred_element_type=F32)
    r_d = jnp.dot(r_ref[...], wb_ref[...], preferred_element_type=F32)
    merged = (jax.nn.sigmoid(ga_ref[...].astype(F32)) * a_d
              + jax.nn.sigmoid(gb_ref[...].astype(F32)) * r_d)
    o_ref[...] = x_ref[...] + jnp.dot(merged.astype(BF16), wo_ref[...],
                                      preferred_element_type=F32)


def _merge(x, attn, ret, ga, gb, wa, wb, wo, *, tm):
    n, d = x.shape
    row = pl.BlockSpec((tm, d), lambda i: (i, 0))
    return pl.pallas_call(
        _merge_kernel,
        grid=(n // tm,),
        in_specs=[row, row, row, row, row,
                  _const_spec(wa.shape), _const_spec(wb.shape), _const_spec(wo.shape)],
        out_specs=row,
        out_shape=jax.ShapeDtypeStruct((n, d), F32),
        compiler_params=pltpu.CompilerParams(
            dimension_semantics=("arbitrary",), vmem_limit_bytes=_vmem_limit(40 << 20)),
        name="merge",
    )(x, attn, ret, ga, gb, wa, wb, wo)


_PAD_ROWS = 8


def _ffn_kernel(x_ref, g_ref, wup_ref, cw_ref, cb_ref, wdn_ref, prev_ref,
                y_ref, st_ref, ubuf_ref, *, rows, n_seqs, d_ff, n_tiles):
    j = pl.program_id(1)
    lo = _PAD_ROWS - (CONV_W - 1)
    stride = _PAD_ROWS + rows
    assert n_seqs == 1 or n_tiles == 1

    @pl.when(j == 0)
    def _():
        for b in range(n_seqs):
            ubuf_ref[b * stride + lo:b * stride + _PAD_ROWS, :] = prev_ref[b]

    if n_tiles > 1:
        @pl.when(j > 0)
        def _():
            ubuf_ref[0:_PAD_ROWS, :] = ubuf_ref[rows:rows + _PAD_ROWS, :]

    xf = x_ref[...]
    hn = (xf * _rms_scale(xf, xf.shape[-1]) * g_ref[...]).astype(BF16)
    u = jnp.dot(hn, wup_ref[...], preferred_element_type=F32)
    for b in range(n_seqs):
        ubuf_ref[b * stride + _PAD_ROWS:(b + 1) * stride, :] = u[b * rows:(b + 1) * rows, :]
    runs = []
    for b in range(n_seqs):
        c = cb_ref[...]
        for tap in range(CONV_W):
            first = b * stride + lo + tap
            c = c + ubuf_ref[first:first + rows, :] * cw_ref[tap:tap + 1, :]
        runs.append(c)
    c = runs[0] if n_seqs == 1 else jnp.concatenate(runs, axis=0)
    gate, val = c[:, :d_ff], c[:, d_ff:]
    act = (gate * jax.nn.sigmoid(gate) * val).astype(BF16)
    y_ref[...] = xf + jnp.dot(act, wdn_ref[...], preferred_element_type=F32)

    @pl.when(j == n_tiles - 1)
    def _():
        for b in range(n_seqs):
            st_ref[b] = ubuf_ref[(b + 1) * stride - (CONV_W - 1):(b + 1) * stride, :]


def _conv_ffn(x, g, w_up, cw, cb, w_dn, prev, *, batch, seq, tm):
    n, d = x.shape
    d_ff = w_dn.shape[0]
    n_seqs = max(tm // seq, 1)
    rows = tm // n_seqs
    nt = seq // rows
    row = pl.BlockSpec((tm, d), lambda b, j: (b * nt + j, 0))
    st = pl.BlockSpec((n_seqs, CONV_W - 1, 2 * d_ff), lambda b, j: (b, 0, 0))
    return pl.pallas_call(
        functools.partial(_ffn_kernel, rows=rows, n_seqs=n_seqs, d_ff=d_ff, n_tiles=nt),
        grid=(batch // n_seqs, nt),
        in_specs=[row, _const_spec(g.shape), _const_spec(w_up.shape), _const_spec(cw.shape),
                  _const_spec(cb.shape), _const_spec(w_dn.shape), st],
        out_specs=[row, st],
        out_shape=(jax.ShapeDtypeStruct((n, d), F32),
                   jax.ShapeDtypeStruct((batch, CONV_W - 1, 2 * d_ff), F32)),
        scratch_shapes=[pltpu.VMEM((n_seqs * (_PAD_ROWS + rows), 2 * d_ff), F32)],
        compiler_params=pltpu.CompilerParams(
            dimension_semantics=("arbitrary", "arbitrary"),
            vmem_limit_bytes=_vmem_limit(56 << 20)),
        name="conv_ffn",
    )(x, g, w_up, cw, cb, w_dn, prev)


def _pack_weights(w_in, w_q_up, w_kv_up, w_o_branch, w_out, w_ffn_up, w_ffn_down):
    head_width = sum(width for _, width in _IN_HEAD)
    w_head = jnp.pad(w_in[:, :_IN_HEAD_SRC_WIDTH].astype(BF16),
                     ((0, 0), (0, head_width - _IN_HEAD_SRC_WIDTH)))
    w_in_p = (w_head, w_in[:, _IN_HEAD_SRC_WIDTH:].astype(BF16))
    w_q_p = w_q_up.T.astype(BF16)
    wkv = w_kv_up.reshape(w_kv_up.shape[0], MLA_HEADS, QK_NOPE + V_HEAD)
    w_kv_p = jnp.concatenate([wkv[:, :, :QK_NOPE].reshape(wkv.shape[0], -1),
                              wkv[:, :, QK_NOPE:].reshape(wkv.shape[0], -1)], axis=1).astype(BF16)
    mla_width = MLA_HEADS * V_HEAD
    return (w_in_p, w_q_p, w_kv_p, w_o_branch[:mla_width].astype(BF16),
            w_o_branch[mla_width:].astype(BF16), w_out.astype(BF16),
            w_ffn_up.astype(BF16), w_ffn_down.astype(BF16))


class _Tiles(NamedTuple):
    tm: int
    kv_tm: int
    tq: int
    ret_chunk: int
    ffn_tm: int


_TOKEN_TILE = 512


def _tiles(batch, seq):
    if seq >= _TOKEN_TILE:
        return _Tiles(tm=_TOKEN_TILE, kv_tm=2 * _TOKEN_TILE, tq=4 * _ATTN_KEY_BLOCK,
                      ret_chunk=MXU_TILE, ffn_tm=_TOKEN_TILE)
    return _Tiles(tm=batch * seq, kv_tm=batch * seq, tq=seq, ret_chunk=seq,
                  ffn_tm=batch * seq)


def _layer(x, pos, past, w, *, batch, seq):
    n = x.shape[0]
    tm, kv_tm, tq, ret_chunk, ffn_tm = _tiles(batch, seq)
    tabs64 = _rope_tables(pos, RET_DK // 2)
    tabs32 = _rope_tables(pos, QK_ROPE // 2)
    seq_tiles = max(seq // tm, 1)
    if seq < tm:
        tabs64 = tuple(np.tile(t, (tm // seq, 1)) for t in tabs64)
        tabs32 = tuple(np.tile(t, (tm // seq, 1)) for t in tabs32)
    qlat, ckv, kpe, rq, rk, rv, rg, ga, gb = _in_proj(
        x, w['w_in'], w['g_norm_mix'], w['g_q_lat'], w['g_kv_lat'], w['g_k_rope'],
        tabs64, tabs32, seq_tiles=seq_tiles, tm=tm)
    tabs32_t = tuple(np.ascontiguousarray(t[:, :QK_ROPE].T) for t in tabs32)
    bcast = lambda g: jnp.broadcast_to(g.reshape(-1, 1), (g.size, tm))
    qt = _q_up(qlat, w['w_q_up'], bcast(w['g_q_nope']), bcast(w['g_q_rope']), tabs32_t,
               seq_tiles=seq_tiles, tm=tm)
    if past is None:
        k_new, vt_new = _kv_up(ckv, kpe, w['w_kv_up'], w['g_k_nope'], tm=kv_tm)
        attn = _prompt_attention(qt, k_new, vt_new, batch=batch, seq=seq, tq=tq)
        s0 = jnp.zeros((batch, RET_HEADS, RET_DK, RET_DV), F32)
        prev = jnp.zeros((batch, CONV_W - 1, w['w_ffn_up'].shape[1]), F32)
    else:
        lat_c, pe_c, s0, prev = past
        attn = _sample_attention(qt, lat_c, pe_c, ckv.reshape(batch, seq, -1),
                                 kpe.reshape(batch, seq, -1), w['w_kv_up'], w['g_k_nope'],
                                 batch=batch, new=seq).reshape(n, -1)
    ret, s_fin = _retention(rq, rk, rv, rg, w['g_ret_out'], s0,
                            batch=batch, seq=seq, chunk=ret_chunk)
    x1 = _merge(x, attn, ret, ga, gb, w['w_o_a'], w['w_o_b'], w['w_out'], tm=tm)
    y, conv_state = _conv_ffn(x1, w['g_norm_ffn'], w['w_ffn_up'], w['ffn_conv_w'],
                              w['ffn_conv_b'], w['w_ffn_down'], prev,
                              batch=batch, seq=seq, tm=ffn_tm)
    return y, (ckv, kpe, s_fin, conv_state)


def kernel(x_prompt, x_sample, cache_mla_latent, cache_mla_rope_key, state_retention, state_ffn_conv, g_norm_mix, w_in, g_q_lat, w_q_up, g_q_nope, g_q_rope, g_kv_lat, w_kv_up, g_k_nope, g_k_rope, g_ret_out, w_o_branch, w_out, g_norm_ffn, w_ffn_up, ffn_conv_w, ffn_conv_b, w_ffn_down):
    depth = w_in.shape[0]
    assert depth == 1, "single-layer trunk"
    bp, tp, d = x_prompt.shape
    bs, ts, _ = x_sample.shape
    past_len = cache_mla_latent.shape[2]
    (w_in_p, w_q_p, w_kv_p, w_o_a, w_o_b, w_out_p, w_up_p, w_dn_p) = _pack_weights(
        w_in[0], w_q_up[0], w_kv_up[0], w_o_branch[0], w_out[0], w_ffn_up[0], w_ffn_down[0])
    row = lambda g: g.reshape(1, -1).astype(F32)
    w = dict(
        w_in=w_in_p, w_q_up=w_q_p, w_kv_up=w_kv_p, w_o_a=w_o_a, w_o_b=w_o_b, w_out=w_out_p,
        w_ffn_up=w_up_p, w_ffn_down=w_dn_p,
        g_norm_mix=row(g_norm_mix[0]), g_q_lat=row(g_q_lat[0]), g_kv_lat=row(g_kv_lat[0]),
        g_k_rope=jnp.pad(row(g_k_rope[0]), ((0, 0), (0, LANES - QK_ROPE))),
        g_q_nope=g_q_nope[0].astype(F32), g_q_rope=g_q_rope[0].astype(F32),
        g_k_nope=row(g_k_nope[0]), g_ret_out=g_ret_out[0], g_norm_ffn=row(g_norm_ffn[0]),
        ffn_conv_w=ffn_conv_w[0], ffn_conv_b=row(ffn_conv_b[0]))

    yp, sp = _layer(x_prompt.reshape(bp * tp, d), np.arange(tp), None, w, batch=bp, seq=tp)
    past = (cache_mla_latent[0], cache_mla_rope_key[0], state_retention[0], state_ffn_conv[0])
    ys, ss = _layer(x_sample.reshape(bs * ts, d), past_len + np.arange(ts), past, w,
                    batch=bs, seq=ts)

    def states(st, b, t):
        ckv, kpe, s_fin, conv = st
        return (ckv.reshape(1, b, t, -1), kpe.reshape(1, b, t, -1), s_fin[None], conv[None])

    return (yp.reshape(bp, tp, d), ys.reshape(bs, ts, d)) + states(sp, bp, tp) + states(ss, bs, ts)
```

```python
import functools
import math
from typing import NamedTuple

import jax
import jax.numpy as jnp
import numpy as np
from jax import lax
from jax.experimental import pallas as pl
from jax.experimental.pallas import tpu as pltpu

F32 = jnp.float32
BF16 = jnp.bfloat16

EPS = 1e-6
NEG_INF = -1e30
ROPE_THETA = 10000.0
CHUNK = 64
LANES = 128
MXU_TILE = 256
V7X_VMEM_BYTES = 64 * 1024 * 1024

D_MODEL = 1024
Q_LORA = 768
KV_LORA = 512
MLA_HEADS = 8
QK_NOPE = 128
QK_ROPE = 64
QK_DIM = QK_NOPE + QK_ROPE
V_HEAD = 128
MLA_SCALE = QK_DIM ** -0.5
Q_PRESCALE = MLA_SCALE * math.log2(math.e)
RET_HEADS = 4
RET_DK = 128
RET_DV = 256
CONV_W = 3


def _vmem_limit(nbytes):
    return int(min(V7X_VMEM_BYTES - (4 << 20), max(32 << 20, nbytes)))


def _const_spec(shape):
    nd = len(shape)
    return pl.BlockSpec(shape, lambda *_: (0,) * nd, pipeline_mode=pl.Buffered(1))


def _rms_scale(z, width):
    return lax.rsqrt(jnp.sum(z * z, axis=-1, keepdims=True) * (1.0 / width) + EPS)


def _swap_halves_64(z):
    lane = lax.broadcasted_iota(jnp.int32, z.shape, 1)
    return jnp.where((lane % 64) < 32, pltpu.roll(z, 96, 1), pltpu.roll(z, 32, 1))


def _rope_tables(pos, half):
    freqs = ROPE_THETA ** (-np.arange(half, dtype=np.float64) / half)
    ang = pos.astype(np.float64)[:, None] * freqs[None, :]
    cos, sin = np.cos(ang), np.sin(ang)
    reps = LANES // (2 * half)
    return (np.tile(np.concatenate([cos, cos], -1), (1, reps)).astype(np.float32),
            np.tile(np.concatenate([-sin, sin], -1), (1, reps)).astype(np.float32))


_IN_HEAD = (('qlat', Q_LORA), ('kvl', KV_LORA), ('kpe', LANES))
_IN_TAIL = (('rq', RET_HEADS * RET_DK), ('rk', RET_HEADS * RET_DK), ('rv', RET_HEADS * RET_DV),
            ('rg', RET_HEADS * RET_DV), ('ga', D_MODEL), ('gb', D_MODEL))
_IN_HEAD_SRC_WIDTH = Q_LORA + KV_LORA + QK_ROPE


def _group_offsets(groups):
    offs, o = {}, 0
    for name, width in groups:
        offs[name] = (o, o + width)
        o += width
    return offs


_IN_OFFS = {**{n: (0,) + ab for n, ab in _group_offsets(_IN_HEAD).items()},
            **{n: (1,) + ab for n, ab in _group_offsets(_IN_TAIL).items()}}


def _inproj_kernel(x_ref, gmix_ref, wh_ref, wt_ref, gq_ref, gkv_ref, gkpe_ref,
                   c64_ref, s64_ref, c32_ref, s32_ref,
                   qlat_ref, ckv_ref, kpe_ref, rq_ref, rk_ref, rv_ref, rg_ref, ga_ref, gb_ref):
    xf = x_ref[...]
    h = (xf * _rms_scale(xf, xf.shape[-1]) * gmix_ref[...]).astype(BF16)

    def proj(name):
        part, a, b = _IN_OFFS[name]
        w_ref = (wh_ref, wt_ref)[part]
        return jnp.dot(h, w_ref[:, a:b], preferred_element_type=F32)

    z = proj('qlat')
    qlat_ref[...] = (z * _rms_scale(z, z.shape[-1]) * gq_ref[...]).astype(BF16)
    z = proj('kvl')
    ckv_ref[...] = z * _rms_scale(z, z.shape[-1]) * gkv_ref[...]
    z = proj('kpe')
    zn = z * _rms_scale(z, QK_ROPE) * gkpe_ref[...]
    kpe = zn * c32_ref[...] + _swap_halves_64(zn) * s32_ref[...]
    kpe_ref[...] = kpe[:, :QK_ROPE]
    c64, s64 = c64_ref[...], s64_ref[...]
    for name, ref, scale in (('rq', rq_ref, 1.0), ('rk', rk_ref, RET_DK ** -0.5)):
        z = proj(name)
        for hd in range(RET_HEADS):
            zh = z[:, hd * RET_DK:(hd + 1) * RET_DK]
            r = zh * c64 + pltpu.roll(zh, RET_DK // 2, 1) * s64
            if scale != 1.0:
                r = r * scale
            ref[:, hd * RET_DK:(hd + 1) * RET_DK] = r.astype(BF16)
    rv_ref[...] = proj('rv').astype(BF16)
    rg_ref[...] = proj('rg').astype(BF16)
    ga_ref[...] = proj('ga').astype(BF16)
    gb_ref[...] = proj('gb').astype(BF16)


def _in_proj(x, w_in_parts, gmix, gq, gkv, gkpe, tabs64, tabs32, *, seq_tiles, tm):
    n, d = x.shape
    w_head, w_tail = w_in_parts
    row = lambda w: pl.BlockSpec((tm, w), lambda i: (i, 0))
    tab = pl.BlockSpec((tm, LANES), lambda i: (i % seq_tiles, 0))
    width = dict(_IN_HEAD + _IN_TAIL)
    out_dtypes = dict(qlat=BF16, kvl=F32, kpe=F32, rq=BF16, rk=BF16, rv=BF16, rg=BF16,
                      ga=BF16, gb=BF16)
    out_shapes = tuple(
        jax.ShapeDtypeStruct((n, QK_ROPE if name == 'kpe' else width[name]), dtype)
        for name, dtype in out_dtypes.items())
    return pl.pallas_call(
        _inproj_kernel,
        grid=(n // tm,),
        in_specs=[row(d), _const_spec(gmix.shape), _const_spec(w_head.shape),
                  _const_spec(w_tail.shape), _const_spec(gq.shape), _const_spec(gkv.shape),
                  _const_spec(gkpe.shape), tab, tab, tab, tab],
        out_specs=[row(s.shape[1]) for s in out_shapes],
        out_shape=out_shapes,
        compiler_params=pltpu.CompilerParams(
            dimension_semantics=("arbitrary",), vmem_limit_bytes=_vmem_limit(56 << 20)),
        name="in_proj",
    )(x, gmix, w_head, w_tail, gq, gkv, gkpe, *tabs64, *tabs32)


_QUP_GROUPS = 2


def _qup_kernel(ql_ref, wt_ref, gn_ref, gr_ref, c_ref, s_ref, qt_ref):
    gn, gr, cos, sin = gn_ref[...], gr_ref[...], c_ref[...], s_ref[...]
    half = QK_ROPE // 2
    ql = ql_ref[...]

    group = MLA_HEADS // _QUP_GROUPS

    def project(g):
        return lax.dot_general(wt_ref[g * group * QK_DIM:(g + 1) * group * QK_DIM, :], ql,
                               (((1,), (1,)), ((), ())), preferred_element_type=F32)

    def finish(g, zt):
        for j in range(group):
            hd = g * group + j
            zn = zt[j * QK_DIM:j * QK_DIM + QK_NOPE, :]
            inv = lax.rsqrt(jnp.sum(zn * zn, axis=0, keepdims=True) * (1.0 / QK_NOPE) + EPS)
            qt_ref[hd, :QK_NOPE, :] = (zn * (inv * Q_PRESCALE) * gn).astype(BF16)
            zr = zt[j * QK_DIM + QK_NOPE:(j + 1) * QK_DIM, :]
            inv = lax.rsqrt(jnp.sum(zr * zr, axis=0, keepdims=True) * (1.0 / QK_ROPE) + EPS)
            z = zr * (inv * Q_PRESCALE) * gr
            swapped = jnp.concatenate([z[half:], z[:half]], axis=0)
            qt_ref[hd, QK_NOPE:, :] = (z * cos + swapped * sin).astype(BF16)

    zt = project(0)
    for g in range(_QUP_GROUPS):
        zt_next = project(g + 1) if g + 1 < _QUP_GROUPS else None
        finish(g, zt)
        zt = zt_next


def _q_up(qlat, w_qt, gn, gr, tabs32_t, *, seq_tiles, tm):
    n = qlat.shape[0]
    tab = pl.BlockSpec((QK_ROPE, tm), lambda i: (0, i % seq_tiles))
    return pl.pallas_call(
        _qup_kernel,
        grid=(n // tm,),
        in_specs=[pl.BlockSpec((tm, qlat.shape[1]), lambda i: (i, 0)), _const_spec(w_qt.shape),
                  _const_spec(gn.shape), _const_spec(gr.shape), tab, tab],
        out_specs=pl.BlockSpec((MLA_HEADS, QK_DIM, tm), lambda i: (0, 0, i)),
        out_shape=jax.ShapeDtypeStruct((MLA_HEADS, QK_DIM, n), BF16),
        compiler_params=pltpu.CompilerParams(
            dimension_semantics=("arbitrary",), vmem_limit_bytes=_vmem_limit(40 << 20)),
        name="q_up",
    )(qlat, w_qt, gn, gr, *tabs32_t)


_ONES_ROWS = 16


def _kvup_kernel(c_ref, kpe_ref, w_ref, gk_ref, k_ref, vt_ref):
    c = c_ref[...].astype(BF16)
    kpe = kpe_ref[...].astype(BF16)
    gk = gk_ref[...]
    kw = MLA_HEADS * QK_NOPE
    n_pairs = MLA_HEADS // 2

    def project(pair):
        cols = slice(pair * MXU_TILE, (pair + 1) * MXU_TILE)
        zk = jnp.dot(c, w_ref[:, cols], preferred_element_type=F32)
        zv = jnp.dot(c, w_ref[:, kw + cols.start:kw + cols.stop], preferred_element_type=F32)
        return zk, zv

    def finish(pair, zk, zv):
        for sub in range(2):
            hd = 2 * pair + sub
            zh = zk[:, sub * QK_NOPE:(sub + 1) * QK_NOPE]
            k_ref[hd, :, :QK_NOPE] = (zh * _rms_scale(zh, QK_NOPE) * gk).astype(BF16)
            k_ref[hd, :, QK_NOPE:] = kpe
            vh = zv[:, sub * V_HEAD:(sub + 1) * V_HEAD]
            vt_ref[hd, :V_HEAD, :] = vh.T.astype(BF16)
            vt_ref[hd, V_HEAD:, :] = jnp.ones((_ONES_ROWS, vh.shape[0]), BF16)

    z = project(0)
    for pair in range(n_pairs):
        z_next = project(pair + 1) if pair + 1 < n_pairs else None
        finish(pair, *z)
        z = z_next


def _kv_up(ckv, kpe, w_kv, gk, *, tm):
    n = ckv.shape[0]
    v_spec = pl.BlockSpec((MLA_HEADS, V_HEAD + _ONES_ROWS, tm), lambda i: (0, 0, i))
    v_shape = (MLA_HEADS, V_HEAD + _ONES_ROWS, n)
    return pl.pallas_call(
        _kvup_kernel,
        grid=(n // tm,),
        in_specs=[pl.BlockSpec((tm, ckv.shape[1]), lambda i: (i, 0)),
                  pl.BlockSpec((tm, QK_ROPE), lambda i: (i, 0)),
                  _const_spec(w_kv.shape), _const_spec(gk.shape)],
        out_specs=[pl.BlockSpec((MLA_HEADS, tm, QK_DIM), lambda i: (0, i, 0)), v_spec],
        out_shape=(jax.ShapeDtypeStruct((MLA_HEADS, n, QK_DIM), BF16),
                   jax.ShapeDtypeStruct(v_shape, BF16)),
        compiler_params=pltpu.CompilerParams(
            dimension_semantics=("arbitrary",), vmem_limit_bytes=_vmem_limit(40 << 20)),
        name="kv_up",
    )(ckv, kpe, w_kv, gk)


def _qk(q, k):
    return lax.dot_general(q, k, (((1,), (1,)), ((), ())), preferred_element_type=F32)


_STRIP = MXU_TILE
_ATTN_KEY_BLOCK = 512


def _prompt_attn_kernel(qt_ref, k_ref, vt_ref, o_ref, s_ref, smax_ref, p_ref, a_ref, m_ref,
                        acc_ref, *, tq, tk):
    n_tiles = qt_ref.shape[2] // tq
    n_strips = tq // _STRIP
    n_diag = tq // tk
    assert n_diag % 2 == 0 and tk % _STRIP == 0 and _STRIP % CHUNK == 0
    all_strips = tuple(range(n_strips))
    cols = lambda c: slice(c * _STRIP, (c + 1) * _STRIP)
    n_seen = lambda d, c: max(0, min(tk, (c + 1) * _STRIP - d * tk))
    seen = lambda d: tuple(c for c in all_strips if n_seen(d, c) > 0)
    unmasked = lambda d, c: c * _STRIP >= (d + 1) * tk

    def scores(i, b, slot, c, nk=tk):
        start = pl.multiple_of(b * tk, tk)
        q_start = pl.multiple_of(i * tq + c * _STRIP, _STRIP)
        s = jnp.dot(k_ref[0, pl.ds(start, nk), :], qt_ref[0, :, pl.ds(q_start, _STRIP)],
                    preferred_element_type=F32)
        s_ref[slot, :nk, cols(c)] = s
        smax_ref[slot, :, cols(c)] = jnp.max(s, axis=0, keepdims=True)

    def softmax(slot, c, diag_block=None):
        nk = tk if diag_block is None else n_seen(diag_block, c)
        s = s_ref[slot, :nk, cols(c)]
        if diag_block is None or unmasked(diag_block, c):
            s_max = smax_ref[slot, :, cols(c)]
        else:
            kc = (lax.broadcasted_iota(jnp.int32, s.shape, 0) + diag_block * tk) // CHUNK
            qc = (lax.broadcasted_iota(jnp.int32, s.shape, 1) + c * _STRIP) // CHUNK
            s = jnp.where(kc <= qc, s, NEG_INF)
            s_max = jnp.max(s, axis=0, keepdims=True)
        m_prev = m_ref[:, cols(c)]
        m_new = jnp.maximum(m_prev, s_max)
        alpha = jnp.exp2(m_prev - m_new)
        p = jnp.exp2(s - m_new)
        m_ref[:, cols(c)] = m_new
        a_ref[slot, :, cols(c)] = alpha
        p_ref[slot, :nk, cols(c)] = p.astype(BF16)

    def values(b, slot, c, nk=tk):
        start = pl.multiple_of(jnp.maximum(b, 0) * tk, tk)
        pv = jnp.dot(vt_ref[0, :, pl.ds(start, nk)], p_ref[slot, :nk, cols(c)],
                     preferred_element_type=F32)
        acc_ref[:, cols(c)] = acc_ref[:, cols(c)] * a_ref[slot, :, cols(c)] + pv

    def first_scores(i):
        for c in all_strips:
            scores(i, 0, 0, c)

    def reset_state():
        m_ref[...] = jnp.full(m_ref.shape, NEG_INF, F32)
        acc_ref[...] = jnp.zeros(acc_ref.shape, F32)
        p_ref[1] = jnp.zeros(p_ref.shape[1:], BF16)
        a_ref[1] = jnp.ones(a_ref.shape[1:], F32)

    def tile(i, has_next):
        def pair(jj, carry):
            b = 2 * jj
            for c in all_strips:
                scores(i, b + 1, 1, c)
            for c in all_strips:
                values(b - 1, 1, c)
            for c in all_strips:
                softmax(0, c)
            for c in all_strips:
                scores(i, b + 2, 0, c)
            for c in all_strips:
                values(b, 0, c)
            for c in all_strips:
                softmax(1, c)
            return carry

        lax.fori_loop(0, i * (n_diag // 2), pair, 0)
        b0 = n_diag * i
        order = (0,) + tuple(range(n_diag - 1, 0, -1))
        for t in range(n_diag + 1):
            if 1 <= t + 1 < n_diag:
                d = order[t + 1]
                for c in seen(d):
                    scores(i, b0 + d, (t + 1) % 2, c, n_seen(d, c))
            if t == n_diag - 1 and has_next:
                first_scores(i + 1)
            if t == 0:
                for c in all_strips:
                    values(b0 - 1, 1, c)
            else:
                d = order[t - 1]
                for c in seen(d):
                    values(b0 + d, (t - 1) % 2, c, n_seen(d, c))
            if t < n_diag:
                d = order[t]
                for c in seen(d):
                    softmax(t % 2, c, diag_block=d)
        o = acc_ref[:V_HEAD, :] / acc_ref[V_HEAD:V_HEAD + 1, :]
        o_ref[pl.ds(pl.multiple_of(i * tq, tq), tq), :] = o.T.astype(o_ref.dtype)
        if has_next:
            reset_state()

    def tile_with_next(i, carry):
        tile(i, has_next=True)
        return carry

    first_scores(0)
    reset_state()
    lax.fori_loop(0, n_tiles - 1, tile_with_next, 0)
    tile(jnp.int32(n_tiles - 1), has_next=False)


def _prompt_attention(qt, k, vt, *, batch, seq, tq):
    heads, _, n = qt.shape
    tk = _ATTN_KEY_BLOCK
    return pl.pallas_call(
        functools.partial(_prompt_attn_kernel, tq=tq, tk=tk),
        grid=(batch, heads),
        in_specs=[pl.BlockSpec((1, QK_DIM, seq), lambda b, h: (h, 0, b)),
                  pl.BlockSpec((1, seq, QK_DIM), lambda b, h: (h, b, 0)),
                  pl.BlockSpec((1, vt.shape[1], seq), lambda b, h: (h, 0, b))],
        out_specs=pl.BlockSpec((seq, V_HEAD), lambda b, h: (b, h)),
        out_shape=jax.ShapeDtypeStruct((n, heads * V_HEAD), BF16),
        scratch_shapes=[pltpu.VMEM((2, tk, tq), F32), pltpu.VMEM((2, 1, tq), F32),
                        pltpu.VMEM((2, tk, tq), BF16),
                        pltpu.VMEM((2, 1, tq), F32), pltpu.VMEM((1, tq), F32),
                        pltpu.VMEM((vt.shape[1], tq), F32)],
        compiler_params=pltpu.CompilerParams(
            dimension_semantics=("arbitrary", "arbitrary"),
            vmem_limit_bytes=_vmem_limit(48 << 20)),
        name="prompt_attention",
    )(qt, k, vt)


def _sample_attn_kernel(qbd_ref, qr_ref, latc_ref, kpec_ref, latn_ref, kpen_ref, w_ref, gk_ref,
                        o_ref, *, new):
    gk = gk_ref[...]
    kw = MLA_HEADS * QK_NOPE

    def scores(lat_ref, kpe_ref):
        c = lat_ref[0].astype(BF16)
        k_heads = []
        for pair in range(MLA_HEADS // 2):
            zk = jnp.dot(c, w_ref[:, pair * MXU_TILE:(pair + 1) * MXU_TILE],
                         preferred_element_type=F32)
            for sub in range(2):
                zh = zk[:, sub * QK_NOPE:(sub + 1) * QK_NOPE]
                k_heads.append((zh * _rms_scale(zh, QK_NOPE) * gk).astype(BF16))
        k_all = jnp.concatenate(k_heads, axis=1)
        s = (jnp.dot(k_all, qbd_ref[0], preferred_element_type=F32)
             + jnp.dot(kpe_ref[0].astype(BF16), qr_ref[0], preferred_element_type=F32))
        return s, c

    s_c, c_c = scores(latc_ref, kpec_ref)
    s_n, c_n = scores(latn_ref, kpen_ref)
    m = jnp.maximum(jnp.max(s_c, axis=0, keepdims=True), jnp.max(s_n, axis=0, keepdims=True))
    p_c = jnp.exp2(s_c - m).astype(BF16)
    p_n = jnp.exp2(s_n - m).astype(BF16)
    tn = (((0,), (0,)), ((), ()))
    ctx = (lax.dot_general(p_c, c_c, tn, preferred_element_type=F32)
           + lax.dot_general(p_n, c_n, tn, preferred_element_type=F32))
    den = (lax.dot_general(p_c, jnp.ones((p_c.shape[0], V_HEAD), BF16), tn,
                           preferred_element_type=F32)
           + lax.dot_general(p_n, jnp.ones((p_n.shape[0], V_HEAD), BF16), tn,
                             preferred_element_type=F32))
    ctx = ctx.astype(BF16)
    for hd in range(MLA_HEADS):
        rows = slice(hd * new, (hd + 1) * new)
        o = jnp.dot(ctx[rows, :], w_ref[:, kw + hd * V_HEAD:kw + (hd + 1) * V_HEAD],
                    preferred_element_type=F32)
        o_ref[0, :, hd * V_HEAD:(hd + 1) * V_HEAD] = (o / den[rows, :]).astype(o_ref.dtype)


def _sample_attention(qt, lat_c, kpe_c, lat_n, kpe_n, w_kv, gk, *, batch, new):
    heads = qt.shape[0]
    q4 = qt.reshape(heads, QK_DIM, batch, new)
    qn = jnp.transpose(q4[:, :QK_NOPE], (2, 0, 1, 3))
    qbd = jnp.einsum('bhdj,hg->bhdgj', qn, jnp.eye(heads, dtype=qn.dtype))
    qbd = qbd.reshape(batch, heads * QK_NOPE, heads * new)
    qr = jnp.transpose(q4[:, QK_NOPE:], (2, 1, 0, 3)).reshape(batch, QK_ROPE, heads * new)
    per_batch = lambda a: pl.BlockSpec((1,) + a.shape[1:], lambda b: (b, 0, 0))
    return pl.pallas_call(
        functools.partial(_sample_attn_kernel, new=new),
        grid=(batch,),
        in_specs=[per_batch(qbd), per_batch(qr), per_batch(lat_c), per_batch(kpe_c),
                  per_batch(lat_n), per_batch(kpe_n), _const_spec(w_kv.shape),
                  _const_spec(gk.shape)],
        out_specs=pl.BlockSpec((1, new, heads * V_HEAD), lambda b: (b, 0, 0)),
        out_shape=jax.ShapeDtypeStruct((batch, new, heads * V_HEAD), BF16),
        compiler_params=pltpu.CompilerParams(
            dimension_semantics=("arbitrary",), vmem_limit_bytes=_vmem_limit(48 << 20)),
        name="sample_attention",
    )(qbd, qr, lat_c, kpe_c, lat_n, kpe_n, w_kv, gk)


_RET_SUB_CHUNKS = 4


def _retention_kernel(q_ref, k_ref, v_ref, rg_ref, dec_ref, qd_ref, kd_ref, g_ref, s0_ref,
                      o_ref, sfin_ref, state_ref, *, chunk, n_sub, n_steps):
    c = pl.program_id(1)

    @pl.when(c == 0)
    def _():
        state_ref[...] = s0_ref[0]

    heads = range(RET_HEADS)
    kcols = [slice(hd * RET_DK, (hd + 1) * RET_DK) for hd in heads]
    vcols = [slice(hd * RET_DV, (hd + 1) * RET_DV) for hd in heads]
    qd = [qd_ref[hd] for hd in heads]

    def recur(j, state):
        rows = slice(j * chunk, (j + 1) * chunk)
        q = [q_ref[rows, kcols[hd]] for hd in heads]
        k = [k_ref[rows, kcols[hd]] for hd in heads]
        v = [v_ref[rows, vcols[hd]] for hd in heads]
        s = [_qk(q[hd], k[hd]) for hd in heads]
        cross = [jnp.dot(q[hd], state[hd].astype(BF16), preferred_element_type=F32)
                 for hd in heads]
        s = [(s[hd] * dec_ref[hd]).astype(BF16) for hd in heads]
        k_dec = [(k[hd].astype(F32) * kd_ref[hd]).astype(BF16) for hd in heads]
        inner = [jnp.dot(s[hd], v[hd], preferred_element_type=F32) for hd in heads]
        kv = [lax.dot_general(k_dec[hd], v[hd], (((0,), (0,)), ((), ())),
                              preferred_element_type=F32) for hd in heads]
        new_state = [state[hd] * qd[hd][chunk - 1:chunk, :] + kv[hd] for hd in heads]
        return [inner[hd] + cross[hd] * qd[hd] for hd in heads], new_state

    def finish(j, o):
        rows = slice(j * chunk, (j + 1) * chunk)
        for hd in heads:
            mu = jnp.mean(o[hd], axis=-1, keepdims=True)
            oc = o[hd] - mu
            var = jnp.mean(oc * oc, axis=-1, keepdims=True)
            y = oc * lax.rsqrt(var + EPS) * g_ref[hd]
            rg = rg_ref[rows, vcols[hd]].astype(F32)
            o_ref[rows, vcols[hd]] = (rg * jax.nn.sigmoid(rg) * y).astype(o_ref.dtype)

    state = [state_ref[hd] for hd in heads]
    o_prev = None
    for j in range(n_sub):
        o, state = recur(j, state)
        if o_prev is not None:
            finish(j - 1, o_prev)
        o_prev = o
    finish(n_sub - 1, o_prev)
    for hd in heads:
        state_ref[hd] = state[hd]

    @pl.when(c == n_steps - 1)
    def _():
        sfin_ref[0] = state_ref[...]


def _retention_tables(chunk):
    lg = jnp.log1p(-(2.0 ** (-5.0 - jnp.arange(RET_HEADS, dtype=F32))))
    idx = jnp.arange(chunk, dtype=F32)
    diff = idx[:, None] - idx[None, :]
    decay = jnp.where(diff[None] >= 0,
                      jnp.exp(jnp.maximum(diff, 0.0)[None] * lg[:, None, None]), 0.0)
    q_dec = jnp.exp((idx + 1.0)[None, :] * lg[:, None])[..., None]
    k_dec = jnp.exp((chunk - 1.0 - idx)[None, :] * lg[:, None])[..., None]
    return decay, q_dec, k_dec


def _retention(rq, rk, rv, rg, g_ret, s0, *, batch, seq, chunk):
    n = rq.shape[0]
    n_sub = min(_RET_SUB_CHUNKS, seq // chunk)
    rows = n_sub * chunk
    ns = seq // rows
    decay, q_dec, k_dec = _retention_tables(chunk)
    tok = lambda w: pl.BlockSpec((rows, w), lambda b, c: (b * ns + c, 0))
    state_spec = pl.BlockSpec((1, RET_HEADS, RET_DK, RET_DV), lambda b, c: (b, 0, 0, 0))
    g_ret = g_ret.reshape(RET_HEADS, 1, RET_DV)
    return pl.pallas_call(
        functools.partial(_retention_kernel, chunk=chunk, n_sub=n_sub, n_steps=ns),
        grid=(batch, ns),
        in_specs=[tok(RET_HEADS * RET_DK), tok(RET_HEADS * RET_DK), tok(RET_HEADS * RET_DV),
                  tok(RET_HEADS * RET_DV), _const_spec(decay.shape), _const_spec(q_dec.shape),
                  _const_spec(k_dec.shape), _const_spec(g_ret.shape), state_spec],
        out_specs=[tok(RET_HEADS * RET_DV), state_spec],
        out_shape=(jax.ShapeDtypeStruct((n, RET_HEADS * RET_DV), BF16),
                   jax.ShapeDtypeStruct((batch, RET_HEADS, RET_DK, RET_DV), F32)),
        scratch_shapes=[pltpu.VMEM((RET_HEADS, RET_DK, RET_DV), F32)],
        compiler_params=pltpu.CompilerParams(dimension_semantics=("arbitrary", "arbitrary")),
        name="retention",
    )(rq, rk, rv, rg, decay, q_dec, k_dec, g_ret, s0)


def _merge_kernel(x_ref, a_ref, r_ref, ga_ref, gb_ref, wa_ref, wb_ref, wo_ref, o_ref):
    a_d = jnp.dot(a_ref[...], wa_ref[...], preferred_element_type=F32)
    r_d = jnp.dot(r_ref[...], wb_ref[...], preferred_element_type=F32)
    merged = (jax.nn.sigmoid(ga_ref[...].astype(F32)) * a_d
              + jax.nn.sigmoid(gb_ref[...].astype(F32)) * r_d)
    o_ref[...] = x_ref[...] + jnp.dot(merged.astype(BF16), wo_ref[...],
                                      preferred_element_type=F32)


def _merge(x, attn, ret, ga, gb, wa, wb, wo, *, tm):
    n, d = x.shape
    row = pl.BlockSpec((tm, d), lambda i: (i, 0))
    return pl.pallas_call(
        _merge_kernel,
        grid=(n // tm,),
        in_specs=[row, row, row, row, row,
                  _const_spec(wa.shape), _const_spec(wb.shape), _const_spec(wo.shape)],
        out_specs=row,
        out_shape=jax.ShapeDtypeStruct((n, d), F32),
        compiler_params=pltpu.CompilerParams(
            dimension_semantics=("arbitrary",), vmem_limit_bytes=_vmem_limit(40 << 20)),
        name="merge",
    )(x, attn, ret, ga, gb, wa, wb, wo)


_PAD_ROWS = 8


def _ffn_kernel(x_ref, g_ref, wup_ref, cw_ref, cb_ref, wdn_ref, prev_ref,
                y_ref, st_ref, ubuf_ref, *, rows, n_seqs, d_ff, n_tiles):
    j = pl.program_id(1)
    lo = _PAD_ROWS - (CONV_W - 1)
    stride = _PAD_ROWS + rows
    assert n_seqs == 1 or n_tiles == 1

    @pl.when(j == 0)
    def _():
        for b in range(n_seqs):
            ubuf_ref[b * stride + lo:b * stride + _PAD_ROWS, :] = prev_ref[b]

    if n_tiles > 1:
        @pl.when(j > 0)
        def _():
            ubuf_ref[0:_PAD_ROWS, :] = ubuf_ref[rows:rows + _PAD_ROWS, :]

    xf = x_ref[...]
    hn = (xf * _rms_scale(xf, xf.shape[-1]) * g_ref[...]).astype(BF16)
    u = jnp.dot(hn, wup_ref[...], preferred_element_type=F32)
    for b in range(n_seqs):
        ubuf_ref[b * stride + _PAD_ROWS:(b + 1) * stride, :] = u[b * rows:(b + 1) * rows, :]
    runs = []
    for b in range(n_seqs):
        c = cb_ref[...]
        for tap in range(CONV_W):
            first = b * stride + lo + tap
            c = c + ubuf_ref[first:first + rows, :] * cw_ref[tap:tap + 1, :]
        runs.append(c)
    c = runs[0] if n_seqs == 1 else jnp.concatenate(runs, axis=0)
    gate, val = c[:, :d_ff], c[:, d_ff:]
    act = (gate * jax.nn.sigmoid(gate) * val).astype(BF16)
    y_ref[...] = xf + jnp.dot(act, wdn_ref[...], preferred_element_type=F32)

    @pl.when(j == n_tiles - 1)
    def _():
        for b in range(n_seqs):
            st_ref[b] = ubuf_ref[(b + 1) * stride - (CONV_W - 1):(b + 1) * stride, :]


def _conv_ffn(x, g, w_up, cw, cb, w_dn, prev, *, batch, seq, tm):
    n, d = x.shape
    d_ff = w_dn.shape[0]
    n_seqs = max(tm // seq, 1)
    rows = tm // n_seqs
    nt = seq // rows
    row = pl.BlockSpec((tm, d), lambda b, j: (b * nt + j, 0))
    st = pl.BlockSpec((n_seqs, CONV_W - 1, 2 * d_ff), lambda b, j: (b, 0, 0))
    return pl.pallas_call(
        functools.partial(_ffn_kernel, rows=rows, n_seqs=n_seqs, d_ff=d_ff, n_tiles=nt),
        grid=(batch // n_seqs, nt),
        in_specs=[row, _const_spec(g.shape), _const_spec(w_up.shape), _const_spec(cw.shape),
                  _const_spec(cb.shape), _const_spec(w_dn.shape), st],
        out_specs=[row, st],
        out_shape=(jax.ShapeDtypeStruct((n, d), F32),
                   jax.ShapeDtypeStruct((batch, CONV_W - 1, 2 * d_ff), F32)),
        scratch_shapes=[pltpu.VMEM((n_seqs * (_PAD_ROWS + rows), 2 * d_ff), F32)],
        compiler_params=pltpu.CompilerParams(
            dimension_semantics=("arbitrary", "arbitrary"),
            vmem_limit_bytes=_vmem_limit(56 << 20)),
        name="conv_ffn",
    )(x, g, w_up, cw, cb, w_dn, prev)


def _pack_weights(w_in, w_q_up, w_kv_up, w_o_branch, w_out, w_ffn_up, w_ffn_down):
    head_width = sum(width for _, width in _IN_HEAD)
    w_head = jnp.pad(w_in[:, :_IN_HEAD_SRC_WIDTH].astype(BF16),
                     ((0, 0), (0, head_width - _IN_HEAD_SRC_WIDTH)))
    w_in_p = (w_head, w_in[:, _IN_HEAD_SRC_WIDTH:].astype(BF16))
    w_q_p = w_q_up.T.astype(BF16)
    wkv = w_kv_up.reshape(w_kv_up.shape[0], MLA_HEADS, QK_NOPE + V_HEAD)
    w_kv_p = jnp.concatenate([wkv[:, :, :QK_NOPE].reshape(wkv.shape[0], -1),
                              wkv[:, :, QK_NOPE:].reshape(wkv.shape[0], -1)], axis=1).astype(BF16)
    mla_width = MLA_HEADS * V_HEAD
    return (w_in_p, w_q_p, w_kv_p, w_o_branch[:mla_width].astype(BF16),
            w_o_branch[mla_width:].astype(BF16), w_out.astype(BF16),
            w_ffn_up.astype(BF16), w_ffn_down.astype(BF16))


class _Tiles(NamedTuple):
    tm: int
    kv_tm: int
    tq: int
    ret_chunk: int
    ffn_tm: int


_TOKEN_TILE = 512


def _tiles(batch, seq):
    if seq >= _TOKEN_TILE:
        return _Tiles(tm=_TOKEN_TILE, kv_tm=2 * _TOKEN_TILE, tq=4 * _ATTN_KEY_BLOCK,
                      ret_chunk=MXU_TILE, ffn_tm=_TOKEN_TILE)
    return _Tiles(tm=batch * seq, kv_tm=batch * seq, tq=seq, ret_chunk=seq,
                  ffn_tm=batch * seq)


def _layer(x, pos, past, w, *, batch, seq):
    n = x.shape[0]
    tm, kv_tm, tq, ret_chunk, ffn_tm = _tiles(batch, seq)
    tabs64 = _rope_tables(pos, RET_DK // 2)
    tabs32 = _rope_tables(pos, QK_ROPE // 2)
    seq_tiles = max(seq // tm, 1)
    if seq < tm:
        tabs64 = tuple(np.tile(t, (tm // seq, 1)) for t in tabs64)
        tabs32 = tuple(np.tile(t, (tm // seq, 1)) for t in tabs32)
    qlat, ckv, kpe, rq, rk, rv, rg, ga, gb = _in_proj(
        x, w['w_in'], w['g_norm_mix'], w['g_q_lat'], w['g_kv_lat'], w['g_k_rope'],
        tabs64, tabs32, seq_tiles=seq_tiles, tm=tm)
    tabs32_t = tuple(np.ascontiguousarray(t[:, :QK_ROPE].T) for t in tabs32)
    bcast = lambda g: jnp.broadcast_to(g.reshape(-1, 1), (g.size, tm))
    qt = _q_up(qlat, w['w_q_up'], bcast(w['g_q_nope']), bcast(w['g_q_rope']), tabs32_t,
               seq_tiles=seq_tiles, tm=tm)
    if past is None:
        k_new, vt_new = _kv_up(ckv, kpe, w['w_kv_up'], w['g_k_nope'], tm=kv_tm)
        attn = _prompt_attention(qt, k_new, vt_new, batch=batch, seq=seq, tq=tq)
        s0 = jnp.zeros((batch, RET_HEADS, RET_DK, RET_DV), F32)
        prev = jnp.zeros((batch, CONV_W - 1, w['w_ffn_up'].shape[1]), F32)
    else:
        lat_c, pe_c, s0, prev = past
        attn = _sample_attention(qt, lat_c, pe_c, ckv.reshape(batch, seq, -1),
                                 kpe.reshape(batch, seq, -1), w['w_kv_up'], w['g_k_nope'],
                                 batch=batch, new=seq).reshape(n, -1)
    ret, s_fin = _retention(rq, rk, rv, rg, w['g_ret_out'], s0,
                            batch=batch, seq=seq, chunk=ret_chunk)
    x1 = _merge(x, attn, ret, ga, gb, w['w_o_a'], w['w_o_b'], w['w_out'], tm=tm)
    y, conv_state = _conv_ffn(x1, w['g_norm_ffn'], w['w_ffn_up'], w['ffn_conv_w'],
                              w['ffn_conv_b'], w['w_ffn_down'], prev,
                              batch=batch, seq=seq, tm=ffn_tm)
    return y, (ckv, kpe, s_fin, conv_state)


def kernel(x_prompt, x_sample, cache_mla_latent, cache_mla_rope_key, state_retention, state_ffn_conv, g_norm_mix, w_in, g_q_lat, w_q_up, g_q_nope, g_q_rope, g_kv_lat, w_kv_up, g_k_nope, g_k_rope, g_ret_out, w_o_branch, w_out, g_norm_ffn, w_ffn_up, ffn_conv_w, ffn_conv_b, w_ffn_down):
    depth = w_in.shape[0]
    assert depth == 1, "single-layer trunk"
    bp, tp, d = x_prompt.shape
    bs, ts, _ = x_sample.shape
    past_len = cache_mla_latent.shape[2]
    (w_in_p, w_q_p, w_kv_p, w_o_a, w_o_b, w_out_p, w_up_p, w_dn_p) = _pack_weights(
        w_in[0], w_q_up[0], w_kv_up[0], w_o_branch[0], w_out[0], w_ffn_up[0], w_ffn_down[0])
    row = lambda g: g.reshape(1, -1).astype(F32)
    w = dict(
        w_in=w_in_p, w_q_up=w_q_p, w_kv_up=w_kv_p, w_o_a=w_o_a, w_o_b=w_o_b, w_out=w_out_p,
        w_ffn_up=w_up_p, w_ffn_down=w_dn_p,
        g_norm_mix=row(g_norm_mix[0]), g_q_lat=row(g_q_lat[0]), g_kv_lat=row(g_kv_lat[0]),
        g_k_rope=jnp.pad(row(g_k_rope[0]), ((0, 0), (0, LANES - QK_ROPE))),
        g_q_nope=g_q_nope[0].astype(F32), g_q_rope=g_q_rope[0].astype(F32),
        g_k_nope=row(g_k_nope[0]), g_ret_out=g_ret_out[0], g_norm_ffn=row(g_norm_ffn[0]),
        ffn_conv_w=ffn_conv_w[0], ffn_conv_b=row(ffn_conv_b[0]))

    yp, sp = _layer(x_prompt.reshape(bp * tp, d), np.arange(tp), None, w, batch=bp, seq=tp)
    past = (cache_mla_latent[0], cache_mla_rope_key[0], state_retention[0], state_ffn_conv[0])
    ys, ss = _layer(x_sample.reshape(bs * ts, d), past_len + np.arange(ts), past, w,
                    batch=bs, seq=ts)

    def states(st, b, t):
        ckv, kpe, s_fin, conv = st
        return (ckv.reshape(1, b, t, -1), kpe.reshape(1, b, t, -1), s_fin[None], conv[None])

    return (yp.reshape(bp, tp, d), ys.reshape(bs, ts, d)) + states(sp, bp, tp) + states(ss, bs, ts)
```

```python
import functools
import math
from typing import NamedTuple

import jax
import jax.numpy as jnp
import numpy as np
from jax import lax
from jax.experimental import pallas as pl
from jax.experimental.pallas import tpu as pltpu

F32 = jnp.float32
BF16 = jnp.bfloat16

EPS = 1e-6
NEG_INF = -1e30
ROPE_THETA = 10000.0
CHUNK = 64
LANES = 128
MXU_TILE = 256
V7X_VMEM_BYTES = 64 * 1024 * 1024

D_MODEL = 1024
Q_LORA = 768
KV_LORA = 512
MLA_HEADS = 8
QK_NOPE = 128
QK_ROPE = 64
QK_DIM = QK_NOPE + QK_ROPE
V_HEAD = 128
MLA_SCALE = QK_DIM ** -0.5
Q_PRESCALE = MLA_SCALE * math.log2(math.e)
RET_HEADS = 4
RET_DK = 128
RET_DV = 256
CONV_W = 3


def _vmem_limit(nbytes):
    return int(min(V7X_VMEM_BYTES - (4 << 20), max(32 << 20, nbytes)))


def _const_spec(shape):
    nd = len(shape)
    return pl.BlockSpec(shape, lambda *_: (0,) * nd, pipeline_mode=pl.Buffered(1))


def _rms_scale(z, width):
    return lax.rsqrt(jnp.sum(z * z, axis=-1, keepdims=True) * (1.0 / width) + EPS)


def _swap_halves_64(z):
    lane = lax.broadcasted_iota(jnp.int32, z.shape, 1)
    return jnp.where((lane % 64) < 32, pltpu.roll(z, 96, 1), pltpu.roll(z, 32, 1))


def _rope_tables(pos, half):
    freqs = ROPE_THETA ** (-np.arange(half, dtype=np.float64) / half)
    ang = pos.astype(np.float64)[:, None] * freqs[None, :]
    cos, sin = np.cos(ang), np.sin(ang)
    reps = LANES // (2 * half)
    return (np.tile(np.concatenate([cos, cos], -1), (1, reps)).astype(np.float32),
            np.tile(np.concatenate([-sin, sin], -1), (1, reps)).astype(np.float32))


_IN_HEAD = (('qlat', Q_LORA), ('kvl', KV_LORA), ('kpe', LANES))
_IN_TAIL = (('rq', RET_HEADS * RET_DK), ('rk', RET_HEADS * RET_DK), ('rv', RET_HEADS * RET_DV),
            ('rg', RET_HEADS * RET_DV), ('ga', D_MODEL), ('gb', D_MODEL))
_IN_HEAD_SRC_WIDTH = Q_LORA + KV_LORA + QK_ROPE


def _group_offsets(groups):
    offs, o = {}, 0
    for name, width in groups:
        offs[name] = (o, o + width)
        o += width
    return offs


_IN_OFFS = {**{n: (0,) + ab for n, ab in _group_offsets(_IN_HEAD).items()},
            **{n: (1,) + ab for n, ab in _group_offsets(_IN_TAIL).items()}}


def _inproj_kernel(x_ref, gmix_ref, wh_ref, wt_ref, gq_ref, gkv_ref, gkpe_ref,
                   c64_ref, s64_ref, c32_ref, s32_ref,
                   qlat_ref, ckv_ref, kpe_ref, rq_ref, rk_ref, rv_ref, rg_ref, ga_ref, gb_ref):
    xf = x_ref[...]
    h = (xf * _rms_scale(xf, xf.shape[-1]) * gmix_ref[...]).astype(BF16)

    def proj(name):
        part, a, b = _IN_OFFS[name]
        w_ref = (wh_ref, wt_ref)[part]
        return jnp.dot(h, w_ref[:, a:b], preferred_element_type=F32)

    z = proj('qlat')
    qlat_ref[...] = (z * _rms_scale(z, z.shape[-1]) * gq_ref[...]).astype(BF16)
    z = proj('kvl')
    ckv_ref[...] = z * _rms_scale(z, z.shape[-1]) * gkv_ref[...]
    z = proj('kpe')
    zn = z * _rms_scale(z, QK_ROPE) * gkpe_ref[...]
    kpe = zn * c32_ref[...] + _swap_halves_64(zn) * s32_ref[...]
    kpe_ref[...] = kpe[:, :QK_ROPE]
    c64, s64 = c64_ref[...], s64_ref[...]
    for name, ref, scale in (('rq', rq_ref, 1.0), ('rk', rk_ref, RET_DK ** -0.5)):
        z = proj(name)
        for hd in range(RET_HEADS):
            zh = z[:, hd * RET_DK:(hd + 1) * RET_DK]
            r = zh * c64 + pltpu.roll(zh, RET_DK // 2, 1) * s64
            if scale != 1.0:
                r = r * scale
            ref[:, hd * RET_DK:(hd + 1) * RET_DK] = r.astype(BF16)
    rv_ref[...] = proj('rv').astype(BF16)
    rg_ref[...] = proj('rg').astype(BF16)
    ga_ref[...] = proj('ga').astype(BF16)
    gb_ref[...] = proj('gb').astype(BF16)


def _in_proj(x, w_in_parts, gmix, gq, gkv, gkpe, tabs64, tabs32, *, seq_tiles, tm):
    n, d = x.shape
    w_head, w_tail = w_in_parts
    row = lambda w: pl.BlockSpec((tm, w), lambda i: (i, 0))
    tab = pl.BlockSpec((tm, LANES), lambda i: (i % seq_tiles, 0))
    width = dict(_IN_HEAD + _IN_TAIL)
    out_dtypes = dict(qlat=BF16, kvl=F32, kpe=F32, rq=BF16, rk=BF16, rv=BF16, rg=BF16,
                      ga=BF16, gb=BF16)
    out_shapes = tuple(
        jax.ShapeDtypeStruct((n, QK_ROPE if name == 'kpe' else width[name]), dtype)
        for name, dtype in out_dtypes.items())
    return pl.pallas_call(
        _inproj_kernel,
        grid=(n // tm,),
        in_specs=[row(d), _const_spec(gmix.shape), _const_spec(w_head.shape),
                  _const_spec(w_tail.shape), _const_spec(gq.shape), _const_spec(gkv.shape),
                  _const_spec(gkpe.shape), tab, tab, tab, tab],
        out_specs=[row(s.shape[1]) for s in out_shapes],
        out_shape=out_shapes,
        compiler_params=pltpu.CompilerParams(
            dimension_semantics=("arbitrary",), vmem_limit_bytes=_vmem_limit(56 << 20)),
        name="in_proj",
    )(x, gmix, w_head, w_tail, gq, gkv, gkpe, *tabs64, *tabs32)


_QUP_GROUPS = 4


def _qup_kernel(ql_ref, wt_ref, gn_ref, gr_ref, c_ref, s_ref, qt_ref):
    gn, gr, cos, sin = gn_ref[...], gr_ref[...], c_ref[...], s_ref[...]
    half = QK_ROPE // 2
    ql = ql_ref[...]

    group = MLA_HEADS // _QUP_GROUPS

    def project(g):
        return lax.dot_general(wt_ref[g * group * QK_DIM:(g + 1) * group * QK_DIM, :], ql,
                               (((1,), (1,)), ((), ())), preferred_element_type=F32)

    def finish(g, zt):
        for j in range(group):
            hd = g * group + j
            zn = zt[j * QK_DIM:j * QK_DIM + QK_NOPE, :]
            inv = lax.rsqrt(jnp.sum(zn * zn, axis=0, keepdims=True) * (1.0 / QK_NOPE) + EPS)
            qt_ref[hd, :QK_NOPE, :] = (zn * (inv * Q_PRESCALE) * gn).astype(BF16)
            zr = zt[j * QK_DIM + QK_NOPE:(j + 1) * QK_DIM, :]
            inv = lax.rsqrt(jnp.sum(zr * zr, axis=0, keepdims=True) * (1.0 / QK_ROPE) + EPS)
            z = zr * (inv * Q_PRESCALE) * gr
            swapped = jnp.concatenate([z[half:], z[:half]], axis=0)
            qt_ref[hd, QK_NOPE:, :] = (z * cos + swapped * sin).astype(BF16)

    zt = project(0)
    for g in range(_QUP_GROUPS):
        zt_next = project(g + 1) if g + 1 < _QUP_GROUPS else None
        finish(g, zt)
        zt = zt_next


def _q_up(qlat, w_qt, gn, gr, tabs32_t, *, seq_tiles, tm):
    n = qlat.shape[0]
    tab = pl.BlockSpec((QK_ROPE, tm), lambda i: (0, i % seq_tiles))
    return pl.pallas_call(
        _qup_kernel,
        grid=(n // tm,),
        in_specs=[pl.BlockSpec((tm, qlat.shape[1]), lambda i: (i, 0)), _const_spec(w_qt.shape),
                  _const_spec(gn.shape), _const_spec(gr.shape), tab, tab],
        out_specs=pl.BlockSpec((MLA_HEADS, QK_DIM, tm), lambda i: (0, 0, i)),
        out_shape=jax.ShapeDtypeStruct((MLA_HEADS, QK_DIM, n), BF16),
        compiler_params=pltpu.CompilerParams(
            dimension_semantics=("arbitrary",), vmem_limit_bytes=_vmem_limit(40 << 20)),
        name="q_up",
    )(qlat, w_qt, gn, gr, *tabs32_t)


_ONES_ROWS = 16


def _kvup_kernel(c_ref, kpe_ref, w_ref, gk_ref, k_ref, vt_ref):
    c = c_ref[...].astype(BF16)
    kpe = kpe_ref[...].astype(BF16)
    gk = gk_ref[...]
    kw = MLA_HEADS * QK_NOPE
    n_pairs = MLA_HEADS // 2

    def project(pair):
        cols = slice(pair * MXU_TILE, (pair + 1) * MXU_TILE)
        zk = jnp.dot(c, w_ref[:, cols], preferred_element_type=F32)
        zv = jnp.dot(c, w_ref[:, kw + cols.start:kw + cols.stop], preferred_element_type=F32)
        return zk, zv

    def finish(pair, zk, zv):
        for sub in range(2):
            hd = 2 * pair + sub
            zh = zk[:, sub * QK_NOPE:(sub + 1) * QK_NOPE]
            k_ref[hd, :, :QK_NOPE] = (zh * _rms_scale(zh, QK_NOPE) * gk).astype(BF16)
            k_ref[hd, :, QK_NOPE:] = kpe
            vh = zv[:, sub * V_HEAD:(sub + 1) * V_HEAD]
            vt_ref[hd, :V_HEAD, :] = vh.T.astype(BF16)
            vt_ref[hd, V_HEAD:, :] = jnp.ones((_ONES_ROWS, vh.shape[0]), BF16)

    z = project(0)
    for pair in range(n_pairs):
        z_next = project(pair + 1) if pair + 1 < n_pairs else None
        finish(pair, *z)
        z = z_next


def _kv_up(ckv, kpe, w_kv, gk, *, tm):
    n = ckv.shape[0]
    v_spec = pl.BlockSpec((MLA_HEADS, V_HEAD + _ONES_ROWS, tm), lambda i: (0, 0, i))
    v_shape = (MLA_HEADS, V_HEAD + _ONES_ROWS, n)
    return pl.pallas_call(
        _kvup_kernel,
        grid=(n // tm,),
        in_specs=[pl.BlockSpec((tm, ckv.shape[1]), lambda i: (i, 0)),
                  pl.BlockSpec((tm, QK_ROPE), lambda i: (i, 0)),
                  _const_spec(w_kv.shape), _const_spec(gk.shape)],
        out_specs=[pl.BlockSpec((MLA_HEADS, tm, QK_DIM), lambda i: (0, i, 0)), v_spec],
        out_shape=(jax.ShapeDtypeStruct((MLA_HEADS, n, QK_DIM), BF16),
                   jax.ShapeDtypeStruct(v_shape, BF16)),
        compiler_params=pltpu.CompilerParams(
            dimension_semantics=("arbitrary",), vmem_limit_bytes=_vmem_limit(40 << 20)),
        name="kv_up",
    )(ckv, kpe, w_kv, gk)


def _qk(q, k):
    return lax.dot_general(q, k, (((1,), (1,)), ((), ())), preferred_element_type=F32)


_STRIP = MXU_TILE
_ATTN_KEY_BLOCK = 512


def _prompt_attn_kernel(qt_ref, k_ref, vt_ref, o_ref, s_ref, smax_ref, p_ref, a_ref, m_ref,
                        acc_ref, *, tq, tk):
    n_tiles = qt_ref.shape[2] // tq
    n_strips = tq // _STRIP
    n_diag = tq // tk
    assert n_diag % 2 == 0 and tk % _STRIP == 0 and _STRIP % CHUNK == 0
    all_strips = tuple(range(n_strips))
    cols = lambda c: slice(c * _STRIP, (c + 1) * _STRIP)
    n_seen = lambda d, c: max(0, min(tk, (c + 1) * _STRIP - d * tk))
    seen = lambda d: tuple(c for c in all_strips if n_seen(d, c) > 0)
    unmasked = lambda d, c: c * _STRIP >= (d + 1) * tk

    def scores(i, b, slot, c, nk=tk):
        start = pl.multiple_of(b * tk, tk)
        q_start = pl.multiple_of(i * tq + c * _STRIP, _STRIP)
        s = jnp.dot(k_ref[0, pl.ds(start, nk), :], qt_ref[0, :, pl.ds(q_start, _STRIP)],
                    preferred_element_type=F32)
        s_ref[slot, :nk, cols(c)] = s
        smax_ref[slot, :, cols(c)] = jnp.max(s, axis=0, keepdims=True)

    def softmax(slot, c, diag_block=None):
        nk = tk if diag_block is None else n_seen(diag_block, c)
        s = s_ref[slot, :nk, cols(c)]
        if diag_block is None or unmasked(diag_block, c):
            s_max = smax_ref[slot, :, cols(c)]
        else:
            kc = (lax.broadcasted_iota(jnp.int32, s.shape, 0) + diag_block * tk) // CHUNK
            qc = (lax.broadcasted_iota(jnp.int32, s.shape, 1) + c * _STRIP) // CHUNK
            s = jnp.where(kc <= qc, s, NEG_INF)
            s_max = jnp.max(s, axis=0, keepdims=True)
        m_prev = m_ref[:, cols(c)]
        m_new = jnp.maximum(m_prev, s_max)
        alpha = jnp.exp2(m_prev - m_new)
        p = jnp.exp2(s - m_new)
        m_ref[:, cols(c)] = m_new
        a_ref[slot, :, cols(c)] = alpha
        p_ref[slot, :nk, cols(c)] = p.astype(BF16)

    def values(b, slot, c, nk=tk):
        start = pl.multiple_of(jnp.maximum(b, 0) * tk, tk)
        pv = jnp.dot(vt_ref[0, :, pl.ds(start, nk)], p_ref[slot, :nk, cols(c)],
                     preferred_element_type=F32)
        acc_ref[:, cols(c)] = acc_ref[:, cols(c)] * a_ref[slot, :, cols(c)] + pv

    def first_scores(i):
        for c in all_strips:
            scores(i, 0, 0, c)

    def reset_state():
        m_ref[...] = jnp.full(m_ref.shape, NEG_INF, F32)
        acc_ref[...] = jnp.zeros(acc_ref.shape, F32)
        p_ref[1] = jnp.zeros(p_ref.shape[1:], BF16)
        a_ref[1] = jnp.ones(a_ref.shape[1:], F32)

    def tile(i, has_next):
        def pair(jj, carry):
            b = 2 * jj
            for c in all_strips:
                scores(i, b + 1, 1, c)
            for c in all_strips:
                values(b - 1, 1, c)
            for c in all_strips:
                softmax(0, c)
            for c in all_strips:
                scores(i, b + 2, 0, c)
            for c in all_strips:
                values(b, 0, c)
            for c in all_strips:
                softmax(1, c)
            return carry

        lax.fori_loop(0, i * (n_diag // 2), pair, 0)
        b0 = n_diag * i
        order = (0,) + tuple(range(n_diag - 1, 0, -1))
        for t in range(n_diag + 1):
            if 1 <= t + 1 < n_diag:
                d = order[t + 1]
                for c in seen(d):
                    scores(i, b0 + d, (t + 1) % 2, c, n_seen(d, c))
            if t == n_diag - 1 and has_next:
                first_scores(i + 1)
            if t == 0:
                for c in all_strips:
                    values(b0 - 1, 1, c)
            else:
                d = order[t - 1]
                for c in seen(d):
                    values(b0 + d, (t - 1) % 2, c, n_seen(d, c))
            if t < n_diag:
                d = order[t]
                for c in seen(d):
                    softmax(t % 2, c, diag_block=d)
        o = acc_ref[:V_HEAD, :] / acc_ref[V_HEAD:V_HEAD + 1, :]
        o_ref[pl.ds(pl.multiple_of(i * tq, tq), tq), :] = o.T.astype(o_ref.dtype)
        if has_next:
            reset_state()

    def tile_with_next(i, carry):
        tile(i, has_next=True)
        return carry

    first_scores(0)
    reset_state()
    lax.fori_loop(0, n_tiles - 1, tile_with_next, 0)
    tile(jnp.int32(n_tiles - 1), has_next=False)


def _prompt_attention(qt, k, vt, *, batch, seq, tq):
    heads, _, n = qt.shape
    tk = _ATTN_KEY_BLOCK
    return pl.pallas_call(
        functools.partial(_prompt_attn_kernel, tq=tq, tk=tk),
        grid=(batch, heads),
        in_specs=[pl.BlockSpec((1, QK_DIM, seq), lambda b, h: (h, 0, b)),
                  pl.BlockSpec((1, seq, QK_DIM), lambda b, h: (h, b, 0)),
                  pl.BlockSpec((1, vt.shape[1], seq), lambda b, h: (h, 0, b))],
        out_specs=pl.BlockSpec((seq, V_HEAD), lambda b, h: (b, h)),
        out_shape=jax.ShapeDtypeStruct((n, heads * V_HEAD), BF16),
        scratch_shapes=[pltpu.VMEM((2, tk, tq), F32), pltpu.VMEM((2, 1, tq), F32),
                        pltpu.VMEM((2, tk, tq), BF16),
                        pltpu.VMEM((2, 1, tq), F32), pltpu.VMEM((1, tq), F32),
                        pltpu.VMEM((vt.shape[1], tq), F32)],
        compiler_params=pltpu.CompilerParams(
            dimension_semantics=("arbitrary", "arbitrary"),
            vmem_limit_bytes=_vmem_limit(48 << 20)),
        name="prompt_attention",
    )(qt, k, vt)


def _sample_attn_kernel(qbd_ref, qr_ref, latc_ref, kpec_ref, latn_ref, kpen_ref, w_ref, gk_ref,
                        o_ref, *, new):
    gk = gk_ref[...]
    kw = MLA_HEADS * QK_NOPE

    def scores(lat_ref, kpe_ref):
        c = lat_ref[0].astype(BF16)
        k_heads = []
        for pair in range(MLA_HEADS // 2):
            zk = jnp.dot(c, w_ref[:, pair * MXU_TILE:(pair + 1) * MXU_TILE],
                         preferred_element_type=F32)
            for sub in range(2):
                zh = zk[:, sub * QK_NOPE:(sub + 1) * QK_NOPE]
                k_heads.append((zh * _rms_scale(zh, QK_NOPE) * gk).astype(BF16))
        k_all = jnp.concatenate(k_heads, axis=1)
        s = (jnp.dot(k_all, qbd_ref[0], preferred_element_type=F32)
             + jnp.dot(kpe_ref[0].astype(BF16), qr_ref[0], preferred_element_type=F32))
        return s, c

    s_c, c_c = scores(latc_ref, kpec_ref)
    s_n, c_n = scores(latn_ref, kpen_ref)
    m = jnp.maximum(jnp.max(s_c, axis=0, keepdims=True), jnp.max(s_n, axis=0, keepdims=True))
    p_c = jnp.exp2(s_c - m).astype(BF16)
    p_n = jnp.exp2(s_n - m).astype(BF16)
    tn = (((0,), (0,)), ((), ()))
    ctx = (lax.dot_general(p_c, c_c, tn, preferred_element_type=F32)
           + lax.dot_general(p_n, c_n, tn, preferred_element_type=F32))
    den = (lax.dot_general(p_c, jnp.ones((p_c.shape[0], V_HEAD), BF16), tn,
                           preferred_element_type=F32)
           + lax.dot_general(p_n, jnp.ones((p_n.shape[0], V_HEAD), BF16), tn,
                             preferred_element_type=F32))
    ctx = ctx.astype(BF16)
    for hd in range(MLA_HEADS):
        rows = slice(hd * new, (hd + 1) * new)
        o = jnp.dot(ctx[rows, :], w_ref[:, kw + hd * V_HEAD:kw + (hd + 1) * V_HEAD],
                    preferred_element_type=F32)
        o_ref[0, :, hd * V_HEAD:(hd + 1) * V_HEAD] = (o / den[rows, :]).astype(o_ref.dtype)


def _sample_attention(qt, lat_c, kpe_c, lat_n, kpe_n, w_kv, gk, *, batch, new):
    heads = qt.shape[0]
    q4 = qt.reshape(heads, QK_DIM, batch, new)
    qn = jnp.transpose(q4[:, :QK_NOPE], (2, 0, 1, 3))
    qbd = jnp.einsum('bhdj,hg->bhdgj', qn, jnp.eye(heads, dtype=qn.dtype))
    qbd = qbd.reshape(batch, heads * QK_NOPE, heads * new)
    qr = jnp.transpose(q4[:, QK_NOPE:], (2, 1, 0, 3)).reshape(batch, QK_ROPE, heads * new)
    per_batch = lambda a: pl.BlockSpec((1,) + a.shape[1:], lambda b: (b, 0, 0))
    return pl.pallas_call(
        functools.partial(_sample_attn_kernel, new=new),
        grid=(batch,),
        in_specs=[per_batch(qbd), per_batch(qr), per_batch(lat_c), per_batch(kpe_c),
                  per_batch(lat_n), per_batch(kpe_n), _const_spec(w_kv.shape),
                  _const_spec(gk.shape)],
        out_specs=pl.BlockSpec((1, new, heads * V_HEAD), lambda b: (b, 0, 0)),
        out_shape=jax.ShapeDtypeStruct((batch, new, heads * V_HEAD), BF16),
        compiler_params=pltpu.CompilerParams(
            dimension_semantics=("arbitrary",), vmem_limit_bytes=_vmem_limit(48 << 20)),
        name="sample_attention",
    )(qbd, qr, lat_c, kpe_c, lat_n, kpe_n, w_kv, gk)


_RET_SUB_CHUNKS = 4


def _retention_kernel(q_ref, k_ref, v_ref, rg_ref, dec_ref, qd_ref, kd_ref, g_ref, s0_ref,
                      o_ref, sfin_ref, state_ref, *, chunk, n_sub, n_steps):
    c = pl.program_id(1)

    @pl.when(c == 0)
    def _():
        state_ref[...] = s0_ref[0]

    heads = range(RET_HEADS)
    kcols = [slice(hd * RET_DK, (hd + 1) * RET_DK) for hd in heads]
    vcols = [slice(hd * RET_DV, (hd + 1) * RET_DV) for hd in heads]
    qd = [qd_ref[hd] for hd in heads]

    def recur(j, state):
        rows = slice(j * chunk, (j + 1) * chunk)
        q = [q_ref[rows, kcols[hd]] for hd in heads]
        k = [k_ref[rows, kcols[hd]] for hd in heads]
        v = [v_ref[rows, vcols[hd]] for hd in heads]
        s = [_qk(q[hd], k[hd]) for hd in heads]
        cross = [jnp.dot(q[hd], state[hd].astype(BF16), preferred_element_type=F32)
                 for hd in heads]
        s = [(s[hd] * dec_ref[hd]).astype(BF16) for hd in heads]
        k_dec = [(k[hd].astype(F32) * kd_ref[hd]).astype(BF16) for hd in heads]
        inner = [jnp.dot(s[hd], v[hd], preferred_element_type=F32) for hd in heads]
        kv = [lax.dot_general(k_dec[hd], v[hd], (((0,), (0,)), ((), ())),
                              preferred_element_type=F32) for hd in heads]
        new_state = [state[hd] * qd[hd][chunk - 1:chunk, :] + kv[hd] for hd in heads]
        return [inner[hd] + cross[hd] * qd[hd] for hd in heads], new_state

    def finish(j, o):
        rows = slice(j * chunk, (j + 1) * chunk)
        for hd in heads:
            mu = jnp.mean(o[hd], axis=-1, keepdims=True)
            oc = o[hd] - mu
            var = jnp.mean(oc * oc, axis=-1, keepdims=True)
            y = oc * lax.rsqrt(var + EPS) * g_ref[hd]
            rg = rg_ref[rows, vcols[hd]].astype(F32)
            o_ref[rows, vcols[hd]] = (rg * jax.nn.sigmoid(rg) * y).astype(o_ref.dtype)

    state = [state_ref[hd] for hd in heads]
    o_prev = None
    for j in range(n_sub):
        o, state = recur(j, state)
        if o_prev is not None:
            finish(j - 1, o_prev)
        o_prev = o
    finish(n_sub - 1, o_prev)
    for hd in heads:
        state_ref[hd] = state[hd]

    @pl.when(c == n_steps - 1)
    def _():
        sfin_ref[0] = state_ref[...]


def _retention_tables(chunk):
    lg = jnp.log1p(-(2.0 ** (-5.0 - jnp.arange(RET_HEADS, dtype=F32))))
    idx = jnp.arange(chunk, dtype=F32)
    diff = idx[:, None] - idx[None, :]
    decay = jnp.where(diff[None] >= 0,
                      jnp.exp(jnp.maximum(diff, 0.0)[None] * lg[:, None, None]), 0.0)
    q_dec = jnp.exp((idx + 1.0)[None, :] * lg[:, None])[..., None]
    k_dec = jnp.exp((chunk - 1.0 - idx)[None, :] * lg[:, None])[..., None]
    return decay, q_dec, k_dec


def _retention(rq, rk, rv, rg, g_ret, s0, *, batch, seq, chunk):
    n = rq.shape[0]
    n_sub = min(_RET_SUB_CHUNKS, seq // chunk)
    rows = n_sub * chunk
    ns = seq // rows
    decay, q_dec, k_dec = _retention_tables(chunk)
    tok = lambda w: pl.BlockSpec((rows, w), lambda b, c: (b * ns + c, 0))
    state_spec = pl.BlockSpec((1, RET_HEADS, RET_DK, RET_DV), lambda b, c: (b, 0, 0, 0))
    g_ret = g_ret.reshape(RET_HEADS, 1, RET_DV)
    return pl.pallas_call(
        functools.partial(_retention_kernel, chunk=chunk, n_sub=n_sub, n_steps=ns),
        grid=(batch, ns),
        in_specs=[tok(RET_HEADS * RET_DK), tok(RET_HEADS * RET_DK), tok(RET_HEADS * RET_DV),
                  tok(RET_HEADS * RET_DV), _const_spec(decay.shape), _const_spec(q_dec.shape),
                  _const_spec(k_dec.shape), _const_spec(g_ret.shape), state_spec],
        out_specs=[tok(RET_HEADS * RET_DV), state_spec],
        out_shape=(jax.ShapeDtypeStruct((n, RET_HEADS * RET_DV), BF16),
                   jax.ShapeDtypeStruct((batch, RET_HEADS, RET_DK, RET_DV), F32)),
        scratch_shapes=[pltpu.VMEM((RET_HEADS, RET_DK, RET_DV), F32)],
        compiler_params=pltpu.CompilerParams(dimension_semantics=("arbitrary", "arbitrary")),
        name="retention",
    )(rq, rk, rv, rg, decay, q_dec, k_dec, g_ret, s0)


def _merge_kernel(x_ref, a_ref, r_ref, ga_ref, gb_ref, wa_ref, wb_ref, wo_ref, o_ref):
    a_d = jnp.dot(a_ref[...], wa_ref[...], preferred_element_type=F32)
    r_d = jnp.dot(r_ref[...], wb_ref[...], preferred_element_type=F32)
    merged = (jax.nn.sigmoid(ga_ref[...].astype(F32)) * a_d
              + jax.nn.sigmoid(gb_ref[...].astype(F32)) * r_d)
    o_ref[...] = x_ref[...] + jnp.dot(merged.astype(BF16), wo_ref[...],
                                      preferred_element_type=F32)


def _merge(x, attn, ret, ga, gb, wa, wb, wo, *, tm):
    n, d = x.shape
    row = pl.BlockSpec((tm, d), lambda i: (i, 0))
    return pl.pallas_call(
        _merge_kernel,
        grid=(n // tm,),
        in_specs=[row, row, row, row, row,
                  _const_spec(wa.shape), _const_spec(wb.shape), _const_spec(wo.shape)],
        out_specs=row,
        out_shape=jax.ShapeDtypeStruct((n, d), F32),
        compiler_params=pltpu.CompilerParams(
            dimension_semantics=("arbitrary",), vmem_limit_bytes=_vmem_limit(40 << 20)),
        name="merge",
    )(x, attn, ret, ga, gb, wa, wb, wo)


_PAD_ROWS = 8


def _ffn_kernel(x_ref, g_ref, wup_ref, cw_ref, cb_ref, wdn_ref, prev_ref,
                y_ref, st_ref, ubuf_ref, *, rows, n_seqs, d_ff, n_tiles):
    j = pl.program_id(1)
    lo = _PAD_ROWS - (CONV_W - 1)
    stride = _PAD_ROWS + rows
    assert n_seqs == 1 or n_tiles == 1

    @pl.when(j == 0)
    def _():
        for b in range(n_seqs):
            ubuf_ref[b * stride + lo:b * stride + _PAD_ROWS, :] = prev_ref[b]

    if n_tiles > 1:
        @pl.when(j > 0)
        def _():
            ubuf_ref[0:_PAD_ROWS, :] = ubuf_ref[rows:rows + _PAD_ROWS, :]

    xf = x_ref[...]
    hn = (xf * _rms_scale(xf, xf.shape[-1]) * g_ref[...]).astype(BF16)
    u = jnp.dot(hn, wup_ref[...], preferred_element_type=F32)
    for b in range(n_seqs):
        ubuf_ref[b * stride + _PAD_ROWS:(b + 1) * stride, :] = u[b * rows:(b + 1) * rows, :]
    runs = []
    for b in range(n_seqs):
        c = cb_ref[...]
        for tap in range(CONV_W):
            first = b * stride + lo + tap
            c = c + ubuf_ref[first:first + rows, :] * cw_ref[tap:tap + 1, :]
        runs.append(c)
    c = runs[0] if n_seqs == 1 else jnp.concatenate(runs, axis=0)
    gate, val = c[:, :d_ff], c[:, d_ff:]
    act = (gate * jax.nn.sigmoid(gate) * val).astype(BF16)
    y_ref[...] = xf + jnp.dot(act, wdn_ref[...], preferred_element_type=F32)

    @pl.when(j == n_tiles - 1)
    def _():
        for b in range(n_seqs):
            st_ref[b] = ubuf_ref[(b + 1) * stride - (CONV_W - 1):(b + 1) * stride, :]


def _conv_ffn(x, g, w_up, cw, cb, w_dn, prev, *, batch, seq, tm):
    n, d = x.shape
    d_ff = w_dn.shape[0]
    n_seqs = max(tm // seq, 1)
    rows = tm // n_seqs
    nt = seq // rows
    row = pl.BlockSpec((tm, d), lambda b, j: (b * nt + j, 0))
    st = pl.BlockSpec((n_seqs, CONV_W - 1, 2 * d_ff), lambda b, j: (b, 0, 0))
    return pl.pallas_call(
        functools.partial(_ffn_kernel, rows=rows, n_seqs=n_seqs, d_ff=d_ff, n_tiles=nt),
        grid=(batch // n_seqs, nt),
        in_specs=[row, _const_spec(g.shape), _const_spec(w_up.shape), _const_spec(cw.shape),
                  _const_spec(cb.shape), _const_spec(w_dn.shape), st],
        out_specs=[row, st],
        out_shape=(jax.ShapeDtypeStruct((n, d), F32),
                   jax.ShapeDtypeStruct((batch, CONV_W - 1, 2 * d_ff), F32)),
        scratch_shapes=[pltpu.VMEM((n_seqs * (_PAD_ROWS + rows), 2 * d_ff), F32)],
        compiler_params=pltpu.CompilerParams(
            dimension_semantics=("arbitrary", "arbitrary"),
            vmem_limit_bytes=_vmem_limit(56 << 20)),
        name="conv_ffn",
    )(x, g, w_up, cw, cb, w_dn, prev)


def _pack_weights(w_in, w_q_up, w_kv_up, w_o_branch, w_out, w_ffn_up, w_ffn_down):
    head_width = sum(width for _, width in _IN_HEAD)
    w_head = jnp.pad(w_in[:, :_IN_HEAD_SRC_WIDTH].astype(BF16),
                     ((0, 0), (0, head_width - _IN_HEAD_SRC_WIDTH)))
    w_in_p = (w_head, w_in[:, _IN_HEAD_SRC_WIDTH:].astype(BF16))
    w_q_p = w_q_up.T.astype(BF16)
    wkv = w_kv_up.reshape(w_kv_up.shape[0], MLA_HEADS, QK_NOPE + V_HEAD)
    w_kv_p = jnp.concatenate([wkv[:, :, :QK_NOPE].reshape(wkv.shape[0], -1),
                              wkv[:, :, QK_NOPE:].reshape(wkv.shape[0], -1)], axis=1).astype(BF16)
    mla_width = MLA_HEADS * V_HEAD
    return (w_in_p, w_q_p, w_kv_p, w_o_branch[:mla_width].astype(BF16),
            w_o_branch[mla_width:].astype(BF16), w_out.astype(BF16),
            w_ffn_up.astype(BF16), w_ffn_down.astype(BF16))


class _Tiles(NamedTuple):
    tm: int
    up_tm: int
    tq: int
    ret_chunk: int
    ffn_tm: int


_TOKEN_TILE = 512


def _tiles(batch, seq):
    if seq >= _TOKEN_TILE:
        return _Tiles(tm=_TOKEN_TILE, up_tm=2 * _TOKEN_TILE, tq=4 * _ATTN_KEY_BLOCK,
                      ret_chunk=MXU_TILE, ffn_tm=_TOKEN_TILE)
    return _Tiles(tm=batch * seq, up_tm=batch * seq, tq=seq, ret_chunk=seq,
                  ffn_tm=batch * seq)


def _layer(x, pos, past, w, *, batch, seq):
    n = x.shape[0]
    tm, up_tm, tq, ret_chunk, ffn_tm = _tiles(batch, seq)
    tabs64 = _rope_tables(pos, RET_DK // 2)
    tabs32 = _rope_tables(pos, QK_ROPE // 2)
    seq_tiles = max(seq // tm, 1)
    if seq < tm:
        tabs64 = tuple(np.tile(t, (tm // seq, 1)) for t in tabs64)
        tabs32 = tuple(np.tile(t, (tm // seq, 1)) for t in tabs32)
    qlat, ckv, kpe, rq, rk, rv, rg, ga, gb = _in_proj(
        x, w['w_in'], w['g_norm_mix'], w['g_q_lat'], w['g_kv_lat'], w['g_k_rope'],
        tabs64, tabs32, seq_tiles=seq_tiles, tm=tm)
    tabs32_t = tuple(np.ascontiguousarray(t[:, :QK_ROPE].T) for t in tabs32)
    bcast = lambda g: jnp.broadcast_to(g.reshape(-1, 1), (g.size, up_tm))
    qt = _q_up(qlat, w['w_q_up'], bcast(w['g_q_nope']), bcast(w['g_q_rope']), tabs32_t,
               seq_tiles=max(seq // up_tm, 1), tm=up_tm)
    if past is None:
        k_new, vt_new = _kv_up(ckv, kpe, w['w_kv_up'], w['g_k_nope'], tm=up_tm)
        attn = _prompt_attention(qt, k_new, vt_new, batch=batch, seq=seq, tq=tq)
        s0 = jnp.zeros((batch, RET_HEADS, RET_DK, RET_DV), F32)
        prev = jnp.zeros((batch, CONV_W - 1, w['w_ffn_up'].shape[1]), F32)
    else:
        lat_c, pe_c, s0, prev = past
        attn = _sample_attention(qt, lat_c, pe_c, ckv.reshape(batch, seq, -1),
                                 kpe.reshape(batch, seq, -1), w['w_kv_up'], w['g_k_nope'],
                                 batch=batch, new=seq).reshape(n, -1)
    ret, s_fin = _retention(rq, rk, rv, rg, w['g_ret_out'], s0,
                            batch=batch, seq=seq, chunk=ret_chunk)
    x1 = _merge(x, attn, ret, ga, gb, w['w_o_a'], w['w_o_b'], w['w_out'], tm=up_tm)
    y, conv_state = _conv_ffn(x1, w['g_norm_ffn'], w['w_ffn_up'], w['ffn_conv_w'],
                              w['ffn_conv_b'], w['w_ffn_down'], prev,
                              batch=batch, seq=seq, tm=ffn_tm)
    return y, (ckv, kpe, s_fin, conv_state)


def kernel(x_prompt, x_sample, cache_mla_latent, cache_mla_rope_key, state_retention, state_ffn_conv, g_norm_mix, w_in, g_q_lat, w_q_up, g_q_nope, g_q_rope, g_kv_lat, w_kv_up, g_k_nope, g_k_rope, g_ret_out, w_o_branch, w_out, g_norm_ffn, w_ffn_up, ffn_conv_w, ffn_conv_b, w_ffn_down):
    depth = w_in.shape[0]
    assert depth == 1, "single-layer trunk"
    bp, tp, d = x_prompt.shape
    bs, ts, _ = x_sample.shape
    past_len = cache_mla_latent.shape[2]
    (w_in_p, w_q_p, w_kv_p, w_o_a, w_o_b, w_out_p, w_up_p, w_dn_p) = _pack_weights(
        w_in[0], w_q_up[0], w_kv_up[0], w_o_branch[0], w_out[0], w_ffn_up[0], w_ffn_down[0])
    row = lambda g: g.reshape(1, -1).astype(F32)
    w = dict(
        w_in=w_in_p, w_q_up=w_q_p, w_kv_up=w_kv_p, w_o_a=w_o_a, w_o_b=w_o_b, w_out=w_out_p,
        w_ffn_up=w_up_p, w_ffn_down=w_dn_p,
        g_norm_mix=row(g_norm_mix[0]), g_q_lat=row(g_q_lat[0]), g_kv_lat=row(g_kv_lat[0]),
        g_k_rope=jnp.pad(row(g_k_rope[0]), ((0, 0), (0, LANES - QK_ROPE))),
        g_q_nope=g_q_nope[0].astype(F32), g_q_rope=g_q_rope[0].astype(F32),
        g_k_nope=row(g_k_nope[0]), g_ret_out=g_ret_out[0], g_norm_ffn=row(g_norm_ffn[0]),
        ffn_conv_w=ffn_conv_w[0], ffn_conv_b=row(ffn_conv_b[0]))

    yp, sp = _layer(x_prompt.reshape(bp * tp, d), np.arange(tp), None, w, batch=bp, seq=tp)
    past = (cache_mla_latent[0], cache_mla_rope_key[0], state_retention[0], state_ffn_conv[0])
    ys, ss = _layer(x_sample.reshape(bs * ts, d), past_len + np.arange(ts), past, w,
                    batch=bs, seq=ts)

    def states(st, b, t):
        ckv, kpe, s_fin, conv = st
        return (ckv.reshape(1, b, t, -1), kpe.reshape(1, b, t, -1), s_fin[None], conv[None])

    return (yp.reshape(bp, tp, d), ys.reshape(bs, ts, d)) + states(sp, bp, tp) + states(ss, bs, ts)
```

```python
import functools
import math
from typing import NamedTuple

import jax
import jax.numpy as jnp
import numpy as np
from jax import lax
from jax.experimental import pallas as pl
from jax.experimental.pallas import tpu as pltpu

F32 = jnp.float32
BF16 = jnp.bfloat16

EPS = 1e-6
NEG_INF = -1e30
ROPE_THETA = 10000.0
CHUNK = 64
LANES = 128
MXU_TILE = 256
V7X_VMEM_BYTES = 64 * 1024 * 1024

D_MODEL = 1024
Q_LORA = 768
KV_LORA = 512
MLA_HEADS = 8
QK_NOPE = 128
QK_ROPE = 64
QK_DIM = QK_NOPE + QK_ROPE
V_HEAD = 128
MLA_SCALE = QK_DIM ** -0.5
Q_PRESCALE = MLA_SCALE * math.log2(math.e)
RET_HEADS = 4
RET_DK = 128
RET_DV = 256
CONV_W = 3


def _vmem_limit(nbytes):
    return int(min(V7X_VMEM_BYTES - (4 << 20), max(32 << 20, nbytes)))


def _const_spec(shape):
    nd = len(shape)
    return pl.BlockSpec(shape, lambda *_: (0,) * nd, pipeline_mode=pl.Buffered(1))


def _rms_scale(z, width):
    return lax.rsqrt(jnp.sum(z * z, axis=-1, keepdims=True) * (1.0 / width) + EPS)


def _swap_halves_64(z):
    lane = lax.broadcasted_iota(jnp.int32, z.shape, 1)
    return jnp.where((lane % 64) < 32, pltpu.roll(z, 96, 1), pltpu.roll(z, 32, 1))


def _rope_tables(pos, half):
    freqs = ROPE_THETA ** (-np.arange(half, dtype=np.float64) / half)
    ang = pos.astype(np.float64)[:, None] * freqs[None, :]
    cos, sin = np.cos(ang), np.sin(ang)
    reps = LANES // (2 * half)
    return (np.tile(np.concatenate([cos, cos], -1), (1, reps)).astype(np.float32),
            np.tile(np.concatenate([-sin, sin], -1), (1, reps)).astype(np.float32))


_IN_HEAD = (('qlat', Q_LORA), ('kvl', KV_LORA), ('kpe', LANES))
_IN_TAIL = (('rq', RET_HEADS * RET_DK), ('rk', RET_HEADS * RET_DK), ('rv', RET_HEADS * RET_DV),
            ('rg', RET_HEADS * RET_DV), ('ga', D_MODEL), ('gb', D_MODEL))
_IN_HEAD_SRC_WIDTH = Q_LORA + KV_LORA + QK_ROPE


def _group_offsets(groups):
    offs, o = {}, 0
    for name, width in groups:
        offs[name] = (o, o + width)
        o += width
    return offs


_IN_OFFS = {**{n: (0,) + ab for n, ab in _group_offsets(_IN_HEAD).items()},
            **{n: (1,) + ab for n, ab in _group_offsets(_IN_TAIL).items()}}


def _inproj_kernel(x_ref, gmix_ref, wh_ref, wt_ref, gq_ref, gkv_ref, gkpe_ref,
                   c64_ref, s64_ref, c32_ref, s32_ref,
                   qlat_ref, ckv_ref, kpe_ref, rq_ref, rk_ref, rv_ref, rg_ref, ga_ref, gb_ref):
    xf = x_ref[...]
    h = (xf * _rms_scale(xf, xf.shape[-1]) * gmix_ref[...]).astype(BF16)

    def proj(name):
        part, a, b = _IN_OFFS[name]
        w_ref = (wh_ref, wt_ref)[part]
        return jnp.dot(h, w_ref[:, a:b], preferred_element_type=F32)

    z = proj('qlat')
    qlat_ref[...] = (z * _rms_scale(z, z.shape[-1]) * gq_ref[...]).astype(BF16)
    z = proj('kvl')
    ckv_ref[...] = z * _rms_scale(z, z.shape[-1]) * gkv_ref[...]
    z = proj('kpe')
    zn = z * _rms_scale(z, QK_ROPE) * gkpe_ref[...]
    kpe = zn * c32_ref[...] + _swap_halves_64(zn) * s32_ref[...]
    kpe_ref[...] = kpe[:, :QK_ROPE]
    c64, s64 = c64_ref[...], s64_ref[...]
    for name, ref, scale in (('rq', rq_ref, 1.0), ('rk', rk_ref, RET_DK ** -0.5)):
        z = proj(name)
        for hd in range(RET_HEADS):
            zh = z[:, hd * RET_DK:(hd + 1) * RET_DK]
            r = zh * c64 + pltpu.roll(zh, RET_DK // 2, 1) * s64
            if scale != 1.0:
                r = r * scale
            ref[:, hd * RET_DK:(hd + 1) * RET_DK] = r.astype(BF16)
    rv_ref[...] = proj('rv').astype(BF16)
    rg_ref[...] = proj('rg').astype(BF16)
    ga_ref[...] = proj('ga').astype(BF16)
    gb_ref[...] = proj('gb').astype(BF16)


def _in_proj(x, w_in_parts, gmix, gq, gkv, gkpe, tabs64, tabs32, *, seq_tiles, tm):
    n, d = x.shape
    w_head, w_tail = w_in_parts
    row = lambda w: pl.BlockSpec((tm, w), lambda i: (i, 0))
    tab = pl.BlockSpec((tm, LANES), lambda i: (i % seq_tiles, 0))
    width = dict(_IN_HEAD + _IN_TAIL)
    out_dtypes = dict(qlat=BF16, kvl=F32, kpe=F32, rq=BF16, rk=BF16, rv=BF16, rg=BF16,
                      ga=BF16, gb=BF16)
    out_shapes = tuple(
        jax.ShapeDtypeStruct((n, QK_ROPE if name == 'kpe' else width[name]), dtype)
        for name, dtype in out_dtypes.items())
    return pl.pallas_call(
        _inproj_kernel,
        grid=(n // tm,),
        in_specs=[row(d), _const_spec(gmix.shape), _const_spec(w_head.shape),
                  _const_spec(w_tail.shape), _const_spec(gq.shape), _const_spec(gkv.shape),
                  _const_spec(gkpe.shape), tab, tab, tab, tab],
        out_specs=[row(s.shape[1]) for s in out_shapes],
        out_shape=out_shapes,
        compiler_params=pltpu.CompilerParams(
            dimension_semantics=("arbitrary",), vmem_limit_bytes=_vmem_limit(56 << 20)),
        name="in_proj",
    )(x, gmix, w_head, w_tail, gq, gkv, gkpe, *tabs64, *tabs32)


_QUP_GROUPS = 4


def _qup_kernel(ql_ref, wt_ref, gn_ref, gr_ref, c_ref, s_ref, qt_ref):
    gn, gr, cos, sin = gn_ref[...], gr_ref[...], c_ref[...], s_ref[...]
    half = QK_ROPE // 2
    ql = ql_ref[...]

    group = MLA_HEADS // _QUP_GROUPS

    def project(g):
        return lax.dot_general(wt_ref[g * group * QK_DIM:(g + 1) * group * QK_DIM, :], ql,
                               (((1,), (1,)), ((), ())), preferred_element_type=F32)

    def finish(g, zt):
        for j in range(group):
            hd = g * group + j
            zn = zt[j * QK_DIM:j * QK_DIM + QK_NOPE, :]
            inv = lax.rsqrt(jnp.sum(zn * zn, axis=0, keepdims=True) * (1.0 / QK_NOPE) + EPS)
            qt_ref[hd, :QK_NOPE, :] = (zn * (inv * Q_PRESCALE) * gn).astype(BF16)
            zr = zt[j * QK_DIM + QK_NOPE:(j + 1) * QK_DIM, :]
            inv = lax.rsqrt(jnp.sum(zr * zr, axis=0, keepdims=True) * (1.0 / QK_ROPE) + EPS)
            z = zr * (inv * Q_PRESCALE) * gr
            swapped = jnp.concatenate([z[half:], z[:half]], axis=0)
            qt_ref[hd, QK_NOPE:, :] = (z * cos + swapped * sin).astype(BF16)

    zt = project(0)
    for g in range(_QUP_GROUPS):
        zt_next = project(g + 1) if g + 1 < _QUP_GROUPS else None
        finish(g, zt)
        zt = zt_next


def _q_up(qlat, w_qt, gn, gr, tabs32_t, *, seq_tiles, tm):
    n = qlat.shape[0]
    tab = pl.BlockSpec((QK_ROPE, tm), lambda i: (0, i % seq_tiles))
    return pl.pallas_call(
        _qup_kernel,
        grid=(n // tm,),
        in_specs=[pl.BlockSpec((tm, qlat.shape[1]), lambda i: (i, 0)), _const_spec(w_qt.shape),
                  _const_spec(gn.shape), _const_spec(gr.shape), tab, tab],
        out_specs=pl.BlockSpec((MLA_HEADS, QK_DIM, tm), lambda i: (0, 0, i)),
        out_shape=jax.ShapeDtypeStruct((MLA_HEADS, QK_DIM, n), BF16),
        compiler_params=pltpu.CompilerParams(
            dimension_semantics=("arbitrary",), vmem_limit_bytes=_vmem_limit(40 << 20)),
        name="q_up",
    )(qlat, w_qt, gn, gr, *tabs32_t)


_ONES_ROWS = 16


def _kvup_kernel(c_ref, kpe_ref, w_ref, gk_ref, k_ref, vt_ref):
    c = c_ref[...].astype(BF16)
    kpe = kpe_ref[...].astype(BF16)
    gk = gk_ref[...]
    kw = MLA_HEADS * QK_NOPE
    n_pairs = MLA_HEADS // 2

    def project(pair):
        cols = slice(pair * MXU_TILE, (pair + 1) * MXU_TILE)
        zk = jnp.dot(c, w_ref[:, cols], preferred_element_type=F32)
        zv = jnp.dot(c, w_ref[:, kw + cols.start:kw + cols.stop], preferred_element_type=F32)
        return zk, zv

    def finish(pair, zk, zv):
        for sub in range(2):
            hd = 2 * pair + sub
            zh = zk[:, sub * QK_NOPE:(sub + 1) * QK_NOPE]
            k_ref[hd, :, :QK_NOPE] = (zh * _rms_scale(zh, QK_NOPE) * gk).astype(BF16)
            k_ref[hd, :, QK_NOPE:] = kpe
            vh = zv[:, sub * V_HEAD:(sub + 1) * V_HEAD]
            vt_ref[hd, :V_HEAD, :] = vh.T.astype(BF16)
            vt_ref[hd, V_HEAD:, :] = jnp.ones((_ONES_ROWS, vh.shape[0]), BF16)

    z = project(0)
    for pair in range(n_pairs):
        z_next = project(pair + 1) if pair + 1 < n_pairs else None
        finish(pair, *z)
        z = z_next


def _kv_up(ckv, kpe, w_kv, gk, *, tm):
    n = ckv.shape[0]
    v_spec = pl.BlockSpec((MLA_HEADS, V_HEAD + _ONES_ROWS, tm), lambda i: (0, 0, i))
    v_shape = (MLA_HEADS, V_HEAD + _ONES_ROWS, n)
    return pl.pallas_call(
        _kvup_kernel,
        grid=(n // tm,),
        in_specs=[pl.BlockSpec((tm, ckv.shape[1]), lambda i: (i, 0)),
                  pl.BlockSpec((tm, QK_ROPE), lambda i: (i, 0)),
                  _const_spec(w_kv.shape), _const_spec(gk.shape)],
        out_specs=[pl.BlockSpec((MLA_HEADS, tm, QK_DIM), lambda i: (0, i, 0)), v_spec],
        out_shape=(jax.ShapeDtypeStruct((MLA_HEADS, n, QK_DIM), BF16),
                   jax.ShapeDtypeStruct(v_shape, BF16)),
        compiler_params=pltpu.CompilerParams(
            dimension_semantics=("arbitrary",), vmem_limit_bytes=_vmem_limit(40 << 20)),
        name="kv_up",
    )(ckv, kpe, w_kv, gk)


def _qk(q, k):
    return lax.dot_general(q, k, (((1,), (1,)), ((), ())), preferred_element_type=F32)


_STRIP = MXU_TILE
_ATTN_KEY_BLOCK = 512


def _prompt_attn_kernel(qt_ref, k_ref, vt_ref, o_ref, s_ref, smax_ref, p_ref, a_ref, m_ref,
                        acc_ref, *, tq, tk):
    n_tiles = qt_ref.shape[2] // tq
    n_strips = tq // _STRIP
    n_diag = tq // tk
    assert n_diag % 2 == 0 and tk % _STRIP == 0 and _STRIP % CHUNK == 0
    all_strips = tuple(range(n_strips))
    cols = lambda c: slice(c * _STRIP, (c + 1) * _STRIP)
    n_seen = lambda d, c: max(0, min(tk, (c + 1) * _STRIP - d * tk))
    seen = lambda d: tuple(c for c in all_strips if n_seen(d, c) > 0)
    unmasked = lambda d, c: c * _STRIP >= (d + 1) * tk

    def scores(i, b, slot, c, nk=tk):
        start = pl.multiple_of(b * tk, tk)
        q_start = pl.multiple_of(i * tq + c * _STRIP, _STRIP)
        s = jnp.dot(k_ref[0, pl.ds(start, nk), :], qt_ref[0, :, pl.ds(q_start, _STRIP)],
                    preferred_element_type=F32)
        s_ref[slot, :nk, cols(c)] = s
        smax_ref[slot, :, cols(c)] = jnp.max(s, axis=0, keepdims=True)

    def softmax(slot, c, diag_block=None):
        nk = tk if diag_block is None else n_seen(diag_block, c)
        s = s_ref[slot, :nk, cols(c)]
        if diag_block is None or unmasked(diag_block, c):
            s_max = smax_ref[slot, :, cols(c)]
        else:
            kc = (lax.broadcasted_iota(jnp.int32, s.shape, 0) + diag_block * tk) // CHUNK
            qc = (lax.broadcasted_iota(jnp.int32, s.shape, 1) + c * _STRIP) // CHUNK
            s = jnp.where(kc <= qc, s, NEG_INF)
            s_max = jnp.max(s, axis=0, keepdims=True)
        m_prev = m_ref[:, cols(c)]
        m_new = jnp.maximum(m_prev, s_max)
        alpha = jnp.exp2(m_prev - m_new)
        p = jnp.exp2(s - m_new)
        m_ref[:, cols(c)] = m_new
        a_ref[slot, :, cols(c)] = alpha
        p_ref[slot, :nk, cols(c)] = p.astype(BF16)

    def values(b, slot, c, nk=tk):
        start = pl.multiple_of(jnp.maximum(b, 0) * tk, tk)
        pv = jnp.dot(vt_ref[0, :, pl.ds(start, nk)], p_ref[slot, :nk, cols(c)],
                     preferred_element_type=F32)
        acc_ref[:, cols(c)] = acc_ref[:, cols(c)] * a_ref[slot, :, cols(c)] + pv

    def first_scores(i):
        for c in all_strips:
            scores(i, 0, 0, c)

    def reset_state():
        m_ref[...] = jnp.full(m_ref.shape, NEG_INF, F32)
        acc_ref[...] = jnp.zeros(acc_ref.shape, F32)
        p_ref[1] = jnp.zeros(p_ref.shape[1:], BF16)
        a_ref[1] = jnp.ones(a_ref.shape[1:], F32)

    def tile(i, has_next):
        def pair(jj, carry):
            b = 2 * jj
            for c in all_strips:
                scores(i, b + 1, 1, c)
            for c in all_strips:
                values(b - 1, 1, c)
            for c in all_strips:
                softmax(0, c)
            for c in all_strips:
                scores(i, b + 2, 0, c)
            for c in all_strips:
                values(b, 0, c)
            for c in all_strips:
                softmax(1, c)
            return carry

        lax.fori_loop(0, i * (n_diag // 2), pair, 0)
        b0 = n_diag * i
        order = (0,) + tuple(range(n_diag - 1, 0, -1))
        for t in range(n_diag + 1):
            if 1 <= t + 1 < n_diag:
                d = order[t + 1]
                for c in seen(d):
                    scores(i, b0 + d, (t + 1) % 2, c, n_seen(d, c))
            if t == n_diag - 1 and has_next:
                first_scores(i + 1)
            if t == 0:
                for c in all_strips:
                    values(b0 - 1, 1, c)
            else:
                d = order[t - 1]
                for c in seen(d):
                    values(b0 + d, (t - 1) % 2, c, n_seen(d, c))
            if t < n_diag:
                d = order[t]
                for c in seen(d):
                    softmax(t % 2, c, diag_block=d)
        o = acc_ref[:V_HEAD, :] / acc_ref[V_HEAD:V_HEAD + 1, :]
        o_ref[pl.ds(pl.multiple_of(i * tq, tq), tq), :] = o.T.astype(o_ref.dtype)
        if has_next:
            reset_state()

    def tile_with_next(i, carry):
        tile(i, has_next=True)
        return carry

    first_scores(0)
    reset_state()
    lax.fori_loop(0, n_tiles - 1, tile_with_next, 0)
    tile(jnp.int32(n_tiles - 1), has_next=False)


def _prompt_attention(qt, k, vt, *, batch, seq, tq):
    heads, _, n = qt.shape
    tk = _ATTN_KEY_BLOCK
    return pl.pallas_call(
        functools.partial(_prompt_attn_kernel, tq=tq, tk=tk),
        grid=(batch, heads),
        in_specs=[pl.BlockSpec((1, QK_DIM, seq), lambda b, h: (h, 0, b)),
                  pl.BlockSpec((1, seq, QK_DIM), lambda b, h: (h, b, 0)),
                  pl.BlockSpec((1, vt.shape[1], seq), lambda b, h: (h, 0, b))],
        out_specs=pl.BlockSpec((seq, V_HEAD), lambda b, h: (b, h)),
        out_shape=jax.ShapeDtypeStruct((n, heads * V_HEAD), BF16),
        scratch_shapes=[pltpu.VMEM((2, tk, tq), F32), pltpu.VMEM((2, 1, tq), F32),
                        pltpu.VMEM((2, tk, tq), BF16),
                        pltpu.VMEM((2, 1, tq), F32), pltpu.VMEM((1, tq), F32),
                        pltpu.VMEM((vt.shape[1], tq), F32)],
        compiler_params=pltpu.CompilerParams(
            dimension_semantics=("arbitrary", "arbitrary"),
            vmem_limit_bytes=_vmem_limit(48 << 20)),
        name="prompt_attention",
    )(qt, k, vt)


def _sample_attn_kernel(qbd_ref, qr_ref, latc_ref, kpec_ref, latn_ref, kpen_ref, wkt_ref, wv_ref,
                        gk_ref, o_ref, *, new):
    nt = (((1,), (1,)), ((), ()))

    def scores(lat_ref, kpe_ref):
        c = lat_ref[0].astype(BF16)
        kt = lax.dot_general(wkt_ref[...], c, nt, preferred_element_type=F32)
        gk = gk_ref[:, :c.shape[0]]
        k_heads = []
        for hd in range(MLA_HEADS):
            kh = kt[hd * QK_NOPE:(hd + 1) * QK_NOPE, :]
            inv = lax.rsqrt(jnp.sum(kh * kh, axis=0, keepdims=True) * (1.0 / QK_NOPE) + EPS)
            k_heads.append((kh * inv * gk).astype(BF16))
        kt = jnp.concatenate(k_heads, axis=0)
        s = (jnp.dot(qbd_ref[0], kt, preferred_element_type=F32)
             + lax.dot_general(qr_ref[0], kpe_ref[0].astype(BF16), nt,
                               preferred_element_type=F32))
        return s, c

    s_c, c_c = scores(latc_ref, kpec_ref)
    s_n, c_n = scores(latn_ref, kpen_ref)
    m = jnp.maximum(jnp.max(s_c, axis=-1, keepdims=True), jnp.max(s_n, axis=-1, keepdims=True))
    p_c = jnp.exp2(s_c - m)
    p_n = jnp.exp2(s_n - m)
    den = jnp.sum(p_c, axis=-1, keepdims=True) + jnp.sum(p_n, axis=-1, keepdims=True)
    ctx = (jnp.dot(p_c.astype(BF16), c_c, preferred_element_type=F32)
           + jnp.dot(p_n.astype(BF16), c_n, preferred_element_type=F32))
    ctx = ctx.astype(BF16)
    for hd in range(MLA_HEADS):
        rows = slice(hd * new, (hd + 1) * new)
        o = jnp.dot(ctx[rows, :], wv_ref[:, hd * V_HEAD:(hd + 1) * V_HEAD],
                    preferred_element_type=F32)
        o_ref[0, :, hd * V_HEAD:(hd + 1) * V_HEAD] = (o / den[rows, :]).astype(o_ref.dtype)


def _sample_attention(qt, lat_c, kpe_c, lat_n, kpe_n, w_kv, gk, *, batch, new):
    heads = qt.shape[0]
    kw = heads * QK_NOPE
    q4 = qt.reshape(heads, QK_DIM, batch, new)
    qn = jnp.transpose(q4[:, :QK_NOPE], (2, 0, 3, 1))
    qbd = jnp.einsum('bhjd,hg->bhjgd', qn, jnp.eye(heads, dtype=qn.dtype))
    qbd = qbd.reshape(batch, heads * new, kw)
    qr = jnp.transpose(q4[:, QK_NOPE:], (2, 0, 3, 1)).reshape(batch, heads * new, QK_ROPE)
    w_kt, w_v = w_kv[:, :kw].T, w_kv[:, kw:]
    gk = jnp.broadcast_to(gk.reshape(-1, 1), (QK_NOPE, lat_c.shape[1]))
    per_batch = lambda a: pl.BlockSpec((1,) + a.shape[1:], lambda b: (b, 0, 0))
    return pl.pallas_call(
        functools.partial(_sample_attn_kernel, new=new),
        grid=(batch,),
        in_specs=[per_batch(qbd), per_batch(qr), per_batch(lat_c), per_batch(kpe_c),
                  per_batch(lat_n), per_batch(kpe_n), _const_spec(w_kt.shape),
                  _const_spec(w_v.shape), _const_spec(gk.shape)],
        out_specs=pl.BlockSpec((1, new, heads * V_HEAD), lambda b: (b, 0, 0)),
        out_shape=jax.ShapeDtypeStruct((batch, new, heads * V_HEAD), BF16),
        compiler_params=pltpu.CompilerParams(
            dimension_semantics=("arbitrary",), vmem_limit_bytes=_vmem_limit(48 << 20)),
        name="sample_attention",
    )(qbd, qr, lat_c, kpe_c, lat_n, kpe_n, w_kt, w_v, gk)


_RET_SUB_CHUNKS = 4


def _retention_kernel(q_ref, k_ref, v_ref, rg_ref, dec_ref, qd_ref, kd_ref, g_ref, s0_ref,
                      o_ref, sfin_ref, state_ref, *, chunk, n_sub, n_steps):
    c = pl.program_id(1)

    @pl.when(c == 0)
    def _():
        state_ref[...] = s0_ref[0]

    heads = range(RET_HEADS)
    kcols = [slice(hd * RET_DK, (hd + 1) * RET_DK) for hd in heads]
    vcols = [slice(hd * RET_DV, (hd + 1) * RET_DV) for hd in heads]
    qd = [qd_ref[hd] for hd in heads]

    def recur(j, state):
        rows = slice(j * chunk, (j + 1) * chunk)
        q = [q_ref[rows, kcols[hd]] for hd in heads]
        k = [k_ref[rows, kcols[hd]] for hd in heads]
        v = [v_ref[rows, vcols[hd]] for hd in heads]
        s = [_qk(q[hd], k[hd]) for hd in heads]
        cross = [jnp.dot(q[hd], state[hd].astype(BF16), preferred_element_type=F32)
                 for hd in heads]
        s = [(s[hd] * dec_ref[hd]).astype(BF16) for hd in heads]
        k_dec = [(k[hd].astype(F32) * kd_ref[hd]).astype(BF16) for hd in heads]
        inner = [jnp.dot(s[hd], v[hd], preferred_element_type=F32) for hd in heads]
        kv = [lax.dot_general(k_dec[hd], v[hd], (((0,), (0,)), ((), ())),
                              preferred_element_type=F32) for hd in heads]
        new_state = [state[hd] * qd[hd][chunk - 1:chunk, :] + kv[hd] for hd in heads]
        return [inner[hd] + cross[hd] * qd[hd] for hd in heads], new_state

    def finish(j, o):
        rows = slice(j * chunk, (j + 1) * chunk)
        for hd in heads:
            mu = jnp.mean(o[hd], axis=-1, keepdims=True)
            oc = o[hd] - mu
            var = jnp.mean(oc * oc, axis=-1, keepdims=True)
            y = oc * lax.rsqrt(var + EPS) * g_ref[hd]
            rg = rg_ref[rows, vcols[hd]].astype(F32)
            o_ref[rows, vcols[hd]] = (rg * jax.nn.sigmoid(rg) * y).astype(o_ref.dtype)

    state = [state_ref[hd] for hd in heads]
    o_prev = None
    for j in range(n_sub):
        o, state = recur(j, state)
        if o_prev is not None:
            finish(j - 1, o_prev)
        o_prev = o
    finish(n_sub - 1, o_prev)
    for hd in heads:
        state_ref[hd] = state[hd]

    @pl.when(c == n_steps - 1)
    def _():
        sfin_ref[0] = state_ref[...]


def _retention_tables(chunk):
    lg = jnp.log1p(-(2.0 ** (-5.0 - jnp.arange(RET_HEADS, dtype=F32))))
    idx = jnp.arange(chunk, dtype=F32)
    diff = idx[:, None] - idx[None, :]
    decay = jnp.where(diff[None] >= 0,
                      jnp.exp(jnp.maximum(diff, 0.0)[None] * lg[:, None, None]), 0.0)
    q_dec = jnp.exp((idx + 1.0)[None, :] * lg[:, None])[..., None]
    k_dec = jnp.exp((chunk - 1.0 - idx)[None, :] * lg[:, None])[..., None]
    return decay, q_dec, k_dec


def _retention(rq, rk, rv, rg, g_ret, s0, *, batch, seq, chunk):
    n = rq.shape[0]
    n_sub = min(_RET_SUB_CHUNKS, seq // chunk)
    rows = n_sub * chunk
    ns = seq // rows
    decay, q_dec, k_dec = _retention_tables(chunk)
    tok = lambda w: pl.BlockSpec((rows, w), lambda b, c: (b * ns + c, 0))
    state_spec = pl.BlockSpec((1, RET_HEADS, RET_DK, RET_DV), lambda b, c: (b, 0, 0, 0))
    g_ret = g_ret.reshape(RET_HEADS, 1, RET_DV)
    return pl.pallas_call(
        functools.partial(_retention_kernel, chunk=chunk, n_sub=n_sub, n_steps=ns),
        grid=(batch, ns),
        in_specs=[tok(RET_HEADS * RET_DK), tok(RET_HEADS * RET_DK), tok(RET_HEADS * RET_DV),
                  tok(RET_HEADS * RET_DV), _const_spec(decay.shape), _const_spec(q_dec.shape),
                  _const_spec(k_dec.shape), _const_spec(g_ret.shape), state_spec],
        out_specs=[tok(RET_HEADS * RET_DV), state_spec],
        out_shape=(jax.ShapeDtypeStruct((n, RET_HEADS * RET_DV), BF16),
                   jax.ShapeDtypeStruct((batch, RET_HEADS, RET_DK, RET_DV), F32)),
        scratch_shapes=[pltpu.VMEM((RET_HEADS, RET_DK, RET_DV), F32)],
        compiler_params=pltpu.CompilerParams(dimension_semantics=("arbitrary", "arbitrary")),
        name="retention",
    )(rq, rk, rv, rg, decay, q_dec, k_dec, g_ret, s0)


def _merge_kernel(x_ref, a_ref, r_ref, ga_ref, gb_ref, wa_ref, wb_ref, wo_ref, o_ref):
    a_d = jnp.dot(a_ref[...], wa_ref[...], preferred_element_type=F32)
    r_d = jnp.dot(r_ref[...], wb_ref[...], preferred_element_type=F32)
    merged = (jax.nn.sigmoid(ga_ref[...].astype(F32)) * a_d
              + jax.nn.sigmoid(gb_ref[...].astype(F32)) * r_d)
    o_ref[...] = x_ref[...] + jnp.dot(merged.astype(BF16), wo_ref[...],
                                      preferred_element_type=F32)


def _merge(x, attn, ret, ga, gb, wa, wb, wo, *, tm):
    n, d = x.shape
    row = pl.BlockSpec((tm, d), lambda i: (i, 0))
    return pl.pallas_call(
        _merge_kernel,
        grid=(n // tm,),
        in_specs=[row, row, row, row, row,
                  _const_spec(wa.shape), _const_spec(wb.shape), _const_spec(wo.shape)],
        out_specs=row,
        out_shape=jax.ShapeDtypeStruct((n, d), F32),
        compiler_params=pltpu.CompilerParams(
            dimension_semantics=("arbitrary",), vmem_limit_bytes=_vmem_limit(40 << 20)),
        name="merge",
    )(x, attn, ret, ga, gb, wa, wb, wo)


_PAD_ROWS = 8


def _ffn_kernel(x_ref, g_ref, wup_ref, cw_ref, cb_ref, wdn_ref, prev_ref,
                y_ref, st_ref, ubuf_ref, *, rows, n_seqs, d_ff, n_tiles):
    j = pl.program_id(1)
    lo = _PAD_ROWS - (CONV_W - 1)
    stride = _PAD_ROWS + rows
    assert n_seqs == 1 or n_tiles == 1

    @pl.when(j == 0)
    def _():
        for b in range(n_seqs):
            ubuf_ref[b * stride + lo:b * stride + _PAD_ROWS, :] = prev_ref[b]

    if n_tiles > 1:
        @pl.when(j > 0)
        def _():
            ubuf_ref[0:_PAD_ROWS, :] = ubuf_ref[rows:rows + _PAD_ROWS, :]

    xf = x_ref[...]
    hn = (xf * _rms_scale(xf, xf.shape[-1]) * g_ref[...]).astype(BF16)
    u = jnp.dot(hn, wup_ref[...], preferred_element_type=F32)
    for b in range(n_seqs):
        ubuf_ref[b * stride + _PAD_ROWS:(b + 1) * stride, :] = u[b * rows:(b + 1) * rows, :]
    runs = []
    for b in range(n_seqs):
        c = cb_ref[...]
        for tap in range(CONV_W):
            first = b * stride + lo + tap
            c = c + ubuf_ref[first:first + rows, :] * cw_ref[tap:tap + 1, :]
        runs.append(c)
    c = runs[0] if n_seqs == 1 else jnp.concatenate(runs, axis=0)
    gate, val = c[:, :d_ff], c[:, d_ff:]
    act = (gate * jax.nn.sigmoid(gate) * val).astype(BF16)
    y_ref[...] = xf + jnp.dot(act, wdn_ref[...], preferred_element_type=F32)

    @pl.when(j == n_tiles - 1)
    def _():
        for b in range(n_seqs):
            st_ref[b] = ubuf_ref[(b + 1) * stride - (CONV_W - 1):(b + 1) * stride, :]


def _conv_ffn(x, g, w_up, cw, cb, w_dn, prev, *, batch, seq, tm):
    n, d = x.shape
    d_ff = w_dn.shape[0]
    n_seqs = max(tm // seq, 1)
    rows = tm // n_seqs
    nt = seq // rows
    row = pl.BlockSpec((tm, d), lambda b, j: (b * nt + j, 0))
    st = pl.BlockSpec((n_seqs, CONV_W - 1, 2 * d_ff), lambda b, j: (b, 0, 0))
    return pl.pallas_call(
        functools.partial(_ffn_kernel, rows=rows, n_seqs=n_seqs, d_ff=d_ff, n_tiles=nt),
        grid=(batch // n_seqs, nt),
        in_specs=[row, _const_spec(g.shape), _const_spec(w_up.shape), _const_spec(cw.shape),
                  _const_spec(cb.shape), _const_spec(w_dn.shape), st],
        out_specs=[row, st],
        out_shape=(jax.ShapeDtypeStruct((n, d), F32),
                   jax.ShapeDtypeStruct((batch, CONV_W - 1, 2 * d_ff), F32)),
        scratch_shapes=[pltpu.VMEM((n_seqs * (_PAD_ROWS + rows), 2 * d_ff), F32)],
        compiler_params=pltpu.CompilerParams(
            dimension_semantics=("arbitrary", "arbitrary"),
            vmem_limit_bytes=_vmem_limit(56 << 20)),
        name="conv_ffn",
    )(x, g, w_up, cw, cb, w_dn, prev)


def _pack_weights(w_in, w_q_up, w_kv_up, w_o_branch, w_out, w_ffn_up, w_ffn_down):
    head_width = sum(width for _, width in _IN_HEAD)
    w_head = jnp.pad(w_in[:, :_IN_HEAD_SRC_WIDTH].astype(BF16),
                     ((0, 0), (0, head_width - _IN_HEAD_SRC_WIDTH)))
    w_in_p = (w_head, w_in[:, _IN_HEAD_SRC_WIDTH:].astype(BF16))
    w_q_p = w_q_up.T.astype(BF16)
    wkv = w_kv_up.reshape(w_kv_up.shape[0], MLA_HEADS, QK_NOPE + V_HEAD)
    w_kv_p = jnp.concatenate([wkv[:, :, :QK_NOPE].reshape(wkv.shape[0], -1),
                              wkv[:, :, QK_NOPE:].reshape(wkv.shape[0], -1)], axis=1).astype(BF16)
    mla_width = MLA_HEADS * V_HEAD
    return (w_in_p, w_q_p, w_kv_p, w_o_branch[:mla_width].astype(BF16),
            w_o_branch[mla_width:].astype(BF16), w_out.astype(BF16),
            w_ffn_up.astype(BF16), w_ffn_down.astype(BF16))


class _Tiles(NamedTuple):
    tm: int
    up_tm: int
    tq: int
    ret_chunk: int
    ffn_tm: int


_TOKEN_TILE = 512


def _tiles(batch, seq):
    if seq >= _TOKEN_TILE:
        return _Tiles(tm=_TOKEN_TILE, up_tm=2 * _TOKEN_TILE, tq=4 * _ATTN_KEY_BLOCK,
                      ret_chunk=MXU_TILE, ffn_tm=_TOKEN_TILE)
    return _Tiles(tm=batch * seq, up_tm=batch * seq, tq=seq, ret_chunk=seq,
                  ffn_tm=batch * seq)


def _layer(x, pos, past, w, *, batch, seq):
    n = x.shape[0]
    tm, up_tm, tq, ret_chunk, ffn_tm = _tiles(batch, seq)
    tabs64 = _rope_tables(pos, RET_DK // 2)
    tabs32 = _rope_tables(pos, QK_ROPE // 2)
    seq_tiles = max(seq // tm, 1)
    if seq < tm:
        tabs64 = tuple(np.tile(t, (tm // seq, 1)) for t in tabs64)
        tabs32 = tuple(np.tile(t, (tm // seq, 1)) for t in tabs32)
    qlat, ckv, kpe, rq, rk, rv, rg, ga, gb = _in_proj(
        x, w['w_in'], w['g_norm_mix'], w['g_q_lat'], w['g_kv_lat'], w['g_k_rope'],
        tabs64, tabs32, seq_tiles=seq_tiles, tm=tm)
    tabs32_t = tuple(np.ascontiguousarray(t[:, :QK_ROPE].T) for t in tabs32)
    bcast = lambda g: jnp.broadcast_to(g.reshape(-1, 1), (g.size, up_tm))
    qt = _q_up(qlat, w['w_q_up'], bcast(w['g_q_nope']), bcast(w['g_q_rope']), tabs32_t,
               seq_tiles=max(seq // up_tm, 1), tm=up_tm)
    if past is None:
        k_new, vt_new = _kv_up(ckv, kpe, w['w_kv_up'], w['g_k_nope'], tm=up_tm)
        attn = _prompt_attention(qt, k_new, vt_new, batch=batch, seq=seq, tq=tq)
        s0 = jnp.zeros((batch, RET_HEADS, RET_DK, RET_DV), F32)
        prev = jnp.zeros((batch, CONV_W - 1, w['w_ffn_up'].shape[1]), F32)
    else:
        lat_c, pe_c, s0, prev = past
        attn = _sample_attention(qt, lat_c, pe_c, ckv.reshape(batch, seq, -1),
                                 kpe.reshape(batch, seq, -1), w['w_kv_up'], w['g_k_nope'],
                                 batch=batch, new=seq).reshape(n, -1)
    ret, s_fin = _retention(rq, rk, rv, rg, w['g_ret_out'], s0,
                            batch=batch, seq=seq, chunk=ret_chunk)
    x1 = _merge(x, attn, ret, ga, gb, w['w_o_a'], w['w_o_b'], w['w_out'], tm=up_tm)
    y, conv_state = _conv_ffn(x1, w['g_norm_ffn'], w['w_ffn_up'], w['ffn_conv_w'],
                              w['ffn_conv_b'], w['w_ffn_down'], prev,
                              batch=batch, seq=seq, tm=ffn_tm)
    return y, (ckv, kpe, s_fin, conv_state)


def kernel(x_prompt, x_sample, cache_mla_latent, cache_mla_rope_key, state_retention, state_ffn_conv, g_norm_mix, w_in, g_q_lat, w_q_up, g_q_nope, g_q_rope, g_kv_lat, w_kv_up, g_k_nope, g_k_rope, g_ret_out, w_o_branch, w_out, g_norm_ffn, w_ffn_up, ffn_conv_w, ffn_conv_b, w_ffn_down):
    depth = w_in.shape[0]
    assert depth == 1, "single-layer trunk"
    bp, tp, d = x_prompt.shape
    bs, ts, _ = x_sample.shape
    past_len = cache_mla_latent.shape[2]
    (w_in_p, w_q_p, w_kv_p, w_o_a, w_o_b, w_out_p, w_up_p, w_dn_p) = _pack_weights(
        w_in[0], w_q_up[0], w_kv_up[0], w_o_branch[0], w_out[0], w_ffn_up[0], w_ffn_down[0])
    row = lambda g: g.reshape(1, -1).astype(F32)
    w = dict(
        w_in=w_in_p, w_q_up=w_q_p, w_kv_up=w_kv_p, w_o_a=w_o_a, w_o_b=w_o_b, w_out=w_out_p,
        w_ffn_up=w_up_p, w_ffn_down=w_dn_p,
        g_norm_mix=row(g_norm_mix[0]), g_q_lat=row(g_q_lat[0]), g_kv_lat=row(g_kv_lat[0]),
        g_k_rope=jnp.pad(row(g_k_rope[0]), ((0, 0), (0, LANES - QK_ROPE))),
        g_q_nope=g_q_nope[0].astype(F32), g_q_rope=g_q_rope[0].astype(F32),
        g_k_nope=row(g_k_nope[0]), g_ret_out=g_ret_out[0], g_norm_ffn=row(g_norm_ffn[0]),
        ffn_conv_w=ffn_conv_w[0], ffn_conv_b=row(ffn_conv_b[0]))

    yp, sp = _layer(x_prompt.reshape(bp * tp, d), np.arange(tp), None, w, batch=bp, seq=tp)
    past = (cache_mla_latent[0], cache_mla_rope_key[0], state_retention[0], state_ffn_conv[0])
    ys, ss = _layer(x_sample.reshape(bs * ts, d), past_len + np.arange(ts), past, w,
                    batch=bs, seq=ts)

    def states(st, b, t):
        ckv, kpe, s_fin, conv = st
        return (ckv.reshape(1, b, t, -1), kpe.reshape(1, b, t, -1), s_fin[None], conv[None])

    return (yp.reshape(bp, tp, d), ys.reshape(bs, ts, d)) + states(sp, bp, tp) + states(ss, bs, ts)
```

```python
import functools
import math
from typing import NamedTuple

import jax
import jax.numpy as jnp
import numpy as np
from jax import lax
from jax.experimental import pallas as pl
from jax.experimental.pallas import tpu as pltpu

F32 = jnp.float32
BF16 = jnp.bfloat16

EPS = 1e-6
NEG_INF = -1e30
ROPE_THETA = 10000.0
CHUNK = 64
LANES = 128
MXU_TILE = 256
V7X_VMEM_BYTES = 64 * 1024 * 1024

D_MODEL = 1024
Q_LORA = 768
KV_LORA = 512
MLA_HEADS = 8
QK_NOPE = 128
QK_ROPE = 64
QK_DIM = QK_NOPE + QK_ROPE
V_HEAD = 128
MLA_SCALE = QK_DIM ** -0.5
Q_PRESCALE = MLA_SCALE * math.log2(math.e)
RET_HEADS = 4
RET_DK = 128
RET_DV = 256
CONV_W = 3


def _vmem_limit(nbytes):
    return int(min(V7X_VMEM_BYTES - (4 << 20), max(32 << 20, nbytes)))


def _const_spec(shape):
    nd = len(shape)
    return pl.BlockSpec(shape, lambda *_: (0,) * nd, pipeline_mode=pl.Buffered(1))


def _rms_scale(z, width):
    return lax.rsqrt(jnp.sum(z * z, axis=-1, keepdims=True) * (1.0 / width) + EPS)


def _swap_halves_64(z):
    lane = lax.broadcasted_iota(jnp.int32, z.shape, 1)
    return jnp.where((lane % 64) < 32, pltpu.roll(z, 96, 1), pltpu.roll(z, 32, 1))


def _rope_tables(pos, half):
    freqs = ROPE_THETA ** (-np.arange(half, dtype=np.float64) / half)
    ang = pos.astype(np.float64)[:, None] * freqs[None, :]
    cos, sin = np.cos(ang), np.sin(ang)
    reps = LANES // (2 * half)
    return (np.tile(np.concatenate([cos, cos], -1), (1, reps)).astype(np.float32),
            np.tile(np.concatenate([-sin, sin], -1), (1, reps)).astype(np.float32))


_IN_HEAD = (('qlat', Q_LORA), ('kvl', KV_LORA), ('kpe', LANES))
_IN_TAIL = (('rq', RET_HEADS * RET_DK), ('rk', RET_HEADS * RET_DK), ('rv', RET_HEADS * RET_DV),
            ('rg', RET_HEADS * RET_DV), ('ga', D_MODEL), ('gb', D_MODEL))
_IN_HEAD_SRC_WIDTH = Q_LORA + KV_LORA + QK_ROPE


def _group_offsets(groups):
    offs, o = {}, 0
    for name, width in groups:
        offs[name] = (o, o + width)
        o += width
    return offs


_IN_OFFS = {**{n: (0,) + ab for n, ab in _group_offsets(_IN_HEAD).items()},
            **{n: (1,) + ab for n, ab in _group_offsets(_IN_TAIL).items()}}


def _inproj_kernel(x_ref, gmix_ref, wh_ref, wt_ref, gq_ref, gkv_ref, gkpe_ref,
                   c64_ref, s64_ref, c32_ref, s32_ref,
                   qlat_ref, ckv_ref, kpe_ref, rq_ref, rk_ref, rv_ref, rg_ref, ga_ref, gb_ref):
    xf = x_ref[...]
    h = (xf * _rms_scale(xf, xf.shape[-1]) * gmix_ref[...]).astype(BF16)

    def proj(name):
        part, a, b = _IN_OFFS[name]
        w_ref = (wh_ref, wt_ref)[part]
        return jnp.dot(h, w_ref[:, a:b], preferred_element_type=F32)

    z = proj('qlat')
    qlat_ref[...] = (z * _rms_scale(z, z.shape[-1]) * gq_ref[...]).astype(BF16)
    z = proj('kvl')
    ckv_ref[...] = z * _rms_scale(z, z.shape[-1]) * gkv_ref[...]
    z = proj('kpe')
    zn = z * _rms_scale(z, QK_ROPE) * gkpe_ref[...]
    kpe = zn * c32_ref[...] + _swap_halves_64(zn) * s32_ref[...]
    kpe_ref[...] = kpe[:, :QK_ROPE]
    c64, s64 = c64_ref[...], s64_ref[...]
    for name, ref, scale in (('rq', rq_ref, 1.0), ('rk', rk_ref, RET_DK ** -0.5)):
        z = proj(name)
        for hd in range(RET_HEADS):
            zh = z[:, hd * RET_DK:(hd + 1) * RET_DK]
            r = zh * c64 + pltpu.roll(zh, RET_DK // 2, 1) * s64
            if scale != 1.0:
                r = r * scale
            ref[:, hd * RET_DK:(hd + 1) * RET_DK] = r.astype(BF16)
    rv_ref[...] = proj('rv').astype(BF16)
    rg_ref[...] = proj('rg').astype(BF16)
    ga_ref[...] = proj('ga').astype(BF16)
    gb_ref[...] = proj('gb').astype(BF16)


def _in_proj(x, w_in_parts, gmix, gq, gkv, gkpe, tabs64, tabs32, *, seq_tiles, tm):
    n, d = x.shape
    w_head, w_tail = w_in_parts
    row = lambda w: pl.BlockSpec((tm, w), lambda i: (i, 0))
    tab = pl.BlockSpec((tm, LANES), lambda i: (i % seq_tiles, 0))
    width = dict(_IN_HEAD + _IN_TAIL)
    out_dtypes = dict(qlat=BF16, kvl=F32, kpe=F32, rq=BF16, rk=BF16, rv=BF16, rg=BF16,
                      ga=BF16, gb=BF16)
    out_shapes = tuple(
        jax.ShapeDtypeStruct((n, QK_ROPE if name == 'kpe' else width[name]), dtype)
        for name, dtype in out_dtypes.items())
    return pl.pallas_call(
        _inproj_kernel,
        grid=(n // tm,),
        in_specs=[row(d), _const_spec(gmix.shape), _const_spec(w_head.shape),
                  _const_spec(w_tail.shape), _const_spec(gq.shape), _const_spec(gkv.shape),
                  _const_spec(gkpe.shape), tab, tab, tab, tab],
        out_specs=[row(s.shape[1]) for s in out_shapes],
        out_shape=out_shapes,
        compiler_params=pltpu.CompilerParams(
            dimension_semantics=("arbitrary",), vmem_limit_bytes=_vmem_limit(56 << 20)),
        name="in_proj",
    )(x, gmix, w_head, w_tail, gq, gkv, gkpe, *tabs64, *tabs32)


_QUP_GROUPS = 4


def _qup_kernel(ql_ref, wt_ref, gn_ref, gr_ref, c_ref, s_ref, qt_ref):
    gn, gr, cos, sin = gn_ref[...], gr_ref[...], c_ref[...], s_ref[...]
    half = QK_ROPE // 2
    ql = ql_ref[...]

    group = MLA_HEADS // _QUP_GROUPS

    def project(g):
        return lax.dot_general(wt_ref[g * group * QK_DIM:(g + 1) * group * QK_DIM, :], ql,
                               (((1,), (1,)), ((), ())), preferred_element_type=F32)

    def finish(g, zt):
        for j in range(group):
            hd = g * group + j
            zn = zt[j * QK_DIM:j * QK_DIM + QK_NOPE, :]
            inv = lax.rsqrt(jnp.sum(zn * zn, axis=0, keepdims=True) * (1.0 / QK_NOPE) + EPS)
            qt_ref[hd, :QK_NOPE, :] = (zn * (inv * Q_PRESCALE) * gn).astype(BF16)
            zr = zt[j * QK_DIM + QK_NOPE:(j + 1) * QK_DIM, :]
            inv = lax.rsqrt(jnp.sum(zr * zr, axis=0, keepdims=True) * (1.0 / QK_ROPE) + EPS)
            z = zr * (inv * Q_PRESCALE) * gr
            swapped = jnp.concatenate([z[half:], z[:half]], axis=0)
            qt_ref[hd, QK_NOPE:, :] = (z * cos + swapped * sin).astype(BF16)

    zt = project(0)
    for g in range(_QUP_GROUPS):
        zt_next = project(g + 1) if g + 1 < _QUP_GROUPS else None
        finish(g, zt)
        zt = zt_next


def _q_up(qlat, w_qt, gn, gr, tabs32_t, *, seq_tiles, tm):
    n = qlat.shape[0]
    tab = pl.BlockSpec((QK_ROPE, tm), lambda i: (0, i % seq_tiles))
    return pl.pallas_call(
        _qup_kernel,
        grid=(n // tm,),
        in_specs=[pl.BlockSpec((tm, qlat.shape[1]), lambda i: (i, 0)), _const_spec(w_qt.shape),
                  _const_spec(gn.shape), _const_spec(gr.shape), tab, tab],
        out_specs=pl.BlockSpec((MLA_HEADS, QK_DIM, tm), lambda i: (0, 0, i)),
        out_shape=jax.ShapeDtypeStruct((MLA_HEADS, QK_DIM, n), BF16),
        compiler_params=pltpu.CompilerParams(
            dimension_semantics=("arbitrary",), vmem_limit_bytes=_vmem_limit(40 << 20)),
        name="q_up",
    )(qlat, w_qt, gn, gr, *tabs32_t)


_ONES_ROWS = 16


def _kvup_kernel(c_ref, kpe_ref, w_ref, wvt_ref, gk_ref, k_ref, vt_ref):
    c = c_ref[...].astype(BF16)
    kpe = kpe_ref[...].astype(BF16)
    gk = gk_ref[...]
    n_pairs = MLA_HEADS // 2

    def project(pair):
        cols = slice(pair * MXU_TILE, (pair + 1) * MXU_TILE)
        zk = jnp.dot(c, w_ref[:, cols], preferred_element_type=F32)
        zvt = lax.dot_general(wvt_ref[cols, :], c, (((1,), (1,)), ((), ())),
                              preferred_element_type=F32)
        return zk, zvt

    def finish(pair, zk, zvt):
        for sub in range(2):
            hd = 2 * pair + sub
            zh = zk[:, sub * QK_NOPE:(sub + 1) * QK_NOPE]
            k_ref[hd, :, :QK_NOPE] = (zh * _rms_scale(zh, QK_NOPE) * gk).astype(BF16)
            k_ref[hd, :, QK_NOPE:] = kpe
            vt_ref[hd, :V_HEAD, :] = zvt[sub * V_HEAD:(sub + 1) * V_HEAD, :].astype(BF16)
            vt_ref[hd, V_HEAD:, :] = jnp.ones((_ONES_ROWS, zvt.shape[1]), BF16)

    z = project(0)
    for pair in range(n_pairs):
        z_next = project(pair + 1) if pair + 1 < n_pairs else None
        finish(pair, *z)
        z = z_next


def _kv_up(ckv, kpe, w_kv, gk, *, tm):
    n = ckv.shape[0]
    kw = MLA_HEADS * QK_NOPE
    w_k, w_vt = w_kv[:, :kw], w_kv[:, kw:].T
    v_spec = pl.BlockSpec((MLA_HEADS, V_HEAD + _ONES_ROWS, tm), lambda i: (0, 0, i))
    v_shape = (MLA_HEADS, V_HEAD + _ONES_ROWS, n)
    return pl.pallas_call(
        _kvup_kernel,
        grid=(n // tm,),
        in_specs=[pl.BlockSpec((tm, ckv.shape[1]), lambda i: (i, 0)),
                  pl.BlockSpec((tm, QK_ROPE), lambda i: (i, 0)),
                  _const_spec(w_k.shape), _const_spec(w_vt.shape), _const_spec(gk.shape)],
        out_specs=[pl.BlockSpec((MLA_HEADS, tm, QK_DIM), lambda i: (0, i, 0)), v_spec],
        out_shape=(jax.ShapeDtypeStruct((MLA_HEADS, n, QK_DIM), BF16),
                   jax.ShapeDtypeStruct(v_shape, BF16)),
        compiler_params=pltpu.CompilerParams(
            dimension_semantics=("arbitrary",), vmem_limit_bytes=_vmem_limit(40 << 20)),
        name="kv_up",
    )(ckv, kpe, w_k, w_vt, gk)


def _qk(q, k):
    return lax.dot_general(q, k, (((1,), (1,)), ((), ())), preferred_element_type=F32)


_STRIP = MXU_TILE
_ATTN_KEY_BLOCK = 512


def _prompt_attn_kernel(qt_ref, k_ref, vt_ref, o_ref, s_ref, smax_ref, p_ref, a_ref, m_ref,
                        acc_ref, *, tq, tk):
    n_tiles = qt_ref.shape[2] // tq
    n_strips = tq // _STRIP
    n_diag = tq // tk
    assert n_diag % 2 == 0 and tk % _STRIP == 0 and _STRIP % CHUNK == 0
    all_strips = tuple(range(n_strips))
    cols = lambda c: slice(c * _STRIP, (c + 1) * _STRIP)
    n_seen = lambda d, c: max(0, min(tk, (c + 1) * _STRIP - d * tk))
    seen = lambda d: tuple(c for c in all_strips if n_seen(d, c) > 0)
    unmasked = lambda d, c: c * _STRIP >= (d + 1) * tk

    def scores(i, b, slot, c, nk=tk):
        start = pl.multiple_of(b * tk, tk)
        q_start = pl.multiple_of(i * tq + c * _STRIP, _STRIP)
        s = jnp.dot(k_ref[0, pl.ds(start, nk), :], qt_ref[0, :, pl.ds(q_start, _STRIP)],
                    preferred_element_type=F32)
        s_ref[slot, :nk, cols(c)] = s
        smax_ref[slot, :, cols(c)] = jnp.max(s, axis=0, keepdims=True)

    def softmax(slot, c, diag_block=None):
        nk = tk if diag_block is None else n_seen(diag_block, c)
        s = s_ref[slot, :nk, cols(c)]
        if diag_block is None or unmasked(diag_block, c):
            s_max = smax_ref[slot, :, cols(c)]
        else:
            kc = (lax.broadcasted_iota(jnp.int32, s.shape, 0) + diag_block * tk) // CHUNK
            qc = (lax.broadcasted_iota(jnp.int32, s.shape, 1) + c * _STRIP) // CHUNK
            s = jnp.where(kc <= qc, s, NEG_INF)
            s_max = jnp.max(s, axis=0, keepdims=True)
        m_prev = m_ref[:, cols(c)]
        m_new = jnp.maximum(m_prev, s_max)
        alpha = jnp.exp2(m_prev - m_new)
        p = jnp.exp2(s - m_new)
        m_ref[:, cols(c)] = m_new
        a_ref[slot, :, cols(c)] = alpha
        p_ref[slot, :nk, cols(c)] = p.astype(BF16)

    def values(b, slot, c, nk=tk):
        start = pl.multiple_of(jnp.maximum(b, 0) * tk, tk)
        pv = jnp.dot(vt_ref[0, :, pl.ds(start, nk)], p_ref[slot, :nk, cols(c)],
                     preferred_element_type=F32)
        acc_ref[:, cols(c)] = acc_ref[:, cols(c)] * a_ref[slot, :, cols(c)] + pv

    def first_scores(i):
        for c in all_strips:
            scores(i, 0, 0, c)

    def reset_state():
        m_ref[...] = jnp.full(m_ref.shape, NEG_INF, F32)
        acc_ref[...] = jnp.zeros(acc_ref.shape, F32)
        p_ref[1] = jnp.zeros(p_ref.shape[1:], BF16)
        a_ref[1] = jnp.ones(a_ref.shape[1:], F32)

    def tile(i, has_next):
        def pair(jj, carry):
            b = 2 * jj
            for c in all_strips:
                scores(i, b + 1, 1, c)
            for c in all_strips:
                values(b - 1, 1, c)
            for c in all_strips:
                softmax(0, c)
            for c in all_strips:
                scores(i, b + 2, 0, c)
            for c in all_strips:
                values(b, 0, c)
            for c in all_strips:
                softmax(1, c)
            return carry

        lax.fori_loop(0, i * (n_diag // 2), pair, 0)
        b0 = n_diag * i
        order = (0,) + tuple(range(n_diag - 1, 0, -1))
        for t in range(n_diag + 1):
            if 1 <= t + 1 < n_diag:
                d = order[t + 1]
                for c in seen(d):
                    scores(i, b0 + d, (t + 1) % 2, c, n_seen(d, c))
            if t == n_diag - 1 and has_next:
                first_scores(i + 1)
            if t == 0:
                for c in all_strips:
                    values(b0 - 1, 1, c)
            else:
                d = order[t - 1]
                for c in seen(d):
                    values(b0 + d, (t - 1) % 2, c, n_seen(d, c))
            if t < n_diag:
                d = order[t]
                for c in seen(d):
                    softmax(t % 2, c, diag_block=d)
        o = acc_ref[:V_HEAD, :] / acc_ref[V_HEAD:V_HEAD + 1, :]
        o_ref[pl.ds(pl.multiple_of(i * tq, tq), tq), :] = o.T.astype(o_ref.dtype)
        if has_next:
            reset_state()

    def tile_with_next(i, carry):
        tile(i, has_next=True)
        return carry

    first_scores(0)
    reset_state()
    lax.fori_loop(0, n_tiles - 1, tile_with_next, 0)
    tile(jnp.int32(n_tiles - 1), has_next=False)


def _prompt_attention(qt, k, vt, *, batch, seq, tq):
    heads, _, n = qt.shape
    tk = _ATTN_KEY_BLOCK
    return pl.pallas_call(
        functools.partial(_prompt_attn_kernel, tq=tq, tk=tk),
        grid=(batch, heads),
        in_specs=[pl.BlockSpec((1, QK_DIM, seq), lambda b, h: (h, 0, b)),
                  pl.BlockSpec((1, seq, QK_DIM), lambda b, h: (h, b, 0)),
                  pl.BlockSpec((1, vt.shape[1], seq), lambda b, h: (h, 0, b))],
        out_specs=pl.BlockSpec((seq, V_HEAD), lambda b, h: (b, h)),
        out_shape=jax.ShapeDtypeStruct((n, heads * V_HEAD), BF16),
        scratch_shapes=[pltpu.VMEM((2, tk, tq), F32), pltpu.VMEM((2, 1, tq), F32),
                        pltpu.VMEM((2, tk, tq), BF16),
                        pltpu.VMEM((2, 1, tq), F32), pltpu.VMEM((1, tq), F32),
                        pltpu.VMEM((vt.shape[1], tq), F32)],
        compiler_params=pltpu.CompilerParams(
            dimension_semantics=("arbitrary", "arbitrary"),
            vmem_limit_bytes=_vmem_limit(48 << 20)),
        name="prompt_attention",
    )(qt, k, vt)


def _sample_attn_kernel(qbd_ref, qr_ref, latc_ref, kpec_ref, latn_ref, kpen_ref, wkt_ref, wv_ref,
                        gk_ref, o_ref, *, new):
    nt = (((1,), (1,)), ((), ()))

    def scores(lat_ref, kpe_ref):
        c = lat_ref[0].astype(BF16)
        kt = lax.dot_general(wkt_ref[...], c, nt, preferred_element_type=F32)
        gk = gk_ref[:, :c.shape[0]]
        k_heads = []
        for hd in range(MLA_HEADS):
            kh = kt[hd * QK_NOPE:(hd + 1) * QK_NOPE, :]
            inv = lax.rsqrt(jnp.sum(kh * kh, axis=0, keepdims=True) * (1.0 / QK_NOPE) + EPS)
            k_heads.append((kh * inv * gk).astype(BF16))
        kt = jnp.concatenate(k_heads, axis=0)
        s = (jnp.dot(qbd_ref[0], kt, preferred_element_type=F32)
             + lax.dot_general(qr_ref[0], kpe_ref[0].astype(BF16), nt,
                               preferred_element_type=F32))
        return s, c

    s_c, c_c = scores(latc_ref, kpec_ref)
    s_n, c_n = scores(latn_ref, kpen_ref)
    m = jnp.maximum(jnp.max(s_c, axis=-1, keepdims=True), jnp.max(s_n, axis=-1, keepdims=True))
    p_c = jnp.exp2(s_c - m)
    p_n = jnp.exp2(s_n - m)
    den = jnp.sum(p_c, axis=-1, keepdims=True) + jnp.sum(p_n, axis=-1, keepdims=True)
    ctx = (jnp.dot(p_c.astype(BF16), c_c, preferred_element_type=F32)
           + jnp.dot(p_n.astype(BF16), c_n, preferred_element_type=F32))
    ctx = ctx.astype(BF16)
    for hd in range(MLA_HEADS):
        rows = slice(hd * new, (hd + 1) * new)
        o = jnp.dot(ctx[rows, :], wv_ref[:, hd * V_HEAD:(hd + 1) * V_HEAD],
                    preferred_element_type=F32)
        o_ref[0, :, hd * V_HEAD:(hd + 1) * V_HEAD] = (o / den[rows, :]).astype(o_ref.dtype)


def _sample_attention(qt, lat_c, kpe_c, lat_n, kpe_n, w_kv, gk, *, batch, new):
    heads = qt.shape[0]
    kw = heads * QK_NOPE
    q4 = qt.reshape(heads, QK_DIM, batch, new)
    qn = jnp.transpose(q4[:, :QK_NOPE], (2, 0, 3, 1))
    qbd = jnp.einsum('bhjd,hg->bhjgd', qn, jnp.eye(heads, dtype=qn.dtype))
    qbd = qbd.reshape(batch, heads * new, kw)
    qr = jnp.transpose(q4[:, QK_NOPE:], (2, 0, 3, 1)).reshape(batch, heads * new, QK_ROPE)
    w_kt, w_v = w_kv[:, :kw].T, w_kv[:, kw:]
    gk = jnp.broadcast_to(gk.reshape(-1, 1), (QK_NOPE, lat_c.shape[1]))
    per_batch = lambda a: pl.BlockSpec((1,) + a.shape[1:], lambda b: (b, 0, 0))
    return pl.pallas_call(
        functools.partial(_sample_attn_kernel, new=new),
        grid=(batch,),
        in_specs=[per_batch(qbd), per_batch(qr), per_batch(lat_c), per_batch(kpe_c),
                  per_batch(lat_n), per_batch(kpe_n), _const_spec(w_kt.shape),
                  _const_spec(w_v.shape), _const_spec(gk.shape)],
        out_specs=pl.BlockSpec((1, new, heads * V_HEAD), lambda b: (b, 0, 0)),
        out_shape=jax.ShapeDtypeStruct((batch, new, heads * V_HEAD), BF16),
        compiler_params=pltpu.CompilerParams(
            dimension_semantics=("arbitrary",), vmem_limit_bytes=_vmem_limit(48 << 20)),
        name="sample_attention",
    )(qbd, qr, lat_c, kpe_c, lat_n, kpe_n, w_kt, w_v, gk)


_RET_SUB_CHUNKS = 4


def _retention_kernel(q_ref, k_ref, v_ref, rg_ref, dec_ref, qd_ref, kd_ref, g_ref, s0_ref,
                      o_ref, sfin_ref, state_ref, *, chunk, n_sub, n_steps):
    c = pl.program_id(1)

    @pl.when(c == 0)
    def _():
        state_ref[...] = s0_ref[0]

    heads = range(RET_HEADS)
    kcols = [slice(hd * RET_DK, (hd + 1) * RET_DK) for hd in heads]
    vcols = [slice(hd * RET_DV, (hd + 1) * RET_DV) for hd in heads]
    qd = [qd_ref[hd] for hd in heads]

    def recur(j, state):
        rows = slice(j * chunk, (j + 1) * chunk)
        q = [q_ref[rows, kcols[hd]] for hd in heads]
        k = [k_ref[rows, kcols[hd]] for hd in heads]
        v = [v_ref[rows, vcols[hd]] for hd in heads]
        s = [_qk(q[hd], k[hd]) for hd in heads]
        cross = [jnp.dot(q[hd], state[hd].astype(BF16), preferred_element_type=F32)
                 for hd in heads]
        s = [(s[hd] * dec_ref[hd]).astype(BF16) for hd in heads]
        k_dec = [(k[hd].astype(F32) * kd_ref[hd]).astype(BF16) for hd in heads]
        inner = [jnp.dot(s[hd], v[hd], preferred_element_type=F32) for hd in heads]
        kv = [lax.dot_general(k_dec[hd], v[hd], (((0,), (0,)), ((), ())),
                              preferred_element_type=F32) for hd in heads]
        new_state = [state[hd] * qd[hd][chunk - 1:chunk, :] + kv[hd] for hd in heads]
        return [inner[hd] + cross[hd] * qd[hd] for hd in heads], new_state

    def finish(j, o):
        rows = slice(j * chunk, (j + 1) * chunk)
        for hd in heads:
            mu = jnp.mean(o[hd], axis=-1, keepdims=True)
            oc = o[hd] - mu
            var = jnp.mean(oc * oc, axis=-1, keepdims=True)
            y = oc * lax.rsqrt(var + EPS) * g_ref[hd]
            rg = rg_ref[rows, vcols[hd]].astype(F32)
            o_ref[rows, vcols[hd]] = (rg * jax.nn.sigmoid(rg) * y).astype(o_ref.dtype)

    state = [state_ref[hd] for hd in heads]
    o_prev = None
    for j in range(n_sub):
        o, state = recur(j, state)
        if o_prev is not None:
            finish(j - 1, o_prev)
        o_prev = o
    finish(n_sub - 1, o_prev)
    for hd in heads:
        state_ref[hd] = state[hd]

    @pl.when(c == n_steps - 1)
    def _():
        sfin_ref[0] = state_ref[...]


def _retention_tables(chunk):
    lg = jnp.log1p(-(2.0 ** (-5.0 - jnp.arange(RET_HEADS, dtype=F32))))
    idx = jnp.arange(chunk, dtype=F32)
    diff = idx[:, None] - idx[None, :]
    decay = jnp.where(diff[None] >= 0,
                      jnp.exp(jnp.maximum(diff, 0.0)[None] * lg[:, None, None]), 0.0)
    q_dec = jnp.exp((idx + 1.0)[None, :] * lg[:, None])[..., None]
    k_dec = jnp.exp((chunk - 1.0 - idx)[None, :] * lg[:, None])[..., None]
    return decay, q_dec, k_dec


def _retention(rq, rk, rv, rg, g_ret, s0, *, batch, seq, chunk):
    n = rq.shape[0]
    n_sub = min(_RET_SUB_CHUNKS, seq // chunk)
    rows = n_sub * chunk
    ns = seq // rows
    decay, q_dec, k_dec = _retention_tables(chunk)
    tok = lambda w: pl.BlockSpec((rows, w), lambda b, c: (b * ns + c, 0))
    state_spec = pl.BlockSpec((1, RET_HEADS, RET_DK, RET_DV), lambda b, c: (b, 0, 0, 0))
    g_ret = g_ret.reshape(RET_HEADS, 1, RET_DV)
    return pl.pallas_call(
        functools.partial(_retention_kernel, chunk=chunk, n_sub=n_sub, n_steps=ns),
        grid=(batch, ns),
        in_specs=[tok(RET_HEADS * RET_DK), tok(RET_HEADS * RET_DK), tok(RET_HEADS * RET_DV),
                  tok(RET_HEADS * RET_DV), _const_spec(decay.shape), _const_spec(q_dec.shape),
                  _const_spec(k_dec.shape), _const_spec(g_ret.shape), state_spec],
        out_specs=[tok(RET_HEADS * RET_DV), state_spec],
        out_shape=(jax.ShapeDtypeStruct((n, RET_HEADS * RET_DV), BF16),
                   jax.ShapeDtypeStruct((batch, RET_HEADS, RET_DK, RET_DV), F32)),
        scratch_shapes=[pltpu.VMEM((RET_HEADS, RET_DK, RET_DV), F32)],
        compiler_params=pltpu.CompilerParams(dimension_semantics=("arbitrary", "arbitrary")),
        name="retention",
    )(rq, rk, rv, rg, decay, q_dec, k_dec, g_ret, s0)


def _merge_kernel(x_ref, a_ref, r_ref, ga_ref, gb_ref, wa_ref, wb_ref, wo_ref, o_ref):
    a_d = jnp.dot(a_ref[...], wa_ref[...], preferred_element_type=F32)
    r_d = jnp.dot(r_ref[...], wb_ref[...], preferred_element_type=F32)
    merged = (jax.nn.sigmoid(ga_ref[...].astype(F32)) * a_d
              + jax.nn.sigmoid(gb_ref[...].astype(F32)) * r_d)
    o_ref[...] = x_ref[...] + jnp.dot(merged.astype(BF16), wo_ref[...],
                                      preferred_element_type=F32)


def _merge(x, attn, ret, ga, gb, wa, wb, wo, *, tm):
    n, d = x.shape
    row = pl.BlockSpec((tm, d), lambda i: (i, 0))
    return pl.pallas_call(
        _merge_kernel,
        grid=(n // tm,),
        in_specs=[row, row, row, row, row,
                  _const_spec(wa.shape), _const_spec(wb.shape), _const_spec(wo.shape)],
        out_specs=row,
        out_shape=jax.ShapeDtypeStruct((n, d), F32),
        compiler_params=pltpu.CompilerParams(
            dimension_semantics=("arbitrary",), vmem_limit_bytes=_vmem_limit(40 << 20)),
        name="merge",
    )(x, attn, ret, ga, gb, wa, wb, wo)


_PAD_ROWS = 8


def _ffn_kernel(x_ref, g_ref, wup_ref, cw_ref, cb_ref, wdn_ref, prev_ref,
                y_ref, st_ref, ubuf_ref, *, rows, n_seqs, d_ff, n_tiles):
    j = pl.program_id(1)
    lo = _PAD_ROWS - (CONV_W - 1)
    stride = _PAD_ROWS + rows
    assert n_seqs == 1 or n_tiles == 1

    @pl.when(j == 0)
    def _():
        for b in range(n_seqs):
            ubuf_ref[b * stride + lo:b * stride + _PAD_ROWS, :] = prev_ref[b]

    if n_tiles > 1:
        @pl.when(j > 0)
        def _():
            ubuf_ref[0:_PAD_ROWS, :] = ubuf_ref[rows:rows + _PAD_ROWS, :]

    xf = x_ref[...]
    hn = (xf * _rms_scale(xf, xf.shape[-1]) * g_ref[...]).astype(BF16)
    u = jnp.dot(hn, wup_ref[...], preferred_element_type=F32)
    for b in range(n_seqs):
        ubuf_ref[b * stride + _PAD_ROWS:(b + 1) * stride, :] = u[b * rows:(b + 1) * rows, :]
    runs = []
    for b in range(n_seqs):
        c = cb_ref[...]
        for tap in range(CONV_W):
            first = b * stride + lo + tap
            c = c + ubuf_ref[first:first + rows, :] * cw_ref[tap:tap + 1, :]
        runs.append(c)
    c = runs[0] if n_seqs == 1 else jnp.concatenate(runs, axis=0)
    gate, val = c[:, :d_ff], c[:, d_ff:]
    act = (gate * jax.nn.sigmoid(gate) * val).astype(BF16)
    y_ref[...] = xf + jnp.dot(act, wdn_ref[...], preferred_element_type=F32)

    @pl.when(j == n_tiles - 1)
    def _():
        for b in range(n_seqs):
            st_ref[b] = ubuf_ref[(b + 1) * stride - (CONV_W - 1):(b + 1) * stride, :]


def _conv_ffn(x, g, w_up, cw, cb, w_dn, prev, *, batch, seq, tm):
    n, d = x.shape
    d_ff = w_dn.shape[0]
    n_seqs = max(tm // seq, 1)
    rows = tm // n_seqs
    nt = seq // rows
    row = pl.BlockSpec((tm, d), lambda b, j: (b * nt + j, 0))
    st = pl.BlockSpec((n_seqs, CONV_W - 1, 2 * d_ff), lambda b, j: (b, 0, 0))
    return pl.pallas_call(
        functools.partial(_ffn_kernel, rows=rows, n_seqs=n_seqs, d_ff=d_ff, n_tiles=nt),
        grid=(batch // n_seqs, nt),
        in_specs=[row, _const_spec(g.shape), _const_spec(w_up.shape), _const_spec(cw.shape),
                  _const_spec(cb.shape), _const_spec(w_dn.shape), st],
        out_specs=[row, st],
        out_shape=(jax.ShapeDtypeStruct((n, d), F32),
                   jax.ShapeDtypeStruct((batch, CONV_W - 1, 2 * d_ff), F32)),
        scratch_shapes=[pltpu.VMEM((n_seqs * (_PAD_ROWS + rows), 2 * d_ff), F32)],
        compiler_params=pltpu.CompilerParams(
            dimension_semantics=("arbitrary", "arbitrary"),
            vmem_limit_bytes=_vmem_limit(56 << 20)),
        name="conv_ffn",
    )(x, g, w_up, cw, cb, w_dn, prev)


def _pack_weights(w_in, w_q_up, w_kv_up, w_o_branch, w_out, w_ffn_up, w_ffn_down):
    head_width = sum(width for _, width in _IN_HEAD)
    w_head = jnp.pad(w_in[:, :_IN_HEAD_SRC_WIDTH].astype(BF16),
                     ((0, 0), (0, head_width - _IN_HEAD_SRC_WIDTH)))
    w_in_p = (w_head, w_in[:, _IN_HEAD_SRC_WIDTH:].astype(BF16))
    w_q_p = w_q_up.T.astype(BF16)
    wkv = w_kv_up.reshape(w_kv_up.shape[0], MLA_HEADS, QK_NOPE + V_HEAD)
    w_kv_p = jnp.concatenate([wkv[:, :, :QK_NOPE].reshape(wkv.shape[0], -1),
                              wkv[:, :, QK_NOPE:].reshape(wkv.shape[0], -1)], axis=1).astype(BF16)
    mla_width = MLA_HEADS * V_HEAD
    return (w_in_p, w_q_p, w_kv_p, w_o_branch[:mla_width].astype(BF16),
            w_o_branch[mla_width:].astype(BF16), w_out.astype(BF16),
            w_ffn_up.astype(BF16), w_ffn_down.astype(BF16))


class _Tiles(NamedTuple):
    tm: int
    up_tm: int
    tq: int
    ret_chunk: int
    ffn_tm: int


_TOKEN_TILE = 512


def _tiles(batch, seq):
    if seq >= _TOKEN_TILE:
        return _Tiles(tm=_TOKEN_TILE, up_tm=2 * _TOKEN_TILE, tq=4 * _ATTN_KEY_BLOCK,
                      ret_chunk=MXU_TILE, ffn_tm=_TOKEN_TILE)
    return _Tiles(tm=batch * seq, up_tm=batch * seq, tq=seq, ret_chunk=seq,
                  ffn_tm=batch * seq)


def _layer(x, pos, past, w, *, batch, seq):
    n = x.shape[0]
    tm, up_tm, tq, ret_chunk, ffn_tm = _tiles(batch, seq)
    tabs64 = _rope_tables(pos, RET_DK // 2)
    tabs32 = _rope_tables(pos, QK_ROPE // 2)
    seq_tiles = max(seq // tm, 1)
    if seq < tm:
        tabs64 = tuple(np.tile(t, (tm // seq, 1)) for t in tabs64)
        tabs32 = tuple(np.tile(t, (tm // seq, 1)) for t in tabs32)
    qlat, ckv, kpe, rq, rk, rv, rg, ga, gb = _in_proj(
        x, w['w_in'], w['g_norm_mix'], w['g_q_lat'], w['g_kv_lat'], w['g_k_rope'],
        tabs64, tabs32, seq_tiles=seq_tiles, tm=tm)
    tabs32_t = tuple(np.ascontiguousarray(t[:, :QK_ROPE].T) for t in tabs32)
    bcast = lambda g: jnp.broadcast_to(g.reshape(-1, 1), (g.size, up_tm))
    qt = _q_up(qlat, w['w_q_up'], bcast(w['g_q_nope']), bcast(w['g_q_rope']), tabs32_t,
               seq_tiles=max(seq // up_tm, 1), tm=up_tm)
    if past is None:
        k_new, vt_new = _kv_up(ckv, kpe, w['w_kv_up'], w['g_k_nope'], tm=up_tm)
        attn = _prompt_attention(qt, k_new, vt_new, batch=batch, seq=seq, tq=tq)
        s0 = jnp.zeros((batch, RET_HEADS, RET_DK, RET_DV), F32)
        prev = jnp.zeros((batch, CONV_W - 1, w['w_ffn_up'].shape[1]), F32)
    else:
        lat_c, pe_c, s0, prev = past
        attn = _sample_attention(qt, lat_c, pe_c, ckv.reshape(batch, seq, -1),
                                 kpe.reshape(batch, seq, -1), w['w_kv_up'], w['g_k_nope'],
                                 batch=batch, new=seq).reshape(n, -1)
    ret, s_fin = _retention(rq, rk, rv, rg, w['g_ret_out'], s0,
                            batch=batch, seq=seq, chunk=ret_chunk)
    x1 = _merge(x, attn, ret, ga, gb, w['w_o_a'], w['w_o_b'], w['w_out'], tm=up_tm)
    y, conv_state = _conv_ffn(x1, w['g_norm_ffn'], w['w_ffn_up'], w['ffn_conv_w'],
                              w['ffn_conv_b'], w['w_ffn_down'], prev,
                              batch=batch, seq=seq, tm=ffn_tm)
    return y, (ckv, kpe, s_fin, conv_state)


def kernel(x_prompt, x_sample, cache_mla_latent, cache_mla_rope_key, state_retention, state_ffn_conv, g_norm_mix, w_in, g_q_lat, w_q_up, g_q_nope, g_q_rope, g_kv_lat, w_kv_up, g_k_nope, g_k_rope, g_ret_out, w_o_branch, w_out, g_norm_ffn, w_ffn_up, ffn_conv_w, ffn_conv_b, w_ffn_down):
    depth = w_in.shape[0]
    assert depth == 1, "single-layer trunk"
    bp, tp, d = x_prompt.shape
    bs, ts, _ = x_sample.shape
    past_len = cache_mla_latent.shape[2]
    (w_in_p, w_q_p, w_kv_p, w_o_a, w_o_b, w_out_p, w_up_p, w_dn_p) = _pack_weights(
        w_in[0], w_q_up[0], w_kv_up[0], w_o_branch[0], w_out[0], w_ffn_up[0], w_ffn_down[0])
    row = lambda g: g.reshape(1, -1).astype(F32)
    w = dict(
        w_in=w_in_p, w_q_up=w_q_p, w_kv_up=w_kv_p, w_o_a=w_o_a, w_o_b=w_o_b, w_out=w_out_p,
        w_ffn_up=w_up_p, w_ffn_down=w_dn_p,
        g_norm_mix=row(g_norm_mix[0]), g_q_lat=row(g_q_lat[0]), g_kv_lat=row(g_kv_lat[0]),
        g_k_rope=jnp.pad(row(g_k_rope[0]), ((0, 0), (0, LANES - QK_ROPE))),
        g_q_nope=g_q_nope[0].astype(F32), g_q_rope=g_q_rope[0].astype(F32),
        g_k_nope=row(g_k_nope[0]), g_ret_out=g_ret_out[0], g_norm_ffn=row(g_norm_ffn[0]),
        ffn_conv_w=ffn_conv_w[0], ffn_conv_b=row(ffn_conv_b[0]))

    yp, sp = _layer(x_prompt.reshape(bp * tp, d), np.arange(tp), None, w, batch=bp, seq=tp)
    past = (cache_mla_latent[0], cache_mla_rope_key[0], state_retention[0], state_ffn_conv[0])
    ys, ss = _layer(x_sample.reshape(bs * ts, d), past_len + np.arange(ts), past, w,
                    batch=bs, seq=ts)

    def states(st, b, t):
        ckv, kpe, s_fin, conv = st
        return (ckv.reshape(1, b, t, -1), kpe.reshape(1, b, t, -1), s_fin[None], conv[None])

    return (yp.reshape(bp, tp, d), ys.reshape(bs, ts, d)) + states(sp, bp, tp) + states(ss, bs, ts)
```
